```python
import math
import jax, jax.numpy as jnp
from jax import lax
import numpy as np

D_MODEL = 2048
BATCH = 4
SEQ = 2048
DEPTH = 2

CHUNK = 64
Q_BLOCK = 128
N_A_LAYERS = DEPTH // 2
N_B_LAYERS = DEPTH - N_A_LAYERS
POOL_WINDOWS = (2, 4, 8, 16)
N_POOL_GROUPS = len(POOL_WINDOWS)
POOL_GROUP_DIM = D_MODEL // N_POOL_GROUPS
HEAD_DIM = 128
N_HEADS = D_MODEL // (2 * HEAD_DIM)
N_MAPS = 2 * N_HEADS
V_HEAD_DIM = 2 * HEAD_DIM
QK_WIDTH = N_MAPS * HEAD_DIM
V_WIDTH = N_HEADS * V_HEAD_DIM
D_FF = ((8 * D_MODEL // 3 + 255) // 256) * 256
NUM_BUCKETS = 32
MAX_DISTANCE = 128
RMS_EPS = 1e-6
SUBLN_EPS = 1e-5

kernel_name = "yoco_pool_diffattn_macaron_trunk"

F32 = jnp.float32


def rms_norm(x, g, eps=RMS_EPS):
    xf = x.astype(F32)
    y = xf * lax.rsqrt(jnp.mean(xf * xf, axis=-1, keepdims=True) + eps)
    return (y * g.astype(F32)).astype(x.dtype)


def swiglu(h, w_gate, w_up, w_down):
    return (jax.nn.silu(h @ w_gate) * (h @ w_up)) @ w_down


def rel_bucket(rel):
    half = NUM_BUCKETS // 2
    max_exact = half // 2
    ret = jnp.where(rel > 0, half, 0)
    n = jnp.abs(rel)
    nf = jnp.maximum(n, 1).astype(F32)
    large = max_exact + (jnp.log(nf / max_exact) / math.log(MAX_DISTANCE / max_exact)
                         * (half - max_exact)).astype(jnp.int32)
    large = jnp.minimum(large, half - 1)
    return ret + jnp.where(n < max_exact, n, large)


def pool_mixer(h, w, scale):
    B, S, _ = h.shape
    hg = h.reshape(B, S, N_POOL_GROUPS, POOL_GROUP_DIM)
    hf = hg.astype(F32)
    cs = lax.cumsum(hf, axis=1)
    t = jnp.arange(S)
    diffs = []
    for gi, win in enumerate(POOL_WINDOWS):
        c = cs[:, :, gi]
        c_prev = jnp.pad(c, ((0, 0), (win, 0), (0, 0)))[:, :S]
        cnt = jnp.minimum(t + 1, win).astype(F32)[None, :, None]
        diffs.append((c - c_prev) / cnt - hf[:, :, gi])
    d = jnp.stack(diffs, axis=2).astype(h.dtype)
    y = jnp.einsum('bsgc,gce->bsge', d, w).reshape(B, S, D_MODEL)
    return y * scale


def shared_kv(x, kv_norm, w_k, w_v):
    B, S, _ = x.shape
    hk = rms_norm(x, kv_norm)
    k = (hk @ w_k).reshape(B, S, N_HEADS, 2, HEAD_DIM)
    v = (hk @ w_v).reshape(B, S, N_HEADS, V_HEAD_DIM)
    return k, v


def diff_attention(h, w_q, w_o, lam, subln_g, k, v, rel_bias, lambda_init):
    B, S, _ = h.shape
    q = (h @ w_q).reshape(B, S, N_HEADS, 2, HEAD_DIM)
    lamf = lam.astype(F32)
    lam_full = (jnp.exp(jnp.sum(lamf[0] * lamf[1])) - jnp.exp(jnp.sum(lamf[2] * lamf[3]))
                + lambda_init)
    scale = HEAD_DIM ** -0.5
    table = rel_bias.astype(F32)
    pos = jnp.arange(S)
    outs = []
    for blk in range(S // Q_BLOCK):
        q0 = blk * Q_BLOCK
        kend = q0 + Q_BLOCK
        qb = q[:, q0:kend]
        kb = k[:, :kend]
        vb = v[:, :kend]
        qpos = pos[q0:kend]
        kpos = pos[:kend]
        s = jnp.einsum('bqhcd,bkhcd->bhcqk', qb, kb, preferred_element_type=F32) * scale
        bias = table[rel_bucket(kpos[None, :] - qpos[:, None])]
        bias = jnp.transpose(bias, (2, 0, 1)).reshape(N_HEADS, 2, Q_BLOCK, kend)
        allowed = (kpos[None, :] // CHUNK) <= (qpos[:, None] // CHUNK)
        s = jnp.where(allowed, s + bias, -jnp.inf)
        p = jax.nn.softmax(s, axis=-1)
        a = p[:, :, 0] - lam_full * p[:, :, 1]
        o = jnp.einsum('bhqk,bkhd->bqhd', a, vb.astype(F32))
        o = o * lax.rsqrt(jnp.mean(o * o, axis=-1, keepdims=True) + SUBLN_EPS)
        o = o * subln_g.astype(F32) * (1.0 - lambda_init)
        outs.append(o)
    o = jnp.concatenate(outs, axis=1).reshape(B, S, V_WIDTH).astype(h.dtype)
    return o @ w_o


def setup_inputs(seed: int = 0) -> dict:
    key = jax.random.key(seed)
    ks = jax.random.split(key, 16)
    nrm = jax.random.normal
    x = nrm(ks[0], (BATCH, SEQ, D_MODEL), F32)
    norm_gains = 1.0 + 0.02 * nrm(ks[1], (DEPTH, 3, 2, D_MODEL), F32)
    ffn_w_gate = nrm(ks[2], (DEPTH, 2, D_MODEL, D_FF), F32) * D_MODEL ** -0.5
    ffn_w_up = nrm(ks[3], (DEPTH, 2, D_MODEL, D_FF), F32) * D_MODEL ** -0.5
    ffn_w_down = nrm(ks[4], (DEPTH, 2, D_FF, D_MODEL), F32) * D_FF ** -0.5
    pool_w = nrm(ks[5], (N_A_LAYERS, N_POOL_GROUPS, POOL_GROUP_DIM, POOL_GROUP_DIM), F32) * POOL_GROUP_DIM ** -0.5
    pool_scale = 1.0 + 0.02 * nrm(ks[6], (N_A_LAYERS, D_MODEL), F32)
    kv_norm = 1.0 + 0.02 * nrm(ks[7], (D_MODEL,), F32)
    w_k = nrm(ks[8], (D_MODEL, QK_WIDTH), F32) * D_MODEL ** -0.5
    w_v = nrm(ks[9], (D_MODEL, V_WIDTH), F32) * D_MODEL ** -0.5
    rel_bias = 0.5 * nrm(ks[10], (NUM_BUCKETS, N_MAPS), F32)
    w_q = nrm(ks[11], (N_B_LAYERS, D_MODEL, QK_WIDTH), F32) * D_MODEL ** -0.5
    w_o = nrm(ks[12], (N_B_LAYERS, V_WIDTH, D_MODEL), F32) * V_WIDTH ** -0.5
    lambdas = 0.1 * nrm(ks[13], (N_B_LAYERS, 4, HEAD_DIM), F32)
    subln_gain = 1.0 + 0.02 * nrm(ks[14], (N_B_LAYERS, V_HEAD_DIM), F32)
    return {"x": x, "norm_gains": norm_gains, "ffn_w_gate": ffn_w_gate, "ffn_w_up": ffn_w_up,
            "ffn_w_down": ffn_w_down, "pool_w": pool_w, "pool_scale": pool_scale,
            "kv_norm": kv_norm, "w_k": w_k, "w_v": w_v, "rel_bias": rel_bias,
            "w_q": w_q, "w_o": w_o, "lambdas": lambdas, "subln_gain": subln_gain}


def reference(x, norm_gains, ffn_w_gate, ffn_w_up, ffn_w_down, pool_w, pool_scale,
              kv_norm, w_k, w_v, rel_bias, w_q, w_o, lambdas, subln_gain):
    k = v = None
    for l in range(DEPTH):
        g = norm_gains[l]
        if l == N_A_LAYERS:
            k, v = shared_kv(x, kv_norm, w_k, w_v)
        h = rms_norm(x, g[0, 0])
        x = x + 0.5 * rms_norm(swiglu(h, ffn_w_gate[l, 0], ffn_w_up[l, 0], ffn_w_down[l, 0]), g[0, 1])
        h = rms_norm(x, g[1, 0])
        if l < N_A_LAYERS:
            m = pool_mixer(h, pool_w[l], pool_scale[l])
        else:
            j = l - N_A_LAYERS
            lambda_init = 0.8 - 0.6 * math.exp(-0.3 * l)
            m = diff_attention(h, w_q[j], w_o[j], lambdas[j], subln_gain[j], k, v, rel_bias, lambda_init)
        x = x + rms_norm(m, g[1, 1])
        h = rms_norm(x, g[2, 0])
        x = x + 0.5 * rms_norm(swiglu(h, ffn_w_gate[l, 1], ffn_w_up[l, 1], ffn_w_down[l, 1]), g[2, 1])
    return x
```

```python
import functools
import math

import numpy as np
import jax
import jax.numpy as jnp
from jax import lax
from jax.experimental import pallas as pl
from jax.experimental.pallas import tpu as pltpu

F32 = jnp.float32
BF16 = jnp.bfloat16

CHUNK = 64
POOL_WINDOWS = (2, 4, 8, 16)
HEAD_DIM = 128
V_HEAD_DIM = 2 * HEAD_DIM
NUM_BUCKETS = 32
MAX_DISTANCE = 128
RMS_EPS = 1e-6
SUBLN_EPS = 1e-5

VMEM_LIMIT_BYTES = 56 * 1024 * 1024
FFN_ROWS = 512
FFN_COLS = 512
PROJ_ROWS = 512
PROJ_COLS = 512
POOL_ROWS = 256
POOL_HALO = 16
ATT_TILE = 256


def _rms(x, gain, eps):
    ms = jnp.mean(x * x, axis=-1, keepdims=True)
    return x * lax.rsqrt(ms + eps) * gain


def _params(*semantics):
    return pltpu.CompilerParams(dimension_semantics=semantics,
                                vmem_limit_bytes=VMEM_LIMIT_BYTES)


def _ffn_kernel(x_ref, gpre_ref, gpost_ref, wg_ref, wu_ref, wd_ref, o_ref, h_ref):
    j = pl.program_id(1)

    @pl.when(j == 0)
    def _():
        h_ref[...] = _rms(x_ref[...], gpre_ref[...], RMS_EPS).astype(BF16)

    h = h_ref[...]
    g = jnp.dot(h, wg_ref[...], preferred_element_type=F32)
    u = jnp.dot(h, wu_ref[...], preferred_element_type=F32)
    a = (g * (1.0 / (1.0 + jnp.exp(-g))) * u).astype(BF16)
    d = jnp.dot(a, wd_ref[...], preferred_element_type=F32)

    @pl.when(j == 0)
    def _():
        o_ref[...] = d

    @pl.when(j > 0)
    def _():
        o_ref[...] += d

    @pl.when(j == pl.num_programs(1) - 1)
    def _():
        o_ref[...] = x_ref[...] + 0.5 * _rms(o_ref[...], gpost_ref[...], RMS_EPS)


def _ffn(x, gpre, gpost, wg, wu, wd):
    m, d = x.shape
    ff = wg.shape[1]
    tm, tf = FFN_ROWS, FFN_COLS
    assert m % tm == 0 and ff % tf == 0
    return pl.pallas_call(
        _ffn_kernel,
        grid=(m // tm, ff // tf),
        in_specs=[
            pl.BlockSpec((tm, d), lambda i, j: (i, 0)),
            pl.BlockSpec((1, d), lambda i, j: (0, 0)),
            pl.BlockSpec((1, d), lambda i, j: (0, 0)),
            pl.BlockSpec((d, tf), lambda i, j: (0, j)),
            pl.BlockSpec((d, tf), lambda i, j: (0, j)),
            pl.BlockSpec((tf, d), lambda i, j: (j, 0)),
        ],
        out_specs=pl.BlockSpec((tm, d), lambda i, j: (i, 0)),
        out_shape=jax.ShapeDtypeStruct((m, d), F32),
        scratch_shapes=[pltpu.VMEM((tm, d), BF16)],
        compiler_params=_params("parallel", "arbitrary"),
        name="ffn",
    )(x, gpre, gpost, wg, wu, wd)


def _pool_kernel(x_ref, halo_ref, gpre_ref, gpost_ref, w_ref, scale_ref, o_ref, hx_ref):
    i = pl.program_id(1)
    ts = x_ref.shape[1]
    gdim = w_ref.shape[1]
    x = x_ref[0]
    h = _rms(x, gpre_ref[...], RMS_EPS)
    hh = _rms(halo_ref[0], gpre_ref[...], RMS_EPS)
    hx_ref[0:POOL_HALO, :] = jnp.where(i > 0, hh, 0.0)
    hx_ref[POOL_HALO:POOL_HALO + ts, :] = h

    t = i * ts + lax.broadcasted_iota(jnp.int32, (ts, 1), 0)
    ys = []
    for gi, win in enumerate(POOL_WINDOWS):
        c0 = gi * gdim
        acc = h[:, c0:c0 + gdim]
        for k in range(1, win):
            acc = acc + hx_ref[POOL_HALO - k:POOL_HALO - k + ts, c0:c0 + gdim]
        inv_cnt = 1.0 / jnp.minimum(t + 1, win).astype(F32)
        dg = acc * inv_cnt - h[:, c0:c0 + gdim]
        ys.append(jnp.dot(dg.astype(BF16), w_ref[gi], preferred_element_type=F32))
    y = jnp.concatenate(ys, axis=-1) * scale_ref[...]
    o_ref[0] = x + _rms(y, gpost_ref[...], RMS_EPS)


def _pool_mixer(x, gpre, gpost, w, scale):
    b, s, d = x.shape
    ts = POOL_ROWS
    assert s % ts == 0 and ts % POOL_HALO == 0
    halo_blocks = ts // POOL_HALO
    return pl.pallas_call(
        _pool_kernel,
        grid=(b, s // ts),
        in_specs=[
            pl.BlockSpec((1, ts, d), lambda bi, i: (bi, i, 0)),
            pl.BlockSpec((1, POOL_HALO, d),
                         lambda bi, i: (bi, jnp.maximum(i * halo_blocks - 1, 0), 0)),
            pl.BlockSpec((1, d), lambda bi, i: (0, 0)),
            pl.BlockSpec((1, d), lambda bi, i: (0, 0)),
            pl.BlockSpec(w.shape, lambda bi, i: (0, 0, 0)),
            pl.BlockSpec((1, d), lambda bi, i: (0, 0)),
        ],
        out_specs=pl.BlockSpec((1, ts, d), lambda bi, i: (bi, i, 0)),
        out_shape=jax.ShapeDtypeStruct((b, s, d), F32),
        scratch_shapes=[pltpu.VMEM((POOL_HALO + ts, d), F32)],
        compiler_params=_params("parallel", "parallel"),
        name="pool_mixer",
    )(x, x, gpre, gpost, w, scale)


def _norm_proj_kernel(n_out, x_ref, g_ref, *refs):
    w_refs, o_refs, h_ref = refs[:n_out], refs[n_out:2 * n_out], refs[2 * n_out]

    @pl.when(pl.program_id(1) == 0)
    def _():
        h_ref[...] = _rms(x_ref[...], g_ref[...], RMS_EPS).astype(BF16)

    h = h_ref[...]
    for w_ref, o_ref in zip(w_refs, o_refs):
        o_ref[...] = jnp.dot(h, w_ref[...], preferred_element_type=F32).astype(o_ref.dtype)


def _norm_proj(x, gain, weights):
    m, d = x.shape
    n = weights[0].shape[1]
    tm, tn = PROJ_ROWS, PROJ_COLS
    assert m % tm == 0 and n % tn == 0
    n_out = len(weights)
    outs = pl.pallas_call(
        functools.partial(_norm_proj_kernel, n_out),
        grid=(m // tm, n // tn),
        in_specs=[pl.BlockSpec((tm, d), lambda i, j: (i, 0)),
                  pl.BlockSpec((1, d), lambda i, j: (0, 0))]
                 + [pl.BlockSpec((d, tn), lambda i, j: (0, j))] * n_out,
        out_specs=[pl.BlockSpec((tm, tn), lambda i, j: (i, j))] * n_out,
        out_shape=[jax.ShapeDtypeStruct((m, n), BF16)] * n_out,
        scratch_shapes=[pltpu.VMEM((tm, d), BF16)],
        compiler_params=_params("parallel", "arbitrary"),
        name="norm_proj",
    )(x, gain, *weights)
    return outs


def _rel_bucket(rel):
    half = NUM_BUCKETS // 2
    max_exact = half // 2
    ret = jnp.where(rel > 0, half, 0)
    n = jnp.abs(rel)
    nf = jnp.maximum(n, 1).astype(F32)
    large = max_exact + (jnp.log(nf / max_exact) / math.log(MAX_DISTANCE / max_exact)
                         * (half - max_exact)).astype(jnp.int32)
    large = jnp.minimum(large, half - 1)
    return ret + jnp.where(n < max_exact, n, large)


FAR_BUCKET = NUM_BUCKETS // 2 - 1
FAR_DISTANCE = MAX_DISTANCE


def _bias_tile_constants(t):
    assert t + 1 >= FAR_DISTANCE and t % CHUNK == 0
    q = np.arange(t, dtype=np.int32)[:, None]
    k = np.arange(t, dtype=np.int32)[None, :]
    buckets = jnp.stack([_rel_bucket(jnp.asarray(k - q)), _rel_bucket(jnp.asarray(k - t - q))])
    allowed = np.stack([(k // CHUNK) <= (q // CHUNK), np.ones((t, t), bool)])
    mask = np.where(allowed, 0.0, -np.inf).astype(np.float32)
    return buckets, jnp.asarray(mask)


def _bias_kernel(table_ref, bucket_ref, mask_ref, o_ref):
    m = pl.program_id(0)
    bucket = bucket_ref[...]
    acc = mask_ref[...]
    for b in range(NUM_BUCKETS):
        acc = acc + jnp.where(bucket == b, table_ref[b, m], 0.0)
    o_ref[0] = acc


def _bias_tiles(rel_bias, t):
    buckets, mask = _bias_tile_constants(t)
    n_maps = rel_bias.shape[1]
    return pl.pallas_call(
        _bias_kernel,
        grid=(n_maps,),
        in_specs=[
            pl.BlockSpec(memory_space=pltpu.SMEM),
            pl.BlockSpec((2, t, t), lambda m: (0, 0, 0)),
            pl.BlockSpec((2, t, t), lambda m: (0, 0, 0)),
        ],
        out_specs=pl.BlockSpec((1, 2, t, t), lambda m: (m, 0, 0, 0)),
        out_shape=jax.ShapeDtypeStruct((n_maps, 2, t, t), F32),
        compiler_params=_params("parallel"),
        name="bias_tiles",
    )(rel_bias, buckets, mask)


def _attn_kernel(lambda_init, q_ref, k_ref, v_ref, bias_ref, table_ref, lam_ref,
                 subg_ref, o_ref, m_ref, l_ref, acc_ref):
    head = pl.program_id(1)
    qi = pl.program_id(2)
    t = q_ref.shape[1]
    scale = HEAD_DIM ** -0.5

    m_ref[...] = jnp.full(m_ref.shape, -jnp.inf, F32)
    l_ref[...] = jnp.zeros(l_ref.shape, F32)
    acc_ref[...] = jnp.zeros(acc_ref.shape, F32)

    def step(kj, bias_of_map):
        k0 = pl.multiple_of(kj * t, t)
        kt = k_ref[0, pl.ds(k0, t), :]
        vt = v_ref[0, pl.ds(k0, t), :]
        for c in range(2):
            qc = q_ref[0, :, c * HEAD_DIM:(c + 1) * HEAD_DIM]
            kc = kt[:, c * HEAD_DIM:(c + 1) * HEAD_DIM]
            s = lax.dot_general(qc, kc, (((1,), (1,)), ((), ())), preferred_element_type=F32)
            s = s * scale + bias_of_map(c)
            m_prev = m_ref[c]
            m_new = jnp.maximum(m_prev, jnp.max(s, axis=-1, keepdims=True))
            alpha = jnp.exp(m_prev - m_new)
            p = jnp.exp(s - m_new)
            l_ref[c] = alpha * l_ref[c] + jnp.sum(p, axis=-1, keepdims=True)
            acc_ref[c] = alpha * acc_ref[c] + jnp.dot(p.astype(BF16), vt, preferred_element_type=F32)
            m_ref[c] = m_new

    def far_body(kj, carry):
        step(kj, lambda c: table_ref[FAR_BUCKET, 2 * head + c])
        return carry

    lax.fori_loop(0, jnp.maximum(qi - 1, 0), far_body, 0)

    @pl.when(qi > 0)
    def _():
        step(qi - 1, lambda c: bias_ref[c, 1])

    step(qi, lambda c: bias_ref[c, 0])

    lam = lam_ref[...]
    lam_full = (jnp.exp(jnp.sum(lam[0:1] * lam[1:2], keepdims=True))
                - jnp.exp(jnp.sum(lam[2:3] * lam[3:4], keepdims=True)) + lambda_init)
    o = acc_ref[0] * (1.0 / l_ref[0]) - lam_full * (acc_ref[1] * (1.0 / l_ref[1]))
    o = o * lax.rsqrt(jnp.mean(o * o, axis=-1, keepdims=True) + SUBLN_EPS)
    o_ref[0] = (o * subg_ref[...] * (1.0 - lambda_init)).astype(o_ref.dtype)


def _attention(q, k, v, bias_tiles, rel_bias, lam, subln_g, lambda_init):
    b, s, width = q.shape
    t = ATT_TILE
    n_heads = width // V_HEAD_DIM
    assert s % t == 0
    return pl.pallas_call(
        functools.partial(_attn_kernel, lambda_init),
        grid=(b, n_heads, s // t),
        in_specs=[
            pl.BlockSpec((1, t, V_HEAD_DIM), lambda bi, h, qi: (bi, qi, h)),
            pl.BlockSpec((1, s, V_HEAD_DIM), lambda bi, h, qi: (bi, 0, h)),
            pl.BlockSpec((1, s, V_HEAD_DIM), lambda bi, h, qi: (bi, 0, h)),
            pl.BlockSpec((2, 2, t, t), lambda bi, h, qi: (h, 0, 0, 0)),
            pl.BlockSpec(memory_space=pltpu.SMEM),
            pl.BlockSpec(lam.shape, lambda bi, h, qi: (0, 0)),
            pl.BlockSpec((1, V_HEAD_DIM), lambda bi, h, qi: (0, 0)),
        ],
        out_specs=pl.BlockSpec((1, t, V_HEAD_DIM), lambda bi, h, qi: (bi, qi, h)),
        out_shape=jax.ShapeDtypeStruct((b, s, width), BF16),
        scratch_shapes=[pltpu.VMEM((2, t, 1), F32), pltpu.VMEM((2, t, 1), F32),
                        pltpu.VMEM((2, t, V_HEAD_DIM), F32)],
        compiler_params=_params("parallel", "parallel", "arbitrary"),
        name="diff_attention",
    )(q, k, v, bias_tiles, rel_bias, lam, subln_g)


def _out_proj_kernel(o_ref, w_ref, x_ref, g_ref, y_ref):
    m = jnp.dot(o_ref[...], w_ref[...], preferred_element_type=F32)
    y_ref[...] = x_ref[...] + _rms(m, g_ref[...], RMS_EPS)


def _out_proj(o, w, x, gain):
    m, d = x.shape
    kdim = o.shape[1]
    tm = PROJ_ROWS
    assert m % tm == 0
    return pl.pallas_call(
        _out_proj_kernel,
        grid=(m // tm,),
        in_specs=[
            pl.BlockSpec((tm, kdim), lambda i: (i, 0)),
            pl.BlockSpec((kdim, d), lambda i: (0, 0)),
            pl.BlockSpec((tm, d), lambda i: (i, 0)),
            pl.BlockSpec((1, d), lambda i: (0, 0)),
        ],
        out_specs=pl.BlockSpec((tm, d), lambda i: (i, 0)),
        out_shape=jax.ShapeDtypeStruct((m, d), F32),
        compiler_params=_params("parallel"),
        name="out_proj",
    )(o, w, x, gain)


def kernel(x, norm_gains, ffn_w_gate, ffn_w_up, ffn_w_down, pool_w, pool_scale, kv_norm, w_k, w_v,
           rel_bias, w_q, w_o, lambdas, subln_gain):
    b, s, d = x.shape
    depth = norm_gains.shape[0]
    n_a = pool_w.shape[0]
    gains = norm_gains.reshape(depth, 3, 2, 1, d)
    wg, wu, wd = (w.astype(BF16) for w in (ffn_w_gate, ffn_w_up, ffn_w_down))

    xf = x.reshape(b * s, d)
    k = v = bias_tiles = None
    for l in range(depth):
        g = gains[l]
        if l == n_a:
            k, v = _norm_proj(xf, kv_norm.reshape(1, d), [w_k.astype(BF16), w_v.astype(BF16)])
            bias_tiles = _bias_tiles(rel_bias, ATT_TILE)
        xf = _ffn(xf, g[0, 0], g[0, 1], wg[l, 0], wu[l, 0], wd[l, 0])
        if l < n_a:
            xf = _pool_mixer(xf.reshape(b, s, d), g[1, 0], g[1, 1], pool_w[l].astype(BF16),
                             pool_scale[l].reshape(1, d)).reshape(b * s, d)
        else:
            j = l - n_a
            lambda_init = 0.8 - 0.6 * math.exp(-0.3 * l)
            (q,) = _norm_proj(xf, g[1, 0], [w_q[j].astype(BF16)])
            o = _attention(q.reshape(b, s, -1), k.reshape(b, s, -1), v.reshape(b, s, -1), bias_tiles,
                           rel_bias, lambdas[j], subln_gain[j].reshape(1, -1), lambda_init)
            xf = _out_proj(o.reshape(b * s, -1), w_o[j].astype(BF16), xf, g[1, 1])
        xf = _ffn(xf, g[2, 0], g[2, 1], wg[l, 1], wu[l, 1], wd[l, 1])
    return xf.reshape(b, s, d)
```

```python
import functools
import math

import numpy as np
import jax
import jax.numpy as jnp
from jax import lax
from jax.experimental import pallas as pl
from jax.experimental.pallas import tpu as pltpu

F32 = jnp.float32
BF16 = jnp.bfloat16

CHUNK = 64
POOL_WINDOWS = (2, 4, 8, 16)
HEAD_DIM = 128
V_HEAD_DIM = 2 * HEAD_DIM
NUM_BUCKETS = 32
MAX_DISTANCE = 128
RMS_EPS = 1e-6
SUBLN_EPS = 1e-5

VMEM_LIMIT_BYTES = 56 * 1024 * 1024
FFN_ROWS = 512
FFN_COLS = 512
PROJ_ROWS = 512
PROJ_COLS = 512
POOL_ROWS = 256
POOL_HALO = 16
ATT_TILE = 256


def _rms(x, gain, eps):
    ms = jnp.mean(x * x, axis=-1, keepdims=True)
    return x * lax.rsqrt(ms + eps) * gain


def _params(*semantics):
    return pltpu.CompilerParams(dimension_semantics=semantics,
                                vmem_limit_bytes=VMEM_LIMIT_BYTES)


def _ffn_kernel(x_ref, gpre_ref, gpost_ref, wg_ref, wu_ref, wd_ref, o_ref, h_ref):
    j = pl.program_id(1)

    @pl.when(j == 0)
    def _():
        h_ref[...] = _rms(x_ref[...], gpre_ref[...], RMS_EPS).astype(BF16)

    h = h_ref[...]
    g = jnp.dot(h, wg_ref[...], preferred_element_type=F32)
    u = jnp.dot(h, wu_ref[...], preferred_element_type=F32)
    a = (g * (1.0 / (1.0 + jnp.exp(-g))) * u).astype(BF16)
    d = jnp.dot(a, wd_ref[...], preferred_element_type=F32)

    @pl.when(j == 0)
    def _():
        o_ref[...] = d

    @pl.when(j > 0)
    def _():
        o_ref[...] += d

    @pl.when(j == pl.num_programs(1) - 1)
    def _():
        o_ref[...] = x_ref[...] + 0.5 * _rms(o_ref[...], gpost_ref[...], RMS_EPS)


def _ffn(x, gpre, gpost, wg, wu, wd, layer, half):
    m, d = x.shape
    ff = wg.shape[-1]
    tm, tf = FFN_ROWS, FFN_COLS
    assert m % tm == 0 and ff % tf == 0
    return pl.pallas_call(
        _ffn_kernel,
        grid=(m // tm, ff // tf),
        in_specs=[
            pl.BlockSpec((tm, d), lambda i, j: (i, 0)),
            pl.BlockSpec((1, d), lambda i, j: (0, 0)),
            pl.BlockSpec((1, d), lambda i, j: (0, 0)),
            pl.BlockSpec((None, None, d, tf), lambda i, j: (layer, half, 0, j)),
            pl.BlockSpec((None, None, d, tf), lambda i, j: (layer, half, 0, j)),
            pl.BlockSpec((None, None, tf, d), lambda i, j: (layer, half, j, 0)),
        ],
        out_specs=pl.BlockSpec((tm, d), lambda i, j: (i, 0)),
        out_shape=jax.ShapeDtypeStruct((m, d), F32),
        scratch_shapes=[pltpu.VMEM((tm, d), BF16)],
        compiler_params=_params("parallel", "arbitrary"),
        name="ffn",
    )(x, gpre, gpost, wg, wu, wd)


def _pool_kernel(x_ref, halo_ref, gpre_ref, gpost_ref, w_ref, scale_ref, o_ref, hx_ref):
    i = pl.program_id(1)
    ts = x_ref.shape[1]
    gdim = w_ref.shape[1]
    x = x_ref[0]
    h = _rms(x, gpre_ref[...], RMS_EPS)
    hh = _rms(halo_ref[0], gpre_ref[...], RMS_EPS)
    hx_ref[0:POOL_HALO, :] = jnp.where(i > 0, hh, 0.0)
    hx_ref[POOL_HALO:POOL_HALO + ts, :] = h

    t = i * ts + lax.broadcasted_iota(jnp.int32, (ts, 1), 0)
    ys = []
    for gi, win in enumerate(POOL_WINDOWS):
        c0 = gi * gdim
        acc = h[:, c0:c0 + gdim]
        for k in range(1, win):
            acc = acc + hx_ref[POOL_HALO - k:POOL_HALO - k + ts, c0:c0 + gdim]
        inv_cnt = 1.0 / jnp.minimum(t + 1, win).astype(F32)
        dg = acc * inv_cnt - h[:, c0:c0 + gdim]
        ys.append(jnp.dot(dg.astype(BF16), w_ref[gi], preferred_element_type=F32))
    y = jnp.concatenate(ys, axis=-1) * scale_ref[...]
    o_ref[0] = x + _rms(y, gpost_ref[...], RMS_EPS)


def _pool_mixer(x, gpre, gpost, w, scale):
    b, s, d = x.shape
    ts = POOL_ROWS
    assert s % ts == 0 and ts % POOL_HALO == 0
    halo_blocks = ts // POOL_HALO
    return pl.pallas_call(
        _pool_kernel,
        grid=(b, s // ts),
        in_specs=[
            pl.BlockSpec((1, ts, d), lambda bi, i: (bi, i, 0)),
            pl.BlockSpec((1, POOL_HALO, d),
                         lambda bi, i: (bi, jnp.maximum(i * halo_blocks - 1, 0), 0)),
            pl.BlockSpec((1, d), lambda bi, i: (0, 0)),
            pl.BlockSpec((1, d), lambda bi, i: (0, 0)),
            pl.BlockSpec(w.shape, lambda bi, i: (0, 0, 0)),
            pl.BlockSpec((1, d), lambda bi, i: (0, 0)),
        ],
        out_specs=pl.BlockSpec((1, ts, d), lambda bi, i: (bi, i, 0)),
        out_shape=jax.ShapeDtypeStruct((b, s, d), F32),
        scratch_shapes=[pltpu.VMEM((POOL_HALO + ts, d), F32)],
        compiler_params=_params("parallel", "parallel"),
        name="pool_mixer",
    )(x, x, gpre, gpost, w, scale)


def _norm_proj_kernel(n_out, x_ref, g_ref, *refs):
    w_refs, o_refs, h_ref = refs[:n_out], refs[n_out:2 * n_out], refs[2 * n_out]

    @pl.when(pl.program_id(1) == 0)
    def _():
        h_ref[...] = _rms(x_ref[...], g_ref[...], RMS_EPS).astype(BF16)

    h = h_ref[...]
    for w_ref, o_ref in zip(w_refs, o_refs):
        o_ref[...] = jnp.dot(h, w_ref[...], preferred_element_type=F32).astype(o_ref.dtype)


def _norm_proj(x, gain, weights):
    m, d = x.shape
    n = weights[0].shape[1]
    tm, tn = PROJ_ROWS, PROJ_COLS
    assert m % tm == 0 and n % tn == 0
    n_out = len(weights)
    outs = pl.pallas_call(
        functools.partial(_norm_proj_kernel, n_out),
        grid=(m // tm, n // tn),
        in_specs=[pl.BlockSpec((tm, d), lambda i, j: (i, 0)),
                  pl.BlockSpec((1, d), lambda i, j: (0, 0))]
                 + [pl.BlockSpec((d, tn), lambda i, j: (0, j))] * n_out,
        out_specs=[pl.BlockSpec((tm, tn), lambda i, j: (i, j))] * n_out,
        out_shape=[jax.ShapeDtypeStruct((m, n), BF16)] * n_out,
        scratch_shapes=[pltpu.VMEM((tm, d), BF16)],
        compiler_params=_params("parallel", "arbitrary"),
        name="norm_proj",
    )(x, gain, *weights)
    return outs


def _rel_bucket(rel):
    half = NUM_BUCKETS // 2
    max_exact = half // 2
    ret = jnp.where(rel > 0, half, 0)
    n = jnp.abs(rel)
    nf = jnp.maximum(n, 1).astype(F32)
    large = max_exact + (jnp.log(nf / max_exact) / math.log(MAX_DISTANCE / max_exact)
                         * (half - max_exact)).astype(jnp.int32)
    large = jnp.minimum(large, half - 1)
    return ret + jnp.where(n < max_exact, n, large)


FAR_BUCKET = NUM_BUCKETS // 2 - 1
FAR_DISTANCE = MAX_DISTANCE


def _bias_tile_constants(t):
    assert t + 1 >= FAR_DISTANCE and t % CHUNK == 0
    q = np.arange(t, dtype=np.int32)[:, None]
    k = np.arange(t, dtype=np.int32)[None, :]
    buckets = jnp.stack([_rel_bucket(jnp.asarray(k - q)), _rel_bucket(jnp.asarray(k - t - q))])
    allowed = np.stack([(k // CHUNK) <= (q // CHUNK), np.ones((t, t), bool)])
    mask = np.where(allowed, 0.0, -np.inf).astype(np.float32)
    return buckets, jnp.asarray(mask)


def _bias_kernel(table_ref, bucket_ref, mask_ref, o_ref):
    m = pl.program_id(0)
    bucket = bucket_ref[...]
    acc = mask_ref[...]
    for b in range(NUM_BUCKETS):
        acc = acc + jnp.where(bucket == b, table_ref[b, m], 0.0)
    o_ref[0] = acc


def _bias_tiles(rel_bias, t):
    buckets, mask = _bias_tile_constants(t)
    n_maps = rel_bias.shape[1]
    return pl.pallas_call(
        _bias_kernel,
        grid=(n_maps,),
        in_specs=[
            pl.BlockSpec(memory_space=pltpu.SMEM),
            pl.BlockSpec((2, t, t), lambda m: (0, 0, 0)),
            pl.BlockSpec((2, t, t), lambda m: (0, 0, 0)),
        ],
        out_specs=pl.BlockSpec((1, 2, t, t), lambda m: (m, 0, 0, 0)),
        out_shape=jax.ShapeDtypeStruct((n_maps, 2, t, t), F32),
        compiler_params=_params("parallel"),
        name="bias_tiles",
    )(rel_bias, buckets, mask)


def _softmax_pv(qi, qc, k_ref, v_ref, cols, far_bias, near_bias, diag_bias):
    t = qc.shape[0]
    scale = HEAD_DIM ** -0.5
    pieces = []
    if qi >= 2:
        pieces.append((0, (qi - 1) * t, far_bias))
    if qi >= 1:
        pieces.append(((qi - 1) * t, t, near_bias))
    pieces.append((qi * t, t, diag_bias))

    scores = []
    for start, size, bias in pieces:
        kc = k_ref[0, start:start + size, cols]
        s = lax.dot_general(qc, kc, (((1,), (1,)), ((), ())), preferred_element_type=F32)
        scores.append(s * scale + bias)
    m = functools.reduce(jnp.maximum, [jnp.max(s, axis=-1, keepdims=True) for s in scores])
    probs = [jnp.exp(s - m) for s in scores]
    denom = sum(jnp.sum(p, axis=-1, keepdims=True) for p in probs)
    acc = sum(jnp.dot(p.astype(BF16), v_ref[0, start:start + size, :], preferred_element_type=F32)
              for p, (start, size, _) in zip(probs, pieces))
    return acc * (1.0 / denom)


def _attn_kernel(lambda_init, q_ref, k_ref, v_ref, bias_ref, table_ref, lam_ref, subg_ref, o_ref):
    head = pl.program_id(1)
    qi = pl.program_id(2)
    t = q_ref.shape[1]

    lam = lam_ref[...]
    lam_full = (jnp.exp(jnp.sum(lam[0:1] * lam[1:2], keepdims=True))
                - jnp.exp(jnp.sum(lam[2:3] * lam[3:4], keepdims=True)) + lambda_init)

    for i in range(k_ref.shape[1] // t):
        @pl.when(qi == i)
        def _(i=i):
            maps = []
            for c in range(2):
                cols = slice(c * HEAD_DIM, (c + 1) * HEAD_DIM)
                maps.append(_softmax_pv(i, q_ref[0, :, cols], k_ref, v_ref, cols,
                                        table_ref[FAR_BUCKET, 2 * head + c],
                                        bias_ref[c, 1], bias_ref[c, 0]))
            o = maps[0] - lam_full * maps[1]
            o = o * lax.rsqrt(jnp.mean(o * o, axis=-1, keepdims=True) + SUBLN_EPS)
            o_ref[0] = (o * subg_ref[...] * (1.0 - lambda_init)).astype(o_ref.dtype)


def _attention(q, k, v, bias_tiles, rel_bias, lam, subln_g, lambda_init):
    b, s, width = q.shape
    t = ATT_TILE
    n_heads = width // V_HEAD_DIM
    assert s % t == 0
    return pl.pallas_call(
        functools.partial(_attn_kernel, lambda_init),
        grid=(b, n_heads, s // t),
        in_specs=[
            pl.BlockSpec((1, t, V_HEAD_DIM), lambda bi, h, qi: (bi, qi, h)),
            pl.BlockSpec((1, s, V_HEAD_DIM), lambda bi, h, qi: (bi, 0, h)),
            pl.BlockSpec((1, s, V_HEAD_DIM), lambda bi, h, qi: (bi, 0, h)),
            pl.BlockSpec((2, 2, t, t), lambda bi, h, qi: (h, 0, 0, 0)),
            pl.BlockSpec(memory_space=pltpu.SMEM),
            pl.BlockSpec(lam.shape, lambda bi, h, qi: (0, 0)),
            pl.BlockSpec((1, V_HEAD_DIM), lambda bi, h, qi: (0, 0)),
        ],
        out_specs=pl.BlockSpec((1, t, V_HEAD_DIM), lambda bi, h, qi: (bi, qi, h)),
        out_shape=jax.ShapeDtypeStruct((b, s, width), BF16),
        compiler_params=_params("parallel", "parallel", "arbitrary"),
        name="diff_attention",
    )(q, k, v, bias_tiles, rel_bias, lam, subln_g)


def _out_proj_kernel(o_ref, w_ref, x_ref, g_ref, y_ref):
    m = jnp.dot(o_ref[...], w_ref[...], preferred_element_type=F32)
    y_ref[...] = x_ref[...] + _rms(m, g_ref[...], RMS_EPS)


def _out_proj(o, w, x, gain):
    m, d = x.shape
    kdim = o.shape[1]
    tm = PROJ_ROWS
    assert m % tm == 0
    return pl.pallas_call(
        _out_proj_kernel,
        grid=(m // tm,),
        in_specs=[
            pl.BlockSpec((tm, kdim), lambda i: (i, 0)),
            pl.BlockSpec((kdim, d), lambda i: (0, 0)),
            pl.BlockSpec((tm, d), lambda i: (i, 0)),
            pl.BlockSpec((1, d), lambda i: (0, 0)),
        ],
        out_specs=pl.BlockSpec((tm, d), lambda i: (i, 0)),
        out_shape=jax.ShapeDtypeStruct((m, d), F32),
        compiler_params=_params("parallel"),
        name="out_proj",
    )(o, w, x, gain)


def kernel(x, norm_gains, ffn_w_gate, ffn_w_up, ffn_w_down, pool_w, pool_scale, kv_norm, w_k, w_v,
           rel_bias, w_q, w_o, lambdas, subln_gain):
    b, s, d = x.shape
    depth = norm_gains.shape[0]
    n_a = pool_w.shape[0]
    gains = norm_gains.reshape(depth, 3, 2, 1, d)
    wg, wu, wd = (w.astype(BF16) for w in (ffn_w_gate, ffn_w_up, ffn_w_down))

    xf = x.reshape(b * s, d)
    k = v = bias_tiles = None
    for l in range(depth):
        g = gains[l]
        if l == n_a:
            k, v = _norm_proj(xf, kv_norm.reshape(1, d), [w_k.astype(BF16), w_v.astype(BF16)])
            bias_tiles = _bias_tiles(rel_bias, ATT_TILE)
        xf = _ffn(xf, g[0, 0], g[0, 1], wg, wu, wd, l, 0)
        if l < n_a:
            xf = _pool_mixer(xf.reshape(b, s, d), g[1, 0], g[1, 1], pool_w[l].astype(BF16),
                             pool_scale[l].reshape(1, d)).reshape(b * s, d)
        else:
            j = l - n_a
            lambda_init = 0.8 - 0.6 * math.exp(-0.3 * l)
            (q,) = _norm_proj(xf, g[1, 0], [w_q[j].astype(BF16)])
            o = _attention(q.reshape(b, s, -1), k.reshape(b, s, -1), v.reshape(b, s, -1), bias_tiles,
                           rel_bias, lambdas[j], subln_gain[j].reshape(1, -1), lambda_init)
            xf = _out_proj(o.reshape(b * s, -1), w_o[j].astype(BF16), xf, g[1, 1])
        xf = _ffn(xf, g[2, 0], g[2, 1], wg, wu, wd, l, 1)
    return xf.reshape(b, s, d)
```

```python
import functools
import math

import numpy as np
import jax
import jax.numpy as jnp
from jax import lax
from jax.experimental import pallas as pl
from jax.experimental.pallas import tpu as pltpu

F32 = jnp.float32
BF16 = jnp.bfloat16

CHUNK = 64
POOL_WINDOWS = (2, 4, 8, 16)
HEAD_DIM = 128
V_HEAD_DIM = 2 * HEAD_DIM
NUM_BUCKETS = 32
MAX_DISTANCE = 128
RMS_EPS = 1e-6
SUBLN_EPS = 1e-5

VMEM_LIMIT_BYTES = 56 * 1024 * 1024
FFN_ROWS = 1024
FFN_COLS = 512
FFN_DOWN_COLS = 512
PROJ_ROWS = 512
PROJ_COLS = 512
POOL_ROWS = 256
POOL_HALO = 16
ATT_TILE = 256


def _rms(x, gain, eps):
    ms = jnp.mean(x * x, axis=-1, keepdims=True)
    return x * lax.rsqrt(ms + eps) * gain


def _params(*semantics):
    return pltpu.CompilerParams(dimension_semantics=semantics,
                                vmem_limit_bytes=VMEM_LIMIT_BYTES)


def _ffn_kernel(x_ref, gpre_ref, gpost_ref, wg_ref, wu_ref, wd_ref, o_ref, h_ref):
    j = pl.program_id(1)

    @pl.when(j == 0)
    def _():
        h_ref[...] = _rms(x_ref[...], gpre_ref[...], RMS_EPS).astype(BF16)
        o_ref[...] = jnp.zeros(o_ref.shape, F32)

    h = h_ref[...]
    g = jnp.dot(h, wg_ref[...], preferred_element_type=F32)
    u = jnp.dot(h, wu_ref[...], preferred_element_type=F32)
    a = (g * (1.0 / (1.0 + jnp.exp(-g))) * u).astype(BF16)
    for c0 in range(0, o_ref.shape[1], FFN_DOWN_COLS):
        cols = slice(c0, c0 + FFN_DOWN_COLS)
        o_ref[:, cols] += jnp.dot(a, wd_ref[:, cols], preferred_element_type=F32)

    @pl.when(j == pl.num_programs(1) - 1)
    def _():
        o_ref[...] = x_ref[...] + 0.5 * _rms(o_ref[...], gpost_ref[...], RMS_EPS)


def _ffn(x, gpre, gpost, wg, wu, wd, layer, half):
    m, d = x.shape
    ff = wg.shape[-1]
    tm, tf = FFN_ROWS, FFN_COLS
    assert m % tm == 0 and ff % tf == 0
    return pl.pallas_call(
        _ffn_kernel,
        grid=(m // tm, ff // tf),
        in_specs=[
            pl.BlockSpec((tm, d), lambda i, j: (i, 0), pipeline_mode=pl.Buffered(1)),
            pl.BlockSpec((1, d), lambda i, j: (0, 0)),
            pl.BlockSpec((1, d), lambda i, j: (0, 0)),
            pl.BlockSpec((None, None, d, tf), lambda i, j: (layer, half, 0, j)),
            pl.BlockSpec((None, None, d, tf), lambda i, j: (layer, half, 0, j)),
            pl.BlockSpec((None, None, tf, d), lambda i, j: (layer, half, j, 0)),
        ],
        out_specs=pl.BlockSpec((tm, d), lambda i, j: (i, 0)),
        out_shape=jax.ShapeDtypeStruct((m, d), F32),
        scratch_shapes=[pltpu.VMEM((tm, d), BF16)],
        compiler_params=_params("parallel", "arbitrary"),
        name="ffn",
    )(x, gpre, gpost, wg, wu, wd)


def _pool_kernel(x_ref, halo_ref, gpre_ref, gpost_ref, w_ref, scale_ref, o_ref, hx_ref):
    i = pl.program_id(1)
    ts = x_ref.shape[1]
    gdim = w_ref.shape[1]
    x = x_ref[0]
    h = _rms(x, gpre_ref[...], RMS_EPS)
    hh = _rms(halo_ref[0], gpre_ref[...], RMS_EPS)
    hx_ref[0:POOL_HALO, :] = jnp.where(i > 0, hh, 0.0)
    hx_ref[POOL_HALO:POOL_HALO + ts, :] = h

    t = i * ts + lax.broadcasted_iota(jnp.int32, (ts, 1), 0)
    ys = []
    for gi, win in enumerate(POOL_WINDOWS):
        c0 = gi * gdim
        acc = h[:, c0:c0 + gdim]
        for k in range(1, win):
            acc = acc + hx_ref[POOL_HALO - k:POOL_HALO - k + ts, c0:c0 + gdim]
        inv_cnt = 1.0 / jnp.minimum(t + 1, win).astype(F32)
        dg = acc * inv_cnt - h[:, c0:c0 + gdim]
        ys.append(jnp.dot(dg.astype(BF16), w_ref[gi], preferred_element_type=F32))
    y = jnp.concatenate(ys, axis=-1) * scale_ref[...]
    o_ref[0] = x + _rms(y, gpost_ref[...], RMS_EPS)


def _pool_mixer(x, gpre, gpost, w, scale):
    b, s, d = x.shape
    ts = POOL_ROWS
    assert s % ts == 0 and ts % POOL_HALO == 0
    halo_blocks = ts // POOL_HALO
    return pl.pallas_call(
        _pool_kernel,
        grid=(b, s // ts),
        in_specs=[
            pl.BlockSpec((1, ts, d), lambda bi, i: (bi, i, 0)),
            pl.BlockSpec((1, POOL_HALO, d),
                         lambda bi, i: (bi, jnp.maximum(i * halo_blocks - 1, 0), 0)),
            pl.BlockSpec((1, d), lambda bi, i: (0, 0)),
            pl.BlockSpec((1, d), lambda bi, i: (0, 0)),
            pl.BlockSpec(w.shape, lambda bi, i: (0, 0, 0)),
            pl.BlockSpec((1, d), lambda bi, i: (0, 0)),
        ],
        out_specs=pl.BlockSpec((1, ts, d), lambda bi, i: (bi, i, 0)),
        out_shape=jax.ShapeDtypeStruct((b, s, d), F32),
        scratch_shapes=[pltpu.VMEM((POOL_HALO + ts, d), F32)],
        compiler_params=_params("parallel", "parallel"),
        name="pool_mixer",
    )(x, x, gpre, gpost, w, scale)


def _norm_proj_kernel(n_out, x_ref, g_ref, *refs):
    w_refs, o_refs, h_ref = refs[:n_out], refs[n_out:2 * n_out], refs[2 * n_out]

    @pl.when(pl.program_id(1) == 0)
    def _():
        h_ref[...] = _rms(x_ref[...], g_ref[...], RMS_EPS).astype(BF16)

    h = h_ref[...]
    for w_ref, o_ref in zip(w_refs, o_refs):
        o_ref[...] = jnp.dot(h, w_ref[...], preferred_element_type=F32).astype(o_ref.dtype)


def _norm_proj(x, gain, weights):
    m, d = x.shape
    n = weights[0].shape[1]
    tm, tn = PROJ_ROWS, PROJ_COLS
    assert m % tm == 0 and n % tn == 0
    n_out = len(weights)
    outs = pl.pallas_call(
        functools.partial(_norm_proj_kernel, n_out),
        grid=(m // tm, n // tn),
        in_specs=[pl.BlockSpec((tm, d), lambda i, j: (i, 0)),
                  pl.BlockSpec((1, d), lambda i, j: (0, 0))]
                 + [pl.BlockSpec((d, tn), lambda i, j: (0, j))] * n_out,
        out_specs=[pl.BlockSpec((tm, tn), lambda i, j: (i, j))] * n_out,
        out_shape=[jax.ShapeDtypeStruct((m, n), BF16)] * n_out,
        scratch_shapes=[pltpu.VMEM((tm, d), BF16)],
        compiler_params=_params("parallel", "arbitrary"),
        name="norm_proj",
    )(x, gain, *weights)
    return outs


def _rel_bucket(rel):
    half = NUM_BUCKETS // 2
    max_exact = half // 2
    ret = jnp.where(rel > 0, half, 0)
    n = jnp.abs(rel)
    nf = jnp.maximum(n, 1).astype(F32)
    large = max_exact + (jnp.log(nf / max_exact) / math.log(MAX_DISTANCE / max_exact)
                         * (half - max_exact)).astype(jnp.int32)
    large = jnp.minimum(large, half - 1)
    return ret + jnp.where(n < max_exact, n, large)


FAR_BUCKET = NUM_BUCKETS // 2 - 1
FAR_DISTANCE = MAX_DISTANCE


def _bias_tile_constants(t):
    assert t + 1 >= FAR_DISTANCE and t % CHUNK == 0
    q = np.arange(t, dtype=np.int32)[:, None]
    k = np.arange(t, dtype=np.int32)[None, :]
    buckets = jnp.stack([_rel_bucket(jnp.asarray(k - q)), _rel_bucket(jnp.asarray(k - t - q))])
    allowed = np.stack([(k // CHUNK) <= (q // CHUNK), np.ones((t, t), bool)])
    mask = np.where(allowed, 0.0, -np.inf).astype(np.float32)
    return buckets, jnp.asarray(mask)


def _bias_kernel(table_ref, bucket_ref, mask_ref, o_ref):
    m = pl.program_id(0)
    bucket = bucket_ref[...]
    acc = mask_ref[...]
    for b in range(NUM_BUCKETS):
        acc = acc + jnp.where(bucket == b, table_ref[b, m], 0.0)
    o_ref[0] = acc


def _bias_tiles(rel_bias, t):
    buckets, mask = _bias_tile_constants(t)
    n_maps = rel_bias.shape[1]
    return pl.pallas_call(
        _bias_kernel,
        grid=(n_maps,),
        in_specs=[
            pl.BlockSpec(memory_space=pltpu.SMEM),
            pl.BlockSpec((2, t, t), lambda m: (0, 0, 0)),
            pl.BlockSpec((2, t, t), lambda m: (0, 0, 0)),
        ],
        out_specs=pl.BlockSpec((1, 2, t, t), lambda m: (m, 0, 0, 0)),
        out_shape=jax.ShapeDtypeStruct((n_maps, 2, t, t), F32),
        compiler_params=_params("parallel"),
        name="bias_tiles",
    )(rel_bias, buckets, mask)


def _softmax_pv(qi, qc, k_ref, v_ref, cols, far_bias, near_bias, diag_bias):
    t = qc.shape[0]
    scale = HEAD_DIM ** -0.5
    pieces = []
    if qi >= 2:
        pieces.append((0, (qi - 1) * t, far_bias))
    if qi >= 1:
        pieces.append(((qi - 1) * t, t, near_bias))
    pieces.append((qi * t, t, diag_bias))

    scores = []
    for start, size, bias in pieces:
        kc = k_ref[0, start:start + size, cols]
        s = lax.dot_general(qc, kc, (((1,), (1,)), ((), ())), preferred_element_type=F32)
        scores.append(s * scale + bias)
    m = functools.reduce(jnp.maximum, [jnp.max(s, axis=-1, keepdims=True) for s in scores])
    probs = [jnp.exp(s - m) for s in scores]
    denom = sum(jnp.sum(p, axis=-1, keepdims=True) for p in probs)
    acc = sum(jnp.dot(p.astype(BF16), v_ref[0, start:start + size, :], preferred_element_type=F32)
              for p, (start, size, _) in zip(probs, pieces))
    return acc * (1.0 / denom)


def _attn_kernel(lambda_init, q_ref, k_ref, v_ref, bias_ref, table_ref, lam_ref, subg_ref, o_ref):
    head = pl.program_id(1)
    qi = pl.program_id(2)
    t = q_ref.shape[1]

    lam = lam_ref[...]
    lam_full = (jnp.exp(jnp.sum(lam[0:1] * lam[1:2], keepdims=True))
                - jnp.exp(jnp.sum(lam[2:3] * lam[3:4], keepdims=True)) + lambda_init)

    for i in range(k_ref.shape[1] // t):
        @pl.when(qi == i)
        def _(i=i):
            maps = []
            for c in range(2):
                cols = slice(c * HEAD_DIM, (c + 1) * HEAD_DIM)
                maps.append(_softmax_pv(i, q_ref[0, :, cols], k_ref, v_ref, cols,
                                        table_ref[FAR_BUCKET, 2 * head + c],
                                        bias_ref[c, 1], bias_ref[c, 0]))
            o = maps[0] - lam_full * maps[1]
            o = o * lax.rsqrt(jnp.mean(o * o, axis=-1, keepdims=True) + SUBLN_EPS)
            o_ref[0] = (o * subg_ref[...] * (1.0 - lambda_init)).astype(o_ref.dtype)


def _attention(q, k, v, bias_tiles, rel_bias, lam, subln_g, lambda_init):
    b, s, width = q.shape
    t = ATT_TILE
    n_heads = width // V_HEAD_DIM
    assert s % t == 0
    return pl.pallas_call(
        functools.partial(_attn_kernel, lambda_init),
        grid=(b, n_heads, s // t),
        in_specs=[
            pl.BlockSpec((1, t, V_HEAD_DIM), lambda bi, h, qi: (bi, qi, h)),
            pl.BlockSpec((1, s, V_HEAD_DIM), lambda bi, h, qi: (bi, 0, h)),
            pl.BlockSpec((1, s, V_HEAD_DIM), lambda bi, h, qi: (bi, 0, h)),
            pl.BlockSpec((2, 2, t, t), lambda bi, h, qi: (h, 0, 0, 0)),
            pl.BlockSpec(memory_space=pltpu.SMEM),
            pl.BlockSpec(lam.shape, lambda bi, h, qi: (0, 0)),
            pl.BlockSpec((1, V_HEAD_DIM), lambda bi, h, qi: (0, 0)),
        ],
        out_specs=pl.BlockSpec((1, t, V_HEAD_DIM), lambda bi, h, qi: (bi, qi, h)),
        out_shape=jax.ShapeDtypeStruct((b, s, width), BF16),
        compiler_params=_params("parallel", "parallel", "arbitrary"),
        name="diff_attention",
    )(q, k, v, bias_tiles, rel_bias, lam, subln_g)


def _out_proj_kernel(o_ref, w_ref, x_ref, g_ref, y_ref):
    m = jnp.dot(o_ref[...], w_ref[...], preferred_element_type=F32)
    y_ref[...] = x_ref[...] + _rms(m, g_ref[...], RMS_EPS)


def _out_proj(o, w, x, gain):
    m, d = x.shape
    kdim = o.shape[1]
    tm = PROJ_ROWS
    assert m % tm == 0
    return pl.pallas_call(
        _out_proj_kernel,
        grid=(m // tm,),
        in_specs=[
            pl.BlockSpec((tm, kdim), lambda i: (i, 0)),
            pl.BlockSpec((kdim, d), lambda i: (0, 0)),
            pl.BlockSpec((tm, d), lambda i: (i, 0)),
            pl.BlockSpec((1, d), lambda i: (0, 0)),
        ],
        out_specs=pl.BlockSpec((tm, d), lambda i: (i, 0)),
        out_shape=jax.ShapeDtypeStruct((m, d), F32),
        compiler_params=_params("parallel"),
        name="out_proj",
    )(o, w, x, gain)


def kernel(x, norm_gains, ffn_w_gate, ffn_w_up, ffn_w_down, pool_w, pool_scale, kv_norm, w_k, w_v,
           rel_bias, w_q, w_o, lambdas, subln_gain):
    b, s, d = x.shape
    depth = norm_gains.shape[0]
    n_a = pool_w.shape[0]
    gains = norm_gains.reshape(depth, 3, 2, 1, d)
    wg, wu, wd = (w.astype(BF16) for w in (ffn_w_gate, ffn_w_up, ffn_w_down))

    xf = x.reshape(b * s, d)
    k = v = bias_tiles = None
    for l in range(depth):
        g = gains[l]
        if l == n_a:
            k, v = _norm_proj(xf, kv_norm.reshape(1, d), [w_k.astype(BF16), w_v.astype(BF16)])
            bias_tiles = _bias_tiles(rel_bias, ATT_TILE)
        xf = _ffn(xf, g[0, 0], g[0, 1], wg, wu, wd, l, 0)
        if l < n_a:
            xf = _pool_mixer(xf.reshape(b, s, d), g[1, 0], g[1, 1], pool_w[l].astype(BF16),
                             pool_scale[l].reshape(1, d)).reshape(b * s, d)
        else:
            j = l - n_a
            lambda_init = 0.8 - 0.6 * math.exp(-0.3 * l)
            (q,) = _norm_proj(xf, g[1, 0], [w_q[j].astype(BF16)])
            o = _attention(q.reshape(b, s, -1), k.reshape(b, s, -1), v.reshape(b, s, -1), bias_tiles,
                           rel_bias, lambdas[j], subln_gain[j].reshape(1, -1), lambda_init)
            xf = _out_proj(o.reshape(b * s, -1), w_o[j].astype(BF16), xf, g[1, 1])
        xf = _ffn(xf, g[2, 0], g[2, 1], wg, wu, wd, l, 1)
    return xf.reshape(b, s, d)
```

```python
import functools
import math

import numpy as np
import jax
import jax.numpy as jnp
from jax import lax
from jax.experimental import pallas as pl
from jax.experimental.pallas import tpu as pltpu

F32 = jnp.float32
BF16 = jnp.bfloat16

CHUNK = 64
POOL_WINDOWS = (2, 4, 8, 16)
HEAD_DIM = 128
V_HEAD_DIM = 2 * HEAD_DIM
NUM_BUCKETS = 32
MAX_DISTANCE = 128
RMS_EPS = 1e-6
SUBLN_EPS = 1e-5

VMEM_LIMIT_BYTES = 56 * 1024 * 1024
FFN_ROWS = 1024
FFN_COLS = 256
FFN_DOWN_COLS = 512
PROJ_ROWS = 512
PROJ_COLS = 512
POOL_ROWS = 256
POOL_HALO = 16
ATT_TILE = 256


def _rms(x, gain, eps):
    ms = jnp.mean(x * x, axis=-1, keepdims=True)
    return x * lax.rsqrt(ms + eps) * gain


def _params(*semantics):
    return pltpu.CompilerParams(dimension_semantics=semantics,
                                vmem_limit_bytes=VMEM_LIMIT_BYTES)


def _ffn_kernel(x_ref, gpre_ref, gpost_ref, wg_ref, wu_ref, wd_ref, o_ref, h_ref):
    j = pl.program_id(1)

    @pl.when(j == 0)
    def _():
        h_ref[...] = _rms(x_ref[...], gpre_ref[...], RMS_EPS).astype(BF16)
        o_ref[...] = jnp.zeros(o_ref.shape, F32)

    h = h_ref[...]
    g = jnp.dot(h, wg_ref[...].astype(BF16), preferred_element_type=F32)
    u = jnp.dot(h, wu_ref[...].astype(BF16), preferred_element_type=F32)
    a = (g * (1.0 / (1.0 + jnp.exp(-g))) * u).astype(BF16)
    for c0 in range(0, o_ref.shape[1], FFN_DOWN_COLS):
        cols = slice(c0, c0 + FFN_DOWN_COLS)
        o_ref[:, cols] += jnp.dot(a, wd_ref[:, cols].astype(BF16), preferred_element_type=F32)

    @pl.when(j == pl.num_programs(1) - 1)
    def _():
        o_ref[...] = x_ref[...] + 0.5 * _rms(o_ref[...], gpost_ref[...], RMS_EPS)


def _ffn(x, gpre, gpost, wg, wu, wd, layer, half):
    m, d = x.shape
    ff = wg.shape[-1]
    tm, tf = FFN_ROWS, FFN_COLS
    assert m % tm == 0 and ff % tf == 0
    return pl.pallas_call(
        _ffn_kernel,
        grid=(m // tm, ff // tf),
        in_specs=[
            pl.BlockSpec((tm, d), lambda i, j: (i, 0), pipeline_mode=pl.Buffered(1)),
            pl.BlockSpec((1, d), lambda i, j: (0, 0)),
            pl.BlockSpec((1, d), lambda i, j: (0, 0)),
            pl.BlockSpec((None, None, d, tf), lambda i, j: (layer, half, 0, j)),
            pl.BlockSpec((None, None, d, tf), lambda i, j: (layer, half, 0, j)),
            pl.BlockSpec((None, None, tf, d), lambda i, j: (layer, half, j, 0)),
        ],
        out_specs=pl.BlockSpec((tm, d), lambda i, j: (i, 0)),
        out_shape=jax.ShapeDtypeStruct((m, d), F32),
        scratch_shapes=[pltpu.VMEM((tm, d), BF16)],
        compiler_params=_params("parallel", "arbitrary"),
        name="ffn",
    )(x, gpre, gpost, wg, wu, wd)


def _pool_kernel(x_ref, halo_ref, gpre_ref, gpost_ref, w_ref, scale_ref, o_ref, hx_ref):
    i = pl.program_id(1)
    ts = x_ref.shape[1]
    gdim = w_ref.shape[1]
    x = x_ref[0]
    h = _rms(x, gpre_ref[...], RMS_EPS)
    hh = _rms(halo_ref[0], gpre_ref[...], RMS_EPS)
    hx_ref[0:POOL_HALO, :] = jnp.where(i > 0, hh, 0.0)
    hx_ref[POOL_HALO:POOL_HALO + ts, :] = h

    t = i * ts + lax.broadcasted_iota(jnp.int32, (ts, 1), 0)
    ys = []
    for gi, win in enumerate(POOL_WINDOWS):
        c0 = gi * gdim
        acc = h[:, c0:c0 + gdim]
        for k in range(1, win):
            acc = acc + hx_ref[POOL_HALO - k:POOL_HALO - k + ts, c0:c0 + gdim]
        inv_cnt = 1.0 / jnp.minimum(t + 1, win).astype(F32)
        dg = acc * inv_cnt - h[:, c0:c0 + gdim]
        ys.append(jnp.dot(dg.astype(BF16), w_ref[gi], preferred_element_type=F32))
    y = jnp.concatenate(ys, axis=-1) * scale_ref[...]
    o_ref[0] = x + _rms(y, gpost_ref[...], RMS_EPS)


def _pool_mixer(x, gpre, gpost, w, scale):
    b, s, d = x.shape
    ts = POOL_ROWS
    assert s % ts == 0 and ts % POOL_HALO == 0
    halo_blocks = ts // POOL_HALO
    return pl.pallas_call(
        _pool_kernel,
        grid=(b, s // ts),
        in_specs=[
            pl.BlockSpec((1, ts, d), lambda bi, i: (bi, i, 0)),
            pl.BlockSpec((1, POOL_HALO, d),
                         lambda bi, i: (bi, jnp.maximum(i * halo_blocks - 1, 0), 0)),
            pl.BlockSpec((1, d), lambda bi, i: (0, 0)),
            pl.BlockSpec((1, d), lambda bi, i: (0, 0)),
            pl.BlockSpec(w.shape, lambda bi, i: (0, 0, 0)),
            pl.BlockSpec((1, d), lambda bi, i: (0, 0)),
        ],
        out_specs=pl.BlockSpec((1, ts, d), lambda bi, i: (bi, i, 0)),
        out_shape=jax.ShapeDtypeStruct((b, s, d), F32),
        scratch_shapes=[pltpu.VMEM((POOL_HALO + ts, d), F32)],
        compiler_params=_params("parallel", "parallel"),
        name="pool_mixer",
    )(x, x, gpre, gpost, w, scale)


def _norm_proj_kernel(n_out, x_ref, g_ref, *refs):
    w_refs, o_refs, h_ref = refs[:n_out], refs[n_out:2 * n_out], refs[2 * n_out]

    @pl.when(pl.program_id(1) == 0)
    def _():
        h_ref[...] = _rms(x_ref[...], g_ref[...], RMS_EPS).astype(BF16)

    h = h_ref[...]
    for w_ref, o_ref in zip(w_refs, o_refs):
        o_ref[...] = jnp.dot(h, w_ref[...], preferred_element_type=F32).astype(o_ref.dtype)


def _norm_proj(x, gain, weights):
    m, d = x.shape
    n = weights[0].shape[1]
    tm, tn = PROJ_ROWS, PROJ_COLS
    assert m % tm == 0 and n % tn == 0
    n_out = len(weights)
    outs = pl.pallas_call(
        functools.partial(_norm_proj_kernel, n_out),
        grid=(m // tm, n // tn),
        in_specs=[pl.BlockSpec((tm, d), lambda i, j: (i, 0)),
                  pl.BlockSpec((1, d), lambda i, j: (0, 0))]
                 + [pl.BlockSpec((d, tn), lambda i, j: (0, j))] * n_out,
        out_specs=[pl.BlockSpec((tm, tn), lambda i, j: (i, j))] * n_out,
        out_shape=[jax.ShapeDtypeStruct((m, n), BF16)] * n_out,
        scratch_shapes=[pltpu.VMEM((tm, d), BF16)],
        compiler_params=_params("parallel", "arbitrary"),
        name="norm_proj",
    )(x, gain, *weights)
    return outs


def _rel_bucket(rel):
    half = NUM_BUCKETS // 2
    max_exact = half // 2
    ret = jnp.where(rel > 0, half, 0)
    n = jnp.abs(rel)
    nf = jnp.maximum(n, 1).astype(F32)
    large = max_exact + (jnp.log(nf / max_exact) / math.log(MAX_DISTANCE / max_exact)
                         * (half - max_exact)).astype(jnp.int32)
    large = jnp.minimum(large, half - 1)
    return ret + jnp.where(n < max_exact, n, large)


FAR_BUCKET = NUM_BUCKETS // 2 - 1
FAR_DISTANCE = MAX_DISTANCE


def _bias_tile_constants(t):
    assert t + 1 >= FAR_DISTANCE and t % CHUNK == 0
    q = np.arange(t, dtype=np.int32)[:, None]
    k = np.arange(t, dtype=np.int32)[None, :]
    buckets = jnp.stack([_rel_bucket(jnp.asarray(k - q)), _rel_bucket(jnp.asarray(k - t - q))])
    allowed = np.stack([(k // CHUNK) <= (q // CHUNK), np.ones((t, t), bool)])
    mask = np.where(allowed, 0.0, -np.inf).astype(np.float32)
    return buckets, jnp.asarray(mask)


def _bias_kernel(table_ref, bucket_ref, mask_ref, o_ref):
    m = pl.program_id(0)
    bucket = bucket_ref[...]
    acc = mask_ref[...]
    for b in range(NUM_BUCKETS):
        acc = acc + jnp.where(bucket == b, table_ref[b, m], 0.0)
    o_ref[0] = acc


def _bias_tiles(rel_bias, t):
    buckets, mask = _bias_tile_constants(t)
    n_maps = rel_bias.shape[1]
    return pl.pallas_call(
        _bias_kernel,
        grid=(n_maps,),
        in_specs=[
            pl.BlockSpec(memory_space=pltpu.SMEM),
            pl.BlockSpec((2, t, t), lambda m: (0, 0, 0)),
            pl.BlockSpec((2, t, t), lambda m: (0, 0, 0)),
        ],
        out_specs=pl.BlockSpec((1, 2, t, t), lambda m: (m, 0, 0, 0)),
        out_shape=jax.ShapeDtypeStruct((n_maps, 2, t, t), F32),
        compiler_params=_params("parallel"),
        name="bias_tiles",
    )(rel_bias, buckets, mask)


def _softmax_pv(qi, qc, k_ref, v_ref, cols, far_bias, near_bias, diag_bias):
    t = qc.shape[0]
    scale = HEAD_DIM ** -0.5
    pieces = []
    if qi >= 2:
        pieces.append((0, (qi - 1) * t, far_bias))
    if qi >= 1:
        pieces.append(((qi - 1) * t, t, near_bias))
    pieces.append((qi * t, t, diag_bias))

    scores = []
    for start, size, bias in pieces:
        kc = k_ref[0, start:start + size, cols]
        s = lax.dot_general(qc, kc, (((1,), (1,)), ((), ())), preferred_element_type=F32)
        scores.append(s * scale + bias)
    m = functools.reduce(jnp.maximum, [jnp.max(s, axis=-1, keepdims=True) for s in scores])
    probs = [jnp.exp(s - m) for s in scores]
    denom = sum(jnp.sum(p, axis=-1, keepdims=True) for p in probs)
    acc = sum(jnp.dot(p.astype(BF16), v_ref[0, start:start + size, :], preferred_element_type=F32)
              for p, (start, size, _) in zip(probs, pieces))
    return acc * (1.0 / denom)


def _attn_kernel(lambda_init, q_ref, k_ref, v_ref, bias_ref, table_ref, lam_ref, subg_ref, o_ref):
    head = pl.program_id(1)
    qi = pl.program_id(2)
    t = q_ref.shape[1]

    lam = lam_ref[...]
    lam_full = (jnp.exp(jnp.sum(lam[0:1] * lam[1:2], keepdims=True))
                - jnp.exp(jnp.sum(lam[2:3] * lam[3:4], keepdims=True)) + lambda_init)

    for i in range(k_ref.shape[1] // t):
        @pl.when(qi == i)
        def _(i=i):
            maps = []
            for c in range(2):
                cols = slice(c * HEAD_DIM, (c + 1) * HEAD_DIM)
                maps.append(_softmax_pv(i, q_ref[0, :, cols], k_ref, v_ref, cols,
                                        table_ref[FAR_BUCKET, 2 * head + c],
                                        bias_ref[c, 1], bias_ref[c, 0]))
            o = maps[0] - lam_full * maps[1]
            o = o * lax.rsqrt(jnp.mean(o * o, axis=-1, keepdims=True) + SUBLN_EPS)
            o_ref[0] = (o * subg_ref[...] * (1.0 - lambda_init)).astype(o_ref.dtype)


def _attention(q, k, v, bias_tiles, rel_bias, lam, subln_g, lambda_init):
    b, s, width = q.shape
    t = ATT_TILE
    n_heads = width // V_HEAD_DIM
    assert s % t == 0
    return pl.pallas_call(
        functools.partial(_attn_kernel, lambda_init),
        grid=(b, n_heads, s // t),
        in_specs=[
            pl.BlockSpec((1, t, V_HEAD_DIM), lambda bi, h, qi: (bi, qi, h)),
            pl.BlockSpec((1, s, V_HEAD_DIM), lambda bi, h, qi: (bi, 0, h)),
            pl.BlockSpec((1, s, V_HEAD_DIM), lambda bi, h, qi: (bi, 0, h)),
            pl.BlockSpec((2, 2, t, t), lambda bi, h, qi: (h, 0, 0, 0)),
            pl.BlockSpec(memory_space=pltpu.SMEM),
            pl.BlockSpec(lam.shape, lambda bi, h, qi: (0, 0)),
            pl.BlockSpec((1, V_HEAD_DIM), lambda bi, h, qi: (0, 0)),
        ],
        out_specs=pl.BlockSpec((1, t, V_HEAD_DIM), lambda bi, h, qi: (bi, qi, h)),
        out_shape=jax.ShapeDtypeStruct((b, s, width), BF16),
        compiler_params=_params("parallel", "parallel", "arbitrary"),
        name="diff_attention",
    )(q, k, v, bias_tiles, rel_bias, lam, subln_g)


def _out_proj_kernel(o_ref, w_ref, x_ref, g_ref, y_ref):
    m = jnp.dot(o_ref[...], w_ref[...], preferred_element_type=F32)
    y_ref[...] = x_ref[...] + _rms(m, g_ref[...], RMS_EPS)


def _out_proj(o, w, x, gain):
    m, d = x.shape
    kdim = o.shape[1]
    tm = PROJ_ROWS
    assert m % tm == 0
    return pl.pallas_call(
        _out_proj_kernel,
        grid=(m // tm,),
        in_specs=[
            pl.BlockSpec((tm, kdim), lambda i: (i, 0)),
            pl.BlockSpec((kdim, d), lambda i: (0, 0)),
            pl.BlockSpec((tm, d), lambda i: (i, 0)),
            pl.BlockSpec((1, d), lambda i: (0, 0)),
        ],
        out_specs=pl.BlockSpec((tm, d), lambda i: (i, 0)),
        out_shape=jax.ShapeDtypeStruct((m, d), F32),
        compiler_params=_params("parallel"),
        name="out_proj",
    )(o, w, x, gain)


def kernel(x, norm_gains, ffn_w_gate, ffn_w_up, ffn_w_down, pool_w, pool_scale, kv_norm, w_k, w_v,
           rel_bias, w_q, w_o, lambdas, subln_gain):
    b, s, d = x.shape
    depth = norm_gains.shape[0]
    n_a = pool_w.shape[0]
    gains = norm_gains.reshape(depth, 3, 2, 1, d)
    wg, wu, wd = ffn_w_gate, ffn_w_up, ffn_w_down

    xf = x.reshape(b * s, d)
    k = v = bias_tiles = None
    for l in range(depth):
        g = gains[l]
        if l == n_a:
            k, v = _norm_proj(xf, kv_norm.reshape(1, d), [w_k.astype(BF16), w_v.astype(BF16)])
            bias_tiles = _bias_tiles(rel_bias, ATT_TILE)
        xf = _ffn(xf, g[0, 0], g[0, 1], wg, wu, wd, l, 0)
        if l < n_a:
            xf = _pool_mixer(xf.reshape(b, s, d), g[1, 0], g[1, 1], pool_w[l].astype(BF16),
                             pool_scale[l].reshape(1, d)).reshape(b * s, d)
        else:
            j = l - n_a
            lambda_init = 0.8 - 0.6 * math.exp(-0.3 * l)
            (q,) = _norm_proj(xf, g[1, 0], [w_q[j].astype(BF16)])
            o = _attention(q.reshape(b, s, -1), k.reshape(b, s, -1), v.reshape(b, s, -1), bias_tiles,
                           rel_bias, lambdas[j], subln_gain[j].reshape(1, -1), lambda_init)
            xf = _out_proj(o.reshape(b * s, -1), w_o[j].astype(BF16), xf, g[1, 1])
        xf = _ffn(xf, g[2, 0], g[2, 1], wg, wu, wd, l, 1)
    return xf.reshape(b, s, d)
```

```python
import functools
import math

import numpy as np
import jax
import jax.numpy as jnp
from jax import lax
from jax.experimental import pallas as pl
from jax.experimental.pallas import tpu as pltpu

F32 = jnp.float32
BF16 = jnp.bfloat16

CHUNK = 64
POOL_WINDOWS = (2, 4, 8, 16)
HEAD_DIM = 128
V_HEAD_DIM = 2 * HEAD_DIM
NUM_BUCKETS = 32
MAX_DISTANCE = 128
RMS_EPS = 1e-6
SUBLN_EPS = 1e-5

VMEM_LIMIT_BYTES = 60 * 1024 * 1024
FFN_ROWS = 1024
FFN_COLS = 256
FFN_DOWN_COLS = 512
NORM_PROJ_ROWS = 1024
PROJ_ROWS = 512
PROJ_COLS = 512
POOL_ROWS = 256
POOL_HALO = 16
ATT_TILE = 256


def _rms(x, gain, eps):
    ms = jnp.mean(x * x, axis=-1, keepdims=True)
    return x * lax.rsqrt(ms + eps) * gain


def _params(*semantics):
    return pltpu.CompilerParams(dimension_semantics=semantics,
                                vmem_limit_bytes=VMEM_LIMIT_BYTES)


def _ffn_kernel(x_ref, gpre_ref, gpost_ref, wg_ref, wu_ref, wd_ref, o_ref, h_ref):
    j = pl.program_id(1)

    @pl.when(j == 0)
    def _():
        h_ref[...] = _rms(x_ref[...], gpre_ref[...], RMS_EPS).astype(BF16)
        o_ref[...] = jnp.zeros(o_ref.shape, F32)

    h = h_ref[...]
    g = jnp.dot(h, wg_ref[...].astype(BF16), preferred_element_type=F32)
    u = jnp.dot(h, wu_ref[...].astype(BF16), preferred_element_type=F32)
    a = (g * (1.0 / (1.0 + jnp.exp(-g))) * u).astype(BF16)
    for c0 in range(0, o_ref.shape[1], FFN_DOWN_COLS):
        cols = slice(c0, c0 + FFN_DOWN_COLS)
        o_ref[:, cols] += jnp.dot(a, wd_ref[:, cols].astype(BF16), preferred_element_type=F32)

    @pl.when(j == pl.num_programs(1) - 1)
    def _():
        o_ref[...] = x_ref[...] + 0.5 * _rms(o_ref[...], gpost_ref[...], RMS_EPS)


def _ffn(x, gpre, gpost, wg, wu, wd, layer, half):
    m, d = x.shape
    ff = wg.shape[-1]
    tm, tf = FFN_ROWS, FFN_COLS
    assert m % tm == 0 and ff % tf == 0
    return pl.pallas_call(
        _ffn_kernel,
        grid=(m // tm, ff // tf),
        in_specs=[
            pl.BlockSpec((tm, d), lambda i, j: (i, 0)),
            pl.BlockSpec((1, d), lambda i, j: (0, 0)),
            pl.BlockSpec((1, d), lambda i, j: (0, 0)),
            pl.BlockSpec((None, None, d, tf), lambda i, j: (layer, half, 0, j)),
            pl.BlockSpec((None, None, d, tf), lambda i, j: (layer, half, 0, j)),
            pl.BlockSpec((None, None, tf, d), lambda i, j: (layer, half, j, 0)),
        ],
        out_specs=pl.BlockSpec((tm, d), lambda i, j: (i, 0)),
        out_shape=jax.ShapeDtypeStruct((m, d), F32),
        scratch_shapes=[pltpu.VMEM((tm, d), BF16)],
        compiler_params=_params("parallel", "arbitrary"),
        name="ffn",
    )(x, gpre, gpost, wg, wu, wd)


def _pool_kernel(x_ref, halo_ref, gpre_ref, gpost_ref, w_ref, scale_ref, o_ref, hx_ref):
    i = pl.program_id(1)
    ts = x_ref.shape[1]
    gdim = w_ref.shape[1]
    x = x_ref[0]
    h = _rms(x, gpre_ref[...], RMS_EPS)
    hh = _rms(halo_ref[0], gpre_ref[...], RMS_EPS)
    hx_ref[0:POOL_HALO, :] = jnp.where(i > 0, hh, 0.0)
    hx_ref[POOL_HALO:POOL_HALO + ts, :] = h

    t = i * ts + lax.broadcasted_iota(jnp.int32, (ts, 1), 0)
    ys = []
    for gi, win in enumerate(POOL_WINDOWS):
        c0 = gi * gdim
        acc = h[:, c0:c0 + gdim]
        for k in range(1, win):
            acc = acc + hx_ref[POOL_HALO - k:POOL_HALO - k + ts, c0:c0 + gdim]
        inv_cnt = 1.0 / jnp.minimum(t + 1, win).astype(F32)
        dg = acc * inv_cnt - h[:, c0:c0 + gdim]
        ys.append(jnp.dot(dg.astype(BF16), w_ref[gi], preferred_element_type=F32))
    y = jnp.concatenate(ys, axis=-1) * scale_ref[...]
    o_ref[0] = x + _rms(y, gpost_ref[...], RMS_EPS)


def _pool_mixer(x, gpre, gpost, w, scale):
    b, s, d = x.shape
    ts = POOL_ROWS
    assert s % ts == 0 and ts % POOL_HALO == 0
    halo_blocks = ts // POOL_HALO
    return pl.pallas_call(
        _pool_kernel,
        grid=(b, s // ts),
        in_specs=[
            pl.BlockSpec((1, ts, d), lambda bi, i: (bi, i, 0)),
            pl.BlockSpec((1, POOL_HALO, d),
                         lambda bi, i: (bi, jnp.maximum(i * halo_blocks - 1, 0), 0)),
            pl.BlockSpec((1, d), lambda bi, i: (0, 0)),
            pl.BlockSpec((1, d), lambda bi, i: (0, 0)),
            pl.BlockSpec(w.shape, lambda bi, i: (0, 0, 0)),
            pl.BlockSpec((1, d), lambda bi, i: (0, 0)),
        ],
        out_specs=pl.BlockSpec((1, ts, d), lambda bi, i: (bi, i, 0)),
        out_shape=jax.ShapeDtypeStruct((b, s, d), F32),
        scratch_shapes=[pltpu.VMEM((POOL_HALO + ts, d), F32)],
        compiler_params=_params("parallel", "parallel"),
        name="pool_mixer",
    )(x, x, gpre, gpost, w, scale)


def _norm_proj_kernel(n_out, x_ref, g_ref, *refs):
    w_refs, o_refs, h_ref = refs[:n_out], refs[n_out:2 * n_out], refs[2 * n_out]

    @pl.when(pl.program_id(1) == 0)
    def _():
        h_ref[...] = _rms(x_ref[...], g_ref[...], RMS_EPS).astype(BF16)

    h = h_ref[...]
    for w_ref, o_ref in zip(w_refs, o_refs):
        o_ref[...] = jnp.dot(h, w_ref[...].astype(BF16),
                             preferred_element_type=F32).astype(o_ref.dtype)


def _norm_proj(x, gain, weights):
    m, d = x.shape
    n = weights[0].shape[1]
    tm, tn = NORM_PROJ_ROWS, PROJ_COLS
    assert m % tm == 0 and n % tn == 0
    n_out = len(weights)
    outs = pl.pallas_call(
        functools.partial(_norm_proj_kernel, n_out),
        grid=(m // tm, n // tn),
        in_specs=[pl.BlockSpec((tm, d), lambda i, j: (i, 0), pipeline_mode=pl.Buffered(1)),
                  pl.BlockSpec((1, d), lambda i, j: (0, 0))]
                 + [pl.BlockSpec((d, tn), lambda i, j: (0, j))] * n_out,
        out_specs=[pl.BlockSpec((tm, tn), lambda i, j: (i, j))] * n_out,
        out_shape=[jax.ShapeDtypeStruct((m, n), BF16)] * n_out,
        scratch_shapes=[pltpu.VMEM((tm, d), BF16)],
        compiler_params=_params("parallel", "arbitrary"),
        name="norm_proj",
    )(x, gain, *weights)
    return outs


def _rel_bucket(rel):
    half = NUM_BUCKETS // 2
    max_exact = half // 2
    ret = jnp.where(rel > 0, half, 0)
    n = jnp.abs(rel)
    nf = jnp.maximum(n, 1).astype(F32)
    large = max_exact + (jnp.log(nf / max_exact) / math.log(MAX_DISTANCE / max_exact)
                         * (half - max_exact)).astype(jnp.int32)
    large = jnp.minimum(large, half - 1)
    return ret + jnp.where(n < max_exact, n, large)


FAR_BUCKET = NUM_BUCKETS // 2 - 1
FAR_DISTANCE = MAX_DISTANCE


def _bias_tile_constants(t):
    assert t + 1 >= FAR_DISTANCE and t % CHUNK == 0
    q = np.arange(t, dtype=np.int32)[:, None]
    k = np.arange(t, dtype=np.int32)[None, :]
    buckets = jnp.stack([_rel_bucket(jnp.asarray(k - q)), _rel_bucket(jnp.asarray(k - t - q))])
    allowed = np.stack([(k // CHUNK) <= (q // CHUNK), np.ones((t, t), bool)])
    mask = np.where(allowed, 0.0, -np.inf).astype(np.float32)
    return buckets, jnp.asarray(mask)


def _bias_kernel(table_ref, bucket_ref, mask_ref, o_ref):
    m = pl.program_id(0)
    bucket = bucket_ref[...]
    acc = mask_ref[...]
    for b in range(NUM_BUCKETS):
        acc = acc + jnp.where(bucket == b, table_ref[b, m], 0.0)
    o_ref[0] = acc


def _bias_tiles(rel_bias, t):
    buckets, mask = _bias_tile_constants(t)
    n_maps = rel_bias.shape[1]
    return pl.pallas_call(
        _bias_kernel,
        grid=(n_maps,),
        in_specs=[
            pl.BlockSpec(memory_space=pltpu.SMEM),
            pl.BlockSpec((2, t, t), lambda m: (0, 0, 0)),
            pl.BlockSpec((2, t, t), lambda m: (0, 0, 0)),
        ],
        out_specs=pl.BlockSpec((1, 2, t, t), lambda m: (m, 0, 0, 0)),
        out_shape=jax.ShapeDtypeStruct((n_maps, 2, t, t), F32),
        compiler_params=_params("parallel"),
        name="bias_tiles",
    )(rel_bias, buckets, mask)


def _softmax_pv(qi, qc, k_ref, v_ref, cols, far_bias, near_bias, diag_bias):
    t = qc.shape[0]
    scale = HEAD_DIM ** -0.5
    pieces = []
    if qi >= 2:
        pieces.append((0, (qi - 1) * t, far_bias))
    if qi >= 1:
        pieces.append(((qi - 1) * t, t, near_bias))
    pieces.append((qi * t, t, diag_bias))

    scores = []
    for start, size, bias in pieces:
        kc = k_ref[0, start:start + size, cols]
        s = lax.dot_general(qc, kc, (((1,), (1,)), ((), ())), preferred_element_type=F32)
        scores.append(s * scale + bias)
    m = functools.reduce(jnp.maximum, [jnp.max(s, axis=-1, keepdims=True) for s in scores])
    probs = [jnp.exp(s - m) for s in scores]
    denom = sum(jnp.sum(p, axis=-1, keepdims=True) for p in probs)
    acc = sum(jnp.dot(p.astype(BF16), v_ref[0, start:start + size, :], preferred_element_type=F32)
              for p, (start, size, _) in zip(probs, pieces))
    return acc * (1.0 / denom)


def _attn_kernel(lambda_init, q_ref, k_ref, v_ref, bias_ref, table_ref, lam_ref, subg_ref, o_ref):
    head = pl.program_id(1)
    t = bias_ref.shape[-1]

    lam = lam_ref[...]
    lam_full = (jnp.exp(jnp.sum(lam[0:1] * lam[1:2], keepdims=True))
                - jnp.exp(jnp.sum(lam[2:3] * lam[3:4], keepdims=True)) + lambda_init)

    for i in range(q_ref.shape[1] // t):
        rows = slice(i * t, (i + 1) * t)
        maps = []
        for c in range(2):
            cols = slice(c * HEAD_DIM, (c + 1) * HEAD_DIM)
            maps.append(_softmax_pv(i, q_ref[0, rows, cols], k_ref, v_ref, cols,
                                    table_ref[FAR_BUCKET, 2 * head + c],
                                    bias_ref[c, 1], bias_ref[c, 0]))
        o = maps[0] - lam_full * maps[1]
        o = o * lax.rsqrt(jnp.mean(o * o, axis=-1, keepdims=True) + SUBLN_EPS)
        o_ref[0, rows, :] = (o * subg_ref[...] * (1.0 - lambda_init)).astype(o_ref.dtype)


def _attention(q, k, v, bias_tiles, rel_bias, lam, subln_g, lambda_init):
    b, s, width = q.shape
    t = ATT_TILE
    n_heads = width // V_HEAD_DIM
    assert s % t == 0
    return pl.pallas_call(
        functools.partial(_attn_kernel, lambda_init),
        grid=(b, n_heads),
        in_specs=[
            pl.BlockSpec((1, s, V_HEAD_DIM), lambda bi, h: (bi, 0, h)),
            pl.BlockSpec((1, s, V_HEAD_DIM), lambda bi, h: (bi, 0, h)),
            pl.BlockSpec((1, s, V_HEAD_DIM), lambda bi, h: (bi, 0, h)),
            pl.BlockSpec((2, 2, t, t), lambda bi, h: (h, 0, 0, 0)),
            pl.BlockSpec(memory_space=pltpu.SMEM),
            pl.BlockSpec(lam.shape, lambda bi, h: (0, 0)),
            pl.BlockSpec((1, V_HEAD_DIM), lambda bi, h: (0, 0)),
        ],
        out_specs=pl.BlockSpec((1, s, V_HEAD_DIM), lambda bi, h: (bi, 0, h)),
        out_shape=jax.ShapeDtypeStruct((b, s, width), BF16),
        compiler_params=_params("parallel", "parallel"),
        name="diff_attention",
    )(q, k, v, bias_tiles, rel_bias, lam, subln_g)


def _out_proj_kernel(o_ref, w_ref, x_ref, g_ref, y_ref):
    m = jnp.dot(o_ref[...], w_ref[...], preferred_element_type=F32)
    y_ref[...] = x_ref[...] + _rms(m, g_ref[...], RMS_EPS)


def _out_proj(o, w, x, gain):
    m, d = x.shape
    kdim = o.shape[1]
    tm = PROJ_ROWS
    assert m % tm == 0
    return pl.pallas_call(
        _out_proj_kernel,
        grid=(m // tm,),
        in_specs=[
            pl.BlockSpec((tm, kdim), lambda i: (i, 0)),
            pl.BlockSpec((kdim, d), lambda i: (0, 0)),
            pl.BlockSpec((tm, d), lambda i: (i, 0)),
            pl.BlockSpec((1, d), lambda i: (0, 0)),
        ],
        out_specs=pl.BlockSpec((tm, d), lambda i: (i, 0)),
        out_shape=jax.ShapeDtypeStruct((m, d), F32),
        compiler_params=_params("parallel"),
        name="out_proj",
    )(o, w, x, gain)


def kernel(x, norm_gains, ffn_w_gate, ffn_w_up, ffn_w_down, pool_w, pool_scale, kv_norm, w_k, w_v,
           rel_bias, w_q, w_o, lambdas, subln_gain):
    b, s, d = x.shape
    depth = norm_gains.shape[0]
    n_a = pool_w.shape[0]
    gains = norm_gains.reshape(depth, 3, 2, 1, d)
    wg, wu, wd = ffn_w_gate, ffn_w_up, ffn_w_down

    xf = x.reshape(b * s, d)
    k = v = bias_tiles = None
    for l in range(depth):
        g = gains[l]
        if l == n_a:
            k, v = _norm_proj(xf, kv_norm.reshape(1, d), [w_k, w_v])
            bias_tiles = _bias_tiles(rel_bias, ATT_TILE)
        xf = _ffn(xf, g[0, 0], g[0, 1], wg, wu, wd, l, 0)
        if l < n_a:
            xf = _pool_mixer(xf.reshape(b, s, d), g[1, 0], g[1, 1], pool_w[l].astype(BF16),
                             pool_scale[l].reshape(1, d)).reshape(b * s, d)
        else:
            j = l - n_a
            lambda_init = 0.8 - 0.6 * math.exp(-0.3 * l)
            (q,) = _norm_proj(xf, g[1, 0], [w_q[j]])
            o = _attention(q.reshape(b, s, -1), k.reshape(b, s, -1), v.reshape(b, s, -1), bias_tiles,
                           rel_bias, lambdas[j], subln_gain[j].reshape(1, -1), lambda_init)
            xf = _out_proj(o.reshape(b * s, -1), w_o[j].astype(BF16), xf, g[1, 1])
        xf = _ffn(xf, g[2, 0], g[2, 1], wg, wu, wd, l, 1)
    return xf.reshape(b, s, d)
```

```python
import functools
import math

import numpy as np
import jax
import jax.numpy as jnp
from jax import lax
from jax.experimental import pallas as pl
from jax.experimental.pallas import tpu as pltpu

F32 = jnp.float32
BF16 = jnp.bfloat16

CHUNK = 64
POOL_WINDOWS = (2, 4, 8, 16)
HEAD_DIM = 128
V_HEAD_DIM = 2 * HEAD_DIM
NUM_BUCKETS = 32
MAX_DISTANCE = 128
RMS_EPS = 1e-6
SUBLN_EPS = 1e-5

VMEM_LIMIT_BYTES = 60 * 1024 * 1024
FFN_ROWS = 1024
FFN_COLS = 256
FFN_DOWN_COLS = 512
NORM_PROJ_ROWS = 1024
PROJ_ROWS = 512
PROJ_COLS = 512
NORM_ROW_CHUNK = 16
POOL_ROWS = 256
POOL_HALO = 16
ATT_TILE = 256


def _rms(x, gain, eps):
    ms = jnp.mean(x * x, axis=-1, keepdims=True)
    return x * lax.rsqrt(ms + eps) * gain


def _for_row_chunks(n_rows, body):
    for r0 in range(0, n_rows, NORM_ROW_CHUNK):
        body(slice(r0, r0 + NORM_ROW_CHUNK))


def _norm_rows_to(dst_ref, src_ref, gain, eps):
    def body(rows):
        dst_ref[rows, :] = _rms(src_ref[rows, :], gain, eps).astype(dst_ref.dtype)
    _for_row_chunks(src_ref.shape[0], body)


def _params(*semantics):
    return pltpu.CompilerParams(dimension_semantics=semantics,
                                vmem_limit_bytes=VMEM_LIMIT_BYTES)


def _ffn_kernel(x_ref, gpre_ref, gpost_ref, wg_ref, wu_ref, wd_ref, o_ref, h_ref):
    j = pl.program_id(1)

    @pl.when(j == 0)
    def _():
        _norm_rows_to(h_ref, x_ref, gpre_ref[...], RMS_EPS)
        o_ref[...] = jnp.zeros(o_ref.shape, F32)

    h = h_ref[...]
    g = jnp.dot(h, wg_ref[...].astype(BF16), preferred_element_type=F32)
    u = jnp.dot(h, wu_ref[...].astype(BF16), preferred_element_type=F32)
    a = (g * (1.0 / (1.0 + jnp.exp(-g))) * u).astype(BF16)
    for c0 in range(0, o_ref.shape[1], FFN_DOWN_COLS):
        cols = slice(c0, c0 + FFN_DOWN_COLS)
        o_ref[:, cols] += jnp.dot(a, wd_ref[:, cols].astype(BF16), preferred_element_type=F32)

    @pl.when(j == pl.num_programs(1) - 1)
    def _():
        half_gain = 0.5 * gpost_ref[...]

        def body(rows):
            o_ref[rows, :] = x_ref[rows, :] + _rms(o_ref[rows, :], half_gain, RMS_EPS)
        _for_row_chunks(o_ref.shape[0], body)


def _ffn(x, gpre, gpost, wg, wu, wd, layer, half):
    m, d = x.shape
    ff = wg.shape[-1]
    tm, tf = FFN_ROWS, FFN_COLS
    assert m % tm == 0 and ff % tf == 0
    return pl.pallas_call(
        _ffn_kernel,
        grid=(m // tm, ff // tf),
        in_specs=[
            pl.BlockSpec((tm, d), lambda i, j: (i, 0)),
            pl.BlockSpec((1, d), lambda i, j: (0, 0)),
            pl.BlockSpec((1, d), lambda i, j: (0, 0)),
            pl.BlockSpec((None, None, d, tf), lambda i, j: (layer, half, 0, j)),
            pl.BlockSpec((None, None, d, tf), lambda i, j: (layer, half, 0, j)),
            pl.BlockSpec((None, None, tf, d), lambda i, j: (layer, half, j, 0)),
        ],
        out_specs=pl.BlockSpec((tm, d), lambda i, j: (i, 0)),
        out_shape=jax.ShapeDtypeStruct((m, d), F32),
        scratch_shapes=[pltpu.VMEM((tm, d), BF16)],
        compiler_params=_params("parallel", "arbitrary"),
        name="ffn",
    )(x, gpre, gpost, wg, wu, wd)


def _pool_kernel(x_ref, halo_ref, gpre_ref, gpost_ref, w_ref, scale_ref, o_ref, hx_ref):
    i = pl.program_id(1)
    ts = x_ref.shape[1]
    gdim = w_ref.shape[1]
    x = x_ref[0]
    h = _rms(x, gpre_ref[...], RMS_EPS)
    hh = _rms(halo_ref[0], gpre_ref[...], RMS_EPS)
    hx_ref[0:POOL_HALO, :] = jnp.where(i > 0, hh, 0.0)
    hx_ref[POOL_HALO:POOL_HALO + ts, :] = h

    t = i * ts + lax.broadcasted_iota(jnp.int32, (ts, 1), 0)
    ys = []
    for gi, win in enumerate(POOL_WINDOWS):
        c0 = gi * gdim
        acc = h[:, c0:c0 + gdim]
        for k in range(1, win):
            acc = acc + hx_ref[POOL_HALO - k:POOL_HALO - k + ts, c0:c0 + gdim]
        inv_cnt = 1.0 / jnp.minimum(t + 1, win).astype(F32)
        dg = acc * inv_cnt - h[:, c0:c0 + gdim]
        ys.append(jnp.dot(dg.astype(BF16), w_ref[gi], preferred_element_type=F32))
    y = jnp.concatenate(ys, axis=-1) * scale_ref[...]
    o_ref[0] = x + _rms(y, gpost_ref[...], RMS_EPS)


def _pool_mixer(x, gpre, gpost, w, scale):
    b, s, d = x.shape
    ts = POOL_ROWS
    assert s % ts == 0 and ts % POOL_HALO == 0
    halo_blocks = ts // POOL_HALO
    return pl.pallas_call(
        _pool_kernel,
        grid=(b, s // ts),
        in_specs=[
            pl.BlockSpec((1, ts, d), lambda bi, i: (bi, i, 0)),
            pl.BlockSpec((1, POOL_HALO, d),
                         lambda bi, i: (bi, jnp.maximum(i * halo_blocks - 1, 0), 0)),
            pl.BlockSpec((1, d), lambda bi, i: (0, 0)),
            pl.BlockSpec((1, d), lambda bi, i: (0, 0)),
            pl.BlockSpec(w.shape, lambda bi, i: (0, 0, 0)),
            pl.BlockSpec((1, d), lambda bi, i: (0, 0)),
        ],
        out_specs=pl.BlockSpec((1, ts, d), lambda bi, i: (bi, i, 0)),
        out_shape=jax.ShapeDtypeStruct((b, s, d), F32),
        scratch_shapes=[pltpu.VMEM((POOL_HALO + ts, d), F32)],
        compiler_params=_params("parallel", "parallel"),
        name="pool_mixer",
    )(x, x, gpre, gpost, w, scale)


def _norm_proj_kernel(n_out, x_ref, g_ref, *refs):
    w_refs, o_refs, h_ref = refs[:n_out], refs[n_out:2 * n_out], refs[2 * n_out]

    @pl.when(pl.program_id(1) == 0)
    def _():
        _norm_rows_to(h_ref, x_ref, g_ref[...], RMS_EPS)

    h = h_ref[...]
    for w_ref, o_ref in zip(w_refs, o_refs):
        o_ref[...] = jnp.dot(h, w_ref[...].astype(BF16),
                             preferred_element_type=F32).astype(o_ref.dtype)


def _norm_proj(x, gain, weights):
    m, d = x.shape
    n = weights[0].shape[1]
    tm, tn = NORM_PROJ_ROWS, PROJ_COLS
    assert m % tm == 0 and n % tn == 0
    n_out = len(weights)
    outs = pl.pallas_call(
        functools.partial(_norm_proj_kernel, n_out),
        grid=(m // tm, n // tn),
        in_specs=[pl.BlockSpec((tm, d), lambda i, j: (i, 0)),
                  pl.BlockSpec((1, d), lambda i, j: (0, 0))]
                 + [pl.BlockSpec((d, tn), lambda i, j: (0, j))] * n_out,
        out_specs=[pl.BlockSpec((tm, tn), lambda i, j: (i, j))] * n_out,
        out_shape=[jax.ShapeDtypeStruct((m, n), BF16)] * n_out,
        scratch_shapes=[pltpu.VMEM((tm, d), BF16)],
        compiler_params=_params("parallel", "arbitrary"),
        name="norm_proj",
    )(x, gain, *weights)
    return outs


def _rel_bucket(rel):
    half = NUM_BUCKETS // 2
    max_exact = half // 2
    ret = jnp.where(rel > 0, half, 0)
    n = jnp.abs(rel)
    nf = jnp.maximum(n, 1).astype(F32)
    large = max_exact + (jnp.log(nf / max_exact) / math.log(MAX_DISTANCE / max_exact)
                         * (half - max_exact)).astype(jnp.int32)
    large = jnp.minimum(large, half - 1)
    return ret + jnp.where(n < max_exact, n, large)


FAR_BUCKET = NUM_BUCKETS // 2 - 1
FAR_DISTANCE = MAX_DISTANCE


def _bias_tile_constants(t):
    assert t + 1 >= FAR_DISTANCE and t % CHUNK == 0
    q = np.arange(t, dtype=np.int32)[:, None]
    k = np.arange(t, dtype=np.int32)[None, :]
    buckets = jnp.stack([_rel_bucket(jnp.asarray(k - q)), _rel_bucket(jnp.asarray(k - t - q))])
    allowed = np.stack([(k // CHUNK) <= (q // CHUNK), np.ones((t, t), bool)])
    mask = np.where(allowed, 0.0, -np.inf).astype(np.float32)
    return buckets, jnp.asarray(mask)


def _bias_kernel(table_ref, bucket_ref, mask_ref, o_ref):
    m = pl.program_id(0)
    bucket = bucket_ref[...]
    acc = mask_ref[...]
    for b in range(NUM_BUCKETS):
        acc = acc + jnp.where(bucket == b, table_ref[b, m], 0.0)
    o_ref[0] = acc


def _bias_tiles(rel_bias, t):
    buckets, mask = _bias_tile_constants(t)
    n_maps = rel_bias.shape[1]
    return pl.pallas_call(
        _bias_kernel,
        grid=(n_maps,),
        in_specs=[
            pl.BlockSpec(memory_space=pltpu.SMEM),
            pl.BlockSpec((2, t, t), lambda m: (0, 0, 0)),
            pl.BlockSpec((2, t, t), lambda m: (0, 0, 0)),
        ],
        out_specs=pl.BlockSpec((1, 2, t, t), lambda m: (m, 0, 0, 0)),
        out_shape=jax.ShapeDtypeStruct((n_maps, 2, t, t), F32),
        compiler_params=_params("parallel"),
        name="bias_tiles",
    )(rel_bias, buckets, mask)


def _softmax_pv(qi, qc, k_ref, v_ref, cols, far_bias, near_bias, diag_bias):
    t = qc.shape[0]
    scale = HEAD_DIM ** -0.5
    pieces = []
    if qi >= 2:
        pieces.append((0, (qi - 1) * t, far_bias))
    if qi >= 1:
        pieces.append(((qi - 1) * t, t, near_bias))
    pieces.append((qi * t, t, diag_bias))

    scores = []
    for start, size, bias in pieces:
        kc = k_ref[0, start:start + size, cols]
        s = lax.dot_general(qc, kc, (((1,), (1,)), ((), ())), preferred_element_type=F32)
        scores.append(s * scale + bias)
    m = functools.reduce(jnp.maximum, [jnp.max(s, axis=-1, keepdims=True) for s in scores])
    probs = [jnp.exp(s - m) for s in scores]
    denom = sum(jnp.sum(p, axis=-1, keepdims=True) for p in probs)
    acc = sum(jnp.dot(p.astype(BF16), v_ref[0, start:start + size, :], preferred_element_type=F32)
              for p, (start, size, _) in zip(probs, pieces))
    return acc * (1.0 / denom)


def _attn_kernel(lambda_init, q_ref, k_ref, v_ref, bias_ref, table_ref, lam_ref, subg_ref, o_ref):
    head = pl.program_id(1)
    t = bias_ref.shape[-1]

    lam = lam_ref[...]
    lam_full = (jnp.exp(jnp.sum(lam[0:1] * lam[1:2], keepdims=True))
                - jnp.exp(jnp.sum(lam[2:3] * lam[3:4], keepdims=True)) + lambda_init)

    for i in range(q_ref.shape[1] // t):
        rows = slice(i * t, (i + 1) * t)
        maps = []
        for c in range(2):
            cols = slice(c * HEAD_DIM, (c + 1) * HEAD_DIM)
            maps.append(_softmax_pv(i, q_ref[0, rows, cols], k_ref, v_ref, cols,
                                    table_ref[FAR_BUCKET, 2 * head + c],
                                    bias_ref[c, 1], bias_ref[c, 0]))
        o = maps[0] - lam_full * maps[1]
        o = o * lax.rsqrt(jnp.mean(o * o, axis=-1, keepdims=True) + SUBLN_EPS)
        o_ref[0, rows, :] = (o * subg_ref[...] * (1.0 - lambda_init)).astype(o_ref.dtype)


def _attention(q, k, v, bias_tiles, rel_bias, lam, subln_g, lambda_init):
    b, s, width = q.shape
    t = ATT_TILE
    n_heads = width // V_HEAD_DIM
    assert s % t == 0
    return pl.pallas_call(
        functools.partial(_attn_kernel, lambda_init),
        grid=(b, n_heads),
        in_specs=[
            pl.BlockSpec((1, s, V_HEAD_DIM), lambda bi, h: (bi, 0, h)),
            pl.BlockSpec((1, s, V_HEAD_DIM), lambda bi, h: (bi, 0, h)),
            pl.BlockSpec((1, s, V_HEAD_DIM), lambda bi, h: (bi, 0, h)),
            pl.BlockSpec((2, 2, t, t), lambda bi, h: (h, 0, 0, 0)),
            pl.BlockSpec(memory_space=pltpu.SMEM),
            pl.BlockSpec(lam.shape, lambda bi, h: (0, 0)),
            pl.BlockSpec((1, V_HEAD_DIM), lambda bi, h: (0, 0)),
        ],
        out_specs=pl.BlockSpec((1, s, V_HEAD_DIM), lambda bi, h: (bi, 0, h)),
        out_shape=jax.ShapeDtypeStruct((b, s, width), BF16),
        compiler_params=_params("parallel", "parallel"),
        name="diff_attention",
    )(q, k, v, bias_tiles, rel_bias, lam, subln_g)


def _out_proj_kernel(o_ref, w_ref, x_ref, g_ref, y_ref):
    y_ref[...] = jnp.dot(o_ref[...], w_ref[...], preferred_element_type=F32)
    gain = g_ref[...]

    def body(rows):
        y_ref[rows, :] = x_ref[rows, :] + _rms(y_ref[rows, :], gain, RMS_EPS)
    _for_row_chunks(y_ref.shape[0], body)


def _out_proj(o, w, x, gain):
    m, d = x.shape
    kdim = o.shape[1]
    tm = PROJ_ROWS
    assert m % tm == 0
    return pl.pallas_call(
        _out_proj_kernel,
        grid=(m // tm,),
        in_specs=[
            pl.BlockSpec((tm, kdim), lambda i: (i, 0)),
            pl.BlockSpec((kdim, d), lambda i: (0, 0)),
            pl.BlockSpec((tm, d), lambda i: (i, 0)),
            pl.BlockSpec((1, d), lambda i: (0, 0)),
        ],
        out_specs=pl.BlockSpec((tm, d), lambda i: (i, 0)),
        out_shape=jax.ShapeDtypeStruct((m, d), F32),
        compiler_params=_params("parallel"),
        name="out_proj",
    )(o, w, x, gain)


def kernel(x, norm_gains, ffn_w_gate, ffn_w_up, ffn_w_down, pool_w, pool_scale, kv_norm, w_k, w_v,
           rel_bias, w_q, w_o, lambdas, subln_gain):
    b, s, d = x.shape
    depth = norm_gains.shape[0]
    n_a = pool_w.shape[0]
    gains = norm_gains.reshape(depth, 3, 2, 1, d)
    wg, wu, wd = ffn_w_gate, ffn_w_up, ffn_w_down

    xf = x.reshape(b * s, d)
    k = v = bias_tiles = None
    for l in range(depth):
        g = gains[l]
        if l == n_a:
            k, v = _norm_proj(xf, kv_norm.reshape(1, d), [w_k, w_v])
            bias_tiles = _bias_tiles(rel_bias, ATT_TILE)
        xf = _ffn(xf, g[0, 0], g[0, 1], wg, wu, wd, l, 0)
        if l < n_a:
            xf = _pool_mixer(xf.reshape(b, s, d), g[1, 0], g[1, 1], pool_w[l].astype(BF16),
                             pool_scale[l].reshape(1, d)).reshape(b * s, d)
        else:
            j = l - n_a
            lambda_init = 0.8 - 0.6 * math.exp(-0.3 * l)
            (q,) = _norm_proj(xf, g[1, 0], [w_q[j]])
            o = _attention(q.reshape(b, s, -1), k.reshape(b, s, -1), v.reshape(b, s, -1), bias_tiles,
                           rel_bias, lambdas[j], subln_gain[j].reshape(1, -1), lambda_init)
            xf = _out_proj(o.reshape(b * s, -1), w_o[j].astype(BF16), xf, g[1, 1])
        xf = _ffn(xf, g[2, 0], g[2, 1], wg, wu, wd, l, 1)
    return xf.reshape(b, s, d)
```

```python
import functools
import math

import numpy as np
import jax
import jax.numpy as jnp
from jax import lax
from jax.experimental import pallas as pl
from jax.experimental.pallas import tpu as pltpu

F32 = jnp.float32
BF16 = jnp.bfloat16

CHUNK = 64
POOL_WINDOWS = (2, 4, 8, 16)
HEAD_DIM = 128
V_HEAD_DIM = 2 * HEAD_DIM
NUM_BUCKETS = 32
MAX_DISTANCE = 128
RMS_EPS = 1e-6
SUBLN_EPS = 1e-5

VMEM_LIMIT_BYTES = 60 * 1024 * 1024
FFN_ROWS = 1024
FFN_COLS = 256
FFN_DOWN_COLS = 512
NORM_PROJ_ROWS = 1024
PROJ_ROWS = 512
PROJ_COLS = 512
NORM_ROW_CHUNK = 16
POOL_ROWS = 256
POOL_HALO = 16
ATT_TILE = 256


def _rms(x, gain, eps):
    ms = jnp.mean(x * x, axis=-1, keepdims=True)
    return x * lax.rsqrt(ms + eps) * gain


def _for_row_chunks(n_rows, body):
    for r0 in range(0, n_rows, NORM_ROW_CHUNK):
        body(slice(r0, r0 + NORM_ROW_CHUNK))


def _norm_rows_to(dst_ref, src_ref, gain, eps):
    def body(rows):
        dst_ref[rows, :] = _rms(src_ref[rows, :], gain, eps).astype(dst_ref.dtype)
    _for_row_chunks(src_ref.shape[0], body)


def _params(*semantics):
    return pltpu.CompilerParams(dimension_semantics=semantics,
                                vmem_limit_bytes=VMEM_LIMIT_BYTES)


def _ffn_kernel(layer, half, x_ref, gpre_ref, gpost_ref, wg_hbm, wu_hbm, wd_hbm, o_ref,
                h_ref, wg_buf, wu_buf, wd_buf, sem):
    tf = wg_buf.shape[2]
    n_blocks = wg_hbm.shape[3] // tf

    def block_copies(j, slot):
        c0 = pl.multiple_of(j * tf, tf)
        return (
            pltpu.make_async_copy(wg_hbm.at[layer, half, :, pl.ds(c0, tf)], wg_buf.at[slot],
                                  sem.at[0, slot]),
            pltpu.make_async_copy(wu_hbm.at[layer, half, :, pl.ds(c0, tf)], wu_buf.at[slot],
                                  sem.at[1, slot]),
            pltpu.make_async_copy(wd_hbm.at[layer, half, pl.ds(c0, tf), :], wd_buf.at[slot],
                                  sem.at[2, slot]),
        )

    for copy in block_copies(0, 0):
        copy.start()
    _norm_rows_to(h_ref, x_ref, gpre_ref[...], RMS_EPS)
    o_ref[...] = jnp.zeros(o_ref.shape, F32)

    def block_pair(jj, carry):
        for slot in range(2):
            j = 2 * jj + slot
            for copy in block_copies(j, slot):
                copy.wait()

            @pl.when(j + 1 < n_blocks)
            def _():
                for copy in block_copies(j + 1, 1 - slot):
                    copy.start()

            h = h_ref[...]
            g = jnp.dot(h, wg_buf[slot].astype(BF16), preferred_element_type=F32)
            u = jnp.dot(h, wu_buf[slot].astype(BF16), preferred_element_type=F32)
            a = (g * (1.0 / (1.0 + jnp.exp(-g))) * u).astype(BF16)
            for c0 in range(0, o_ref.shape[1], FFN_DOWN_COLS):
                cols = slice(c0, c0 + FFN_DOWN_COLS)
                o_ref[:, cols] += jnp.dot(a, wd_buf[slot, :, cols].astype(BF16),
                                          preferred_element_type=F32)
        return carry

    lax.fori_loop(0, n_blocks // 2, block_pair, 0)

    half_gain = 0.5 * gpost_ref[...]

    def finish(rows):
        o_ref[rows, :] = x_ref[rows, :] + _rms(o_ref[rows, :], half_gain, RMS_EPS)
    _for_row_chunks(o_ref.shape[0], finish)


def _ffn(x, gpre, gpost, wg, wu, wd, layer, half):
    m, d = x.shape
    ff = wg.shape[-1]
    tm, tf = FFN_ROWS, FFN_COLS
    assert m % tm == 0 and ff % (2 * tf) == 0
    return pl.pallas_call(
        functools.partial(_ffn_kernel, layer, half),
        grid=(m // tm,),
        in_specs=[
            pl.BlockSpec((tm, d), lambda i: (i, 0)),
            pl.BlockSpec((1, d), lambda i: (0, 0)),
            pl.BlockSpec((1, d), lambda i: (0, 0)),
            pl.BlockSpec(memory_space=pl.ANY),
            pl.BlockSpec(memory_space=pl.ANY),
            pl.BlockSpec(memory_space=pl.ANY),
        ],
        out_specs=pl.BlockSpec((tm, d), lambda i: (i, 0)),
        out_shape=jax.ShapeDtypeStruct((m, d), F32),
        scratch_shapes=[
            pltpu.VMEM((tm, d), BF16),
            pltpu.VMEM((2, d, tf), F32),
            pltpu.VMEM((2, d, tf), F32),
            pltpu.VMEM((2, tf, d), F32),
            pltpu.SemaphoreType.DMA((3, 2)),
        ],
        compiler_params=_params("parallel"),
        name="ffn",
    )(x, gpre, gpost, wg, wu, wd)


def _pool_kernel(x_ref, halo_ref, gpre_ref, gpost_ref, w_ref, scale_ref, o_ref, hx_ref):
    i = pl.program_id(1)
    ts = x_ref.shape[1]
    gdim = w_ref.shape[1]
    x = x_ref[0]
    h = _rms(x, gpre_ref[...], RMS_EPS)
    hh = _rms(halo_ref[0], gpre_ref[...], RMS_EPS)
    hx_ref[0:POOL_HALO, :] = jnp.where(i > 0, hh, 0.0)
    hx_ref[POOL_HALO:POOL_HALO + ts, :] = h

    t = i * ts + lax.broadcasted_iota(jnp.int32, (ts, 1), 0)
    ys = []
    for gi, win in enumerate(POOL_WINDOWS):
        c0 = gi * gdim
        acc = h[:, c0:c0 + gdim]
        for k in range(1, win):
            acc = acc + hx_ref[POOL_HALO - k:POOL_HALO - k + ts, c0:c0 + gdim]
        inv_cnt = 1.0 / jnp.minimum(t + 1, win).astype(F32)
        dg = acc * inv_cnt - h[:, c0:c0 + gdim]
        ys.append(jnp.dot(dg.astype(BF16), w_ref[gi], preferred_element_type=F32))
    y = jnp.concatenate(ys, axis=-1) * scale_ref[...]
    o_ref[0] = x + _rms(y, gpost_ref[...], RMS_EPS)


def _pool_mixer(x, gpre, gpost, w, scale):
    b, s, d = x.shape
    ts = POOL_ROWS
    assert s % ts == 0 and ts % POOL_HALO == 0
    halo_blocks = ts // POOL_HALO
    return pl.pallas_call(
        _pool_kernel,
        grid=(b, s // ts),
        in_specs=[
            pl.BlockSpec((1, ts, d), lambda bi, i: (bi, i, 0)),
            pl.BlockSpec((1, POOL_HALO, d),
                         lambda bi, i: (bi, jnp.maximum(i * halo_blocks - 1, 0), 0)),
            pl.BlockSpec((1, d), lambda bi, i: (0, 0)),
            pl.BlockSpec((1, d), lambda bi, i: (0, 0)),
            pl.BlockSpec(w.shape, lambda bi, i: (0, 0, 0)),
            pl.BlockSpec((1, d), lambda bi, i: (0, 0)),
        ],
        out_specs=pl.BlockSpec((1, ts, d), lambda bi, i: (bi, i, 0)),
        out_shape=jax.ShapeDtypeStruct((b, s, d), F32),
        scratch_shapes=[pltpu.VMEM((POOL_HALO + ts, d), F32)],
        compiler_params=_params("parallel", "parallel"),
        name="pool_mixer",
    )(x, x, gpre, gpost, w, scale)


def _norm_proj_kernel(n_out, x_ref, g_ref, *refs):
    w_refs, o_refs, h_ref = refs[:n_out], refs[n_out:2 * n_out], refs[2 * n_out]

    @pl.when(pl.program_id(1) == 0)
    def _():
        _norm_rows_to(h_ref, x_ref, g_ref[...], RMS_EPS)

    h = h_ref[...]
    for w_ref, o_ref in zip(w_refs, o_refs):
        o_ref[...] = jnp.dot(h, w_ref[...].astype(BF16),
                             preferred_element_type=F32).astype(o_ref.dtype)


def _norm_proj(x, gain, weights):
    m, d = x.shape
    n = weights[0].shape[1]
    tm, tn = NORM_PROJ_ROWS, PROJ_COLS
    assert m % tm == 0 and n % tn == 0
    n_out = len(weights)
    outs = pl.pallas_call(
        functools.partial(_norm_proj_kernel, n_out),
        grid=(m // tm, n // tn),
        in_specs=[pl.BlockSpec((tm, d), lambda i, j: (i, 0)),
                  pl.BlockSpec((1, d), lambda i, j: (0, 0))]
                 + [pl.BlockSpec((d, tn), lambda i, j: (0, j))] * n_out,
        out_specs=[pl.BlockSpec((tm, tn), lambda i, j: (i, j))] * n_out,
        out_shape=[jax.ShapeDtypeStruct((m, n), BF16)] * n_out,
        scratch_shapes=[pltpu.VMEM((tm, d), BF16)],
        compiler_params=_params("parallel", "arbitrary"),
        name="norm_proj",
    )(x, gain, *weights)
    return outs


def _rel_bucket(rel):
    half = NUM_BUCKETS // 2
    max_exact = half // 2
    ret = jnp.where(rel > 0, half, 0)
    n = jnp.abs(rel)
    nf = jnp.maximum(n, 1).astype(F32)
    large = max_exact + (jnp.log(nf / max_exact) / math.log(MAX_DISTANCE / max_exact)
                         * (half - max_exact)).astype(jnp.int32)
    large = jnp.minimum(large, half - 1)
    return ret + jnp.where(n < max_exact, n, large)


FAR_BUCKET = NUM_BUCKETS // 2 - 1
FAR_DISTANCE = MAX_DISTANCE


def _bias_tile_constants(t):
    assert t + 1 >= FAR_DISTANCE and t % CHUNK == 0
    q = np.arange(t, dtype=np.int32)[:, None]
    k = np.arange(t, dtype=np.int32)[None, :]
    buckets = jnp.stack([_rel_bucket(jnp.asarray(k - q)), _rel_bucket(jnp.asarray(k - t - q))])
    allowed = np.stack([(k // CHUNK) <= (q // CHUNK), np.ones((t, t), bool)])
    mask = np.where(allowed, 0.0, -np.inf).astype(np.float32)
    return buckets, jnp.asarray(mask)


def _bias_kernel(table_ref, bucket_ref, mask_ref, o_ref):
    m = pl.program_id(0)
    bucket = bucket_ref[...]
    acc = mask_ref[...]
    for b in range(NUM_BUCKETS):
        acc = acc + jnp.where(bucket == b, table_ref[b, m], 0.0)
    o_ref[0] = acc


def _bias_tiles(rel_bias, t):
    buckets, mask = _bias_tile_constants(t)
    n_maps = rel_bias.shape[1]
    return pl.pallas_call(
        _bias_kernel,
        grid=(n_maps,),
        in_specs=[
            pl.BlockSpec(memory_space=pltpu.SMEM),
            pl.BlockSpec((2, t, t), lambda m: (0, 0, 0)),
            pl.BlockSpec((2, t, t), lambda m: (0, 0, 0)),
        ],
        out_specs=pl.BlockSpec((1, 2, t, t), lambda m: (m, 0, 0, 0)),
        out_shape=jax.ShapeDtypeStruct((n_maps, 2, t, t), F32),
        compiler_params=_params("parallel"),
        name="bias_tiles",
    )(rel_bias, buckets, mask)


def _softmax_pv(qi, qc, k_ref, v_ref, cols, far_bias, near_bias, diag_bias):
    t = qc.shape[0]
    scale = HEAD_DIM ** -0.5
    pieces = []
    if qi >= 2:
        pieces.append((0, (qi - 1) * t, far_bias))
    if qi >= 1:
        pieces.append(((qi - 1) * t, t, near_bias))
    pieces.append((qi * t, t, diag_bias))

    scores = []
    for start, size, bias in pieces:
        kc = k_ref[0, start:start + size, cols]
        s = lax.dot_general(qc, kc, (((1,), (1,)), ((), ())), preferred_element_type=F32)
        scores.append(s * scale + bias)
    m = functools.reduce(jnp.maximum, [jnp.max(s, axis=-1, keepdims=True) for s in scores])
    probs = [jnp.exp(s - m) for s in scores]
    denom = sum(jnp.sum(p, axis=-1, keepdims=True) for p in probs)
    acc = sum(jnp.dot(p.astype(BF16), v_ref[0, start:start + size, :], preferred_element_type=F32)
              for p, (start, size, _) in zip(probs, pieces))
    return acc * (1.0 / denom)


def _attn_kernel(lambda_init, q_ref, k_ref, v_ref, bias_ref, table_ref, lam_ref, subg_ref, o_ref):
    head = pl.program_id(1)
    t = bias_ref.shape[-1]

    lam = lam_ref[...]
    lam_full = (jnp.exp(jnp.sum(lam[0:1] * lam[1:2], keepdims=True))
                - jnp.exp(jnp.sum(lam[2:3] * lam[3:4], keepdims=True)) + lambda_init)

    for i in range(q_ref.shape[1] // t):
        rows = slice(i * t, (i + 1) * t)
        maps = []
        for c in range(2):
            cols = slice(c * HEAD_DIM, (c + 1) * HEAD_DIM)
            maps.append(_softmax_pv(i, q_ref[0, rows, cols], k_ref, v_ref, cols,
                                    table_ref[FAR_BUCKET, 2 * head + c],
                                    bias_ref[c, 1], bias_ref[c, 0]))
        o = maps[0] - lam_full * maps[1]
        o = o * lax.rsqrt(jnp.mean(o * o, axis=-1, keepdims=True) + SUBLN_EPS)
        o_ref[0, rows, :] = (o * subg_ref[...] * (1.0 - lambda_init)).astype(o_ref.dtype)


def _attention(q, k, v, bias_tiles, rel_bias, lam, subln_g, lambda_init):
    b, s, width = q.shape
    t = ATT_TILE
    n_heads = width // V_HEAD_DIM
    assert s % t == 0
    return pl.pallas_call(
        functools.partial(_attn_kernel, lambda_init),
        grid=(b, n_heads),
        in_specs=[
            pl.BlockSpec((1, s, V_HEAD_DIM), lambda bi, h: (bi, 0, h)),
            pl.BlockSpec((1, s, V_HEAD_DIM), lambda bi, h: (bi, 0, h)),
            pl.BlockSpec((1, s, V_HEAD_DIM), lambda bi, h: (bi, 0, h)),
            pl.BlockSpec((2, 2, t, t), lambda bi, h: (h, 0, 0, 0)),
            pl.BlockSpec(memory_space=pltpu.SMEM),
            pl.BlockSpec(lam.shape, lambda bi, h: (0, 0)),
            pl.BlockSpec((1, V_HEAD_DIM), lambda bi, h: (0, 0)),
        ],
        out_specs=pl.BlockSpec((1, s, V_HEAD_DIM), lambda bi, h: (bi, 0, h)),
        out_shape=jax.ShapeDtypeStruct((b, s, width), BF16),
        compiler_params=_params("parallel", "parallel"),
        name="diff_attention",
    )(q, k, v, bias_tiles, rel_bias, lam, subln_g)


def _out_proj_kernel(o_ref, w_ref, x_ref, g_ref, y_ref):
    y_ref[...] = jnp.dot(o_ref[...], w_ref[...], preferred_element_type=F32)
    gain = g_ref[...]

    def body(rows):
        y_ref[rows, :] = x_ref[rows, :] + _rms(y_ref[rows, :], gain, RMS_EPS)
    _for_row_chunks(y_ref.shape[0], body)


def _out_proj(o, w, x, gain):
    m, d = x.shape
    kdim = o.shape[1]
    tm = PROJ_ROWS
    assert m % tm == 0
    return pl.pallas_call(
        _out_proj_kernel,
        grid=(m // tm,),
        in_specs=[
            pl.BlockSpec((tm, kdim), lambda i: (i, 0)),
            pl.BlockSpec((kdim, d), lambda i: (0, 0)),
            pl.BlockSpec((tm, d), lambda i: (i, 0)),
            pl.BlockSpec((1, d), lambda i: (0, 0)),
        ],
        out_specs=pl.BlockSpec((tm, d), lambda i: (i, 0)),
        out_shape=jax.ShapeDtypeStruct((m, d), F32),
        compiler_params=_params("parallel"),
        name="out_proj",
    )(o, w, x, gain)


def kernel(x, norm_gains, ffn_w_gate, ffn_w_up, ffn_w_down, pool_w, pool_scale, kv_norm, w_k, w_v,
           rel_bias, w_q, w_o, lambdas, subln_gain):
    b, s, d = x.shape
    depth = norm_gains.shape[0]
    n_a = pool_w.shape[0]
    gains = norm_gains.reshape(depth, 3, 2, 1, d)
    wg, wu, wd = ffn_w_gate, ffn_w_up, ffn_w_down

    xf = x.reshape(b * s, d)
    k = v = bias_tiles = None
    for l in range(depth):
        g = gains[l]
        if l == n_a:
            k, v = _norm_proj(xf, kv_norm.reshape(1, d), [w_k, w_v])
            bias_tiles = _bias_tiles(rel_bias, ATT_TILE)
        xf = _ffn(xf, g[0, 0], g[0, 1], wg, wu, wd, l, 0)
        if l < n_a:
            xf = _pool_mixer(xf.reshape(b, s, d), g[1, 0], g[1, 1], pool_w[l].astype(BF16),
                             pool_scale[l].reshape(1, d)).reshape(b * s, d)
        else:
            j = l - n_a
            lambda_init = 0.8 - 0.6 * math.exp(-0.3 * l)
            (q,) = _norm_proj(xf, g[1, 0], [w_q[j]])
            o = _attention(q.reshape(b, s, -1), k.reshape(b, s, -1), v.reshape(b, s, -1), bias_tiles,
                           rel_bias, lambdas[j], subln_gain[j].reshape(1, -1), lambda_init)
            xf = _out_proj(o.reshape(b * s, -1), w_o[j].astype(BF16), xf, g[1, 1])
        xf = _ffn(xf, g[2, 0], g[2, 1], wg, wu, wd, l, 1)
    return xf.reshape(b, s, d)
```

```python
import functools
import math

import numpy as np
import jax
import jax.numpy as jnp
from jax import lax
from jax.experimental import pallas as pl
from jax.experimental.pallas import tpu as pltpu

F32 = jnp.float32
BF16 = jnp.bfloat16

CHUNK = 64
POOL_WINDOWS = (2, 4, 8, 16)
HEAD_DIM = 128
V_HEAD_DIM = 2 * HEAD_DIM
NUM_BUCKETS = 32
MAX_DISTANCE = 128
RMS_EPS = 1e-6
SUBLN_EPS = 1e-5

VMEM_LIMIT_BYTES = 60 * 1024 * 1024
FFN_ROWS = 1024
FFN_COLS = 256
FFN_DOWN_COLS = 512
NORM_PROJ_ROWS = 1024
PROJ_ROWS = 512
PROJ_COLS = 512
NORM_ROW_CHUNK = 16
POOL_ROWS = 256
POOL_STAGES = 4
POOL_HALO = 8 * POOL_STAGES
ATT_TILE = 256
STAGGER = 1


def _rms(x, gain, eps):
    ms = jnp.mean(x * x, axis=-1, keepdims=True)
    return x * lax.rsqrt(ms + eps) * gain


def _for_row_chunks(n_rows, body):
    for r0 in range(0, n_rows, NORM_ROW_CHUNK):
        body(slice(r0, r0 + NORM_ROW_CHUNK))


def _norm_rows_to(dst_ref, src_ref, gain, eps):
    def body(rows):
        dst_ref[rows, :] = _rms(src_ref[rows, :], gain, eps).astype(dst_ref.dtype)
    _for_row_chunks(src_ref.shape[0], body)


def _params(*semantics):
    return pltpu.CompilerParams(dimension_semantics=semantics,
                                vmem_limit_bytes=VMEM_LIMIT_BYTES)


def _ffn_kernel(x_ref, gpre_ref, gpost_ref, wg_ref, wu_ref, wd_ref, o_ref, h_ref):
    j = pl.program_id(1)

    @pl.when(j == 0)
    def _():
        _norm_rows_to(h_ref, x_ref, gpre_ref[...], RMS_EPS)
        o_ref[...] = jnp.zeros(o_ref.shape, F32)

    h = h_ref[...]
    g = jnp.dot(h, wg_ref[...].astype(BF16), preferred_element_type=F32)
    u = jnp.dot(h, wu_ref[...].astype(BF16), preferred_element_type=F32)
    a = (g * (1.0 / (1.0 + jnp.exp(-g))) * u).astype(BF16)
    for c0 in range(0, o_ref.shape[1], FFN_DOWN_COLS):
        cols = slice(c0, c0 + FFN_DOWN_COLS)
        o_ref[:, cols] += jnp.dot(a, wd_ref[:, cols].astype(BF16), preferred_element_type=F32)

    @pl.when(j == pl.num_programs(1) - 1)
    def _():
        half_gain = 0.5 * gpost_ref[...]

        def body(rows):
            o_ref[rows, :] = x_ref[rows, :] + _rms(o_ref[rows, :], half_gain, RMS_EPS)
        _for_row_chunks(o_ref.shape[0], body)


def _ffn(x, gpre, gpost, wg, wu, wd, layer, half):
    m, d = x.shape
    ff = wg.shape[-1]
    tm, tf = FFN_ROWS, FFN_COLS
    assert m % tm == 0 and ff % tf == 0
    return pl.pallas_call(
        _ffn_kernel,
        grid=(m // tm, ff // tf),
        in_specs=[
            pl.BlockSpec((tm, d), lambda i, j: (i, 0)),
            pl.BlockSpec((1, d), lambda i, j: (0, 0)),
            pl.BlockSpec((1, d), lambda i, j: (0, 0)),
            pl.BlockSpec((None, None, d, tf), lambda i, j: (layer, half, 0, j)),
            pl.BlockSpec((None, None, d, tf), lambda i, j: (layer, half, 0, j)),
            pl.BlockSpec((None, None, tf, d), lambda i, j: (layer, half, j, 0)),
        ],
        out_specs=pl.BlockSpec((tm, d), lambda i, j: (i, 0)),
        out_shape=jax.ShapeDtypeStruct((m, d), F32),
        scratch_shapes=[pltpu.VMEM((tm, d), BF16)],
        compiler_params=_params("parallel", "arbitrary"),
        name="ffn",
    )(x, gpre, gpost, wg, wu, wd)


def _pool_kernel(x_ref, halo_ref, gpre_ref, gpost_ref, w_ref, scale_ref, o_ref, *sum_refs):
    i = pl.program_id(1)
    ts = x_ref.shape[1]
    d = x_ref.shape[2]
    gdim = w_ref.shape[1]
    rows_all = POOL_HALO + ts
    hx_ref = sum_refs[0]
    gpre = gpre_ref[...]

    hh = _rms(halo_ref[0], gpre, RMS_EPS)
    hx_ref[0:POOL_HALO, :] = jnp.where(i > 0, hh, 0.0)
    for r0 in range(0, ts, NORM_ROW_CHUNK):
        hx_ref[POOL_HALO + r0:POOL_HALO + r0 + NORM_ROW_CHUNK, :] = _rms(
            x_ref[0, r0:r0 + NORM_ROW_CHUNK, :], gpre, RMS_EPS)

    for s in range(1, len(sum_refs)):
        shift = 2 ** (s - 1)
        r0 = 8 * s
        c0 = (s - 1) * gdim
        prev = sum_refs[s - 1]
        sum_refs[s][r0:rows_all, c0:d] = (prev[r0:rows_all, c0:d]
                                          + prev[r0 - shift:rows_all - shift, c0:d])

    t = i * ts + lax.broadcasted_iota(jnp.int32, (ts, 1), 0)
    main = slice(POOL_HALO, rows_all)
    for gi, win in enumerate(POOL_WINDOWS):
        cols = slice(gi * gdim, (gi + 1) * gdim)
        stage = win.bit_length() - 1
        if stage < len(sum_refs):
            total = sum_refs[stage][main, cols]
        else:
            prev = sum_refs[stage - 1]
            half = win // 2
            total = prev[main, cols] + prev[POOL_HALO - half:rows_all - half, cols]
        inv_cnt = 1.0 / jnp.minimum(t + 1, win).astype(F32)
        dg = total * inv_cnt - hx_ref[main, cols]
        o_ref[0, :, cols] = (jnp.dot(dg.astype(BF16), w_ref[gi], preferred_element_type=F32)
                             * scale_ref[:, cols])

    gpost = gpost_ref[...]
    for r0 in range(0, ts, NORM_ROW_CHUNK):
        rows = slice(r0, r0 + NORM_ROW_CHUNK)
        o_ref[0, rows, :] = x_ref[0, rows, :] + _rms(o_ref[0, rows, :], gpost, RMS_EPS)


def _pool_mixer(x, gpre, gpost, w, scale):
    b, s, d = x.shape
    ts = POOL_ROWS
    assert s % ts == 0 and ts % POOL_HALO == 0
    halo_blocks = ts // POOL_HALO
    return pl.pallas_call(
        _pool_kernel,
        grid=(b, s // ts),
        in_specs=[
            pl.BlockSpec((1, ts, d), lambda bi, i: (bi, i, 0)),
            pl.BlockSpec((1, POOL_HALO, d),
                         lambda bi, i: (bi, jnp.maximum(i * halo_blocks - 1, 0), 0)),
            pl.BlockSpec((1, d), lambda bi, i: (0, 0)),
            pl.BlockSpec((1, d), lambda bi, i: (0, 0)),
            pl.BlockSpec(w.shape, lambda bi, i: (0, 0, 0)),
            pl.BlockSpec((1, d), lambda bi, i: (0, 0)),
        ],
        out_specs=pl.BlockSpec((1, ts, d), lambda bi, i: (bi, i, 0)),
        out_shape=jax.ShapeDtypeStruct((b, s, d), F32),
        scratch_shapes=[pltpu.VMEM((POOL_HALO + ts, d), F32)] * POOL_STAGES,
        compiler_params=_params("parallel", "parallel"),
        name="pool_mixer",
    )(x, x, gpre, gpost, w, scale)


def _norm_proj_kernel(n_out, out_scale, x_ref, g_ref, *refs):
    w_refs, o_refs, h_ref = refs[:n_out], refs[n_out:2 * n_out], refs[2 * n_out]

    @pl.when(pl.program_id(1) == 0)
    def _():
        _norm_rows_to(h_ref, x_ref, g_ref[...], RMS_EPS)

    h = h_ref[...]
    for w_ref, o_ref in zip(w_refs, o_refs):
        y = jnp.dot(h, w_ref[...].astype(BF16), preferred_element_type=F32)
        if out_scale != 1.0:
            y = y * out_scale
        o_ref[...] = y.astype(o_ref.dtype)


def _norm_proj(x, gain, weights, out_scale=1.0):
    m, d = x.shape
    n = weights[0].shape[1]
    tm, tn = NORM_PROJ_ROWS, PROJ_COLS
    assert m % tm == 0 and n % tn == 0
    n_out = len(weights)
    outs = pl.pallas_call(
        functools.partial(_norm_proj_kernel, n_out, out_scale),
        grid=(m // tm, n // tn),
        in_specs=[pl.BlockSpec((tm, d), lambda i, j: (i, 0)),
                  pl.BlockSpec((1, d), lambda i, j: (0, 0))]
                 + [pl.BlockSpec((d, tn), lambda i, j: (0, j))] * n_out,
        out_specs=[pl.BlockSpec((tm, tn), lambda i, j: (i, j))] * n_out,
        out_shape=[jax.ShapeDtypeStruct((m, n), BF16)] * n_out,
        scratch_shapes=[pltpu.VMEM((tm, d), BF16)],
        compiler_params=_params("parallel", "arbitrary"),
        name="norm_proj",
    )(x, gain, *weights)
    return outs


def _rel_bucket(rel):
    half = NUM_BUCKETS // 2
    max_exact = half // 2
    ret = jnp.where(rel > 0, half, 0)
    n = jnp.abs(rel)
    nf = jnp.maximum(n, 1).astype(F32)
    large = max_exact + (jnp.log(nf / max_exact) / math.log(MAX_DISTANCE / max_exact)
                         * (half - max_exact)).astype(jnp.int32)
    large = jnp.minimum(large, half - 1)
    return ret + jnp.where(n < max_exact, n, large)


FAR_BUCKET = NUM_BUCKETS // 2 - 1
FAR_DISTANCE = MAX_DISTANCE


def _bias_tile_constants(t):
    assert t + 1 >= FAR_DISTANCE and t % CHUNK == 0
    q = np.arange(t, dtype=np.int32)[:, None]
    k = np.arange(t, dtype=np.int32)[None, :]
    buckets = jnp.stack([_rel_bucket(jnp.asarray(k - q)), _rel_bucket(jnp.asarray(k - t - q))])
    allowed = np.stack([(k // CHUNK) <= (q // CHUNK), np.ones((t, t), bool)])
    mask = np.where(allowed, 0.0, -np.inf).astype(np.float32)
    return buckets, jnp.asarray(mask)


def _bias_kernel(table_ref, bucket_ref, mask_ref, o_ref):
    m = pl.program_id(0)
    bucket = bucket_ref[...]
    acc = mask_ref[...]
    for b in range(NUM_BUCKETS):
        acc = acc + jnp.where(bucket == b, table_ref[b, m], 0.0)
    o_ref[0] = acc


def _bias_tiles(rel_bias, t):
    buckets, mask = _bias_tile_constants(t)
    n_maps = rel_bias.shape[1]
    return pl.pallas_call(
        _bias_kernel,
        grid=(n_maps,),
        in_specs=[
            pl.BlockSpec(memory_space=pltpu.SMEM),
            pl.BlockSpec((2, t, t), lambda m: (0, 0, 0)),
            pl.BlockSpec((2, t, t), lambda m: (0, 0, 0)),
        ],
        out_specs=pl.BlockSpec((1, 2, t, t), lambda m: (m, 0, 0, 0)),
        out_shape=jax.ShapeDtypeStruct((n_maps, 2, t, t), F32),
        compiler_params=_params("parallel"),
        name="bias_tiles",
    )(rel_bias, buckets, mask)


def _softmax_pv(qi, t, q_ref, k_ref, v_ref, cols, far_bias, bias_ref):
    qc = q_ref[0, qi * t:(qi + 1) * t, cols]
    near_bias, diag_bias = bias_ref.at[1], bias_ref.at[0]
    pieces = []
    if qi >= 2:
        pieces.append((0, (qi - 1) * t, None))
    if qi >= 1:
        pieces.append(((qi - 1) * t, t, near_bias))
    pieces.append((qi * t, t, diag_bias))

    scores, row_max = [], []
    for start, size, bias in pieces:
        kc = k_ref[0, start:start + size, cols]
        s = lax.dot_general(qc, kc, (((1,), (1,)), ((), ())), preferred_element_type=F32)
        if bias is None:
            row_max.append(jnp.max(s, axis=-1, keepdims=True) + far_bias)
        else:
            s = s + bias[...]
            row_max.append(jnp.max(s, axis=-1, keepdims=True))
        scores.append(s)
    m = functools.reduce(jnp.maximum, row_max)
    yield
    probs = [jnp.exp(s - ((m - far_bias) if bias is None else m))
             for s, (_, _, bias) in zip(scores, pieces)]
    denom = sum(jnp.sum(p, axis=-1, keepdims=True) for p in probs)
    yield
    acc = sum(jnp.dot(p.astype(BF16), v_ref[0, start:start + size, :], preferred_element_type=F32)
              for p, (start, size, _) in zip(probs, pieces))
    return acc * (1.0 / denom)


def _run_staggered(tasks, n_phases, on_done):
    for step in range(len(tasks) + STAGGER * (n_phases - 1)):
        for phase in range(n_phases):
            n = step - STAGGER * phase
            if 0 <= n < len(tasks):
                try:
                    next(tasks[n])
                except StopIteration as done:
                    on_done(n, done.value)


def _attn_kernel(lambda_init, q_ref, k_ref, v_ref, bias_ref, table_ref, lam_ref, subg_ref, o_ref):
    head = pl.program_id(1)
    t = bias_ref.shape[-1]

    lam = lam_ref[...]
    lam_full = (jnp.exp(jnp.sum(lam[0:1] * lam[1:2], keepdims=True))
                - jnp.exp(jnp.sum(lam[2:3] * lam[3:4], keepdims=True)) + lambda_init)

    tiles = list(reversed(range(q_ref.shape[1] // t)))
    tasks = []
    for i in tiles:
        for c in range(2):
            cols = slice(c * HEAD_DIM, (c + 1) * HEAD_DIM)
            tasks.append(_softmax_pv(i, t, q_ref, k_ref, v_ref, cols,
                                     table_ref[FAR_BUCKET, 2 * head + c], bias_ref.at[c]))

    first_map = {}

    def on_done(n, value):
        if n % 2 == 0:
            first_map[n // 2] = value
            return
        i = tiles[n // 2]
        o = first_map.pop(n // 2) - lam_full * value
        o = o * lax.rsqrt(jnp.mean(o * o, axis=-1, keepdims=True) + SUBLN_EPS)
        o_ref[0, i * t:(i + 1) * t, :] = (o * subg_ref[...] * (1.0 - lambda_init)).astype(o_ref.dtype)

    _run_staggered(tasks, 3, on_done)


def _attention(q, k, v, bias_tiles, rel_bias, lam, subln_g, lambda_init):
    b, s, width = q.shape
    t = ATT_TILE
    n_heads = width // V_HEAD_DIM
    assert s % t == 0
    return pl.pallas_call(
        functools.partial(_attn_kernel, lambda_init),
        grid=(b, n_heads),
        in_specs=[
            pl.BlockSpec((1, s, V_HEAD_DIM), lambda bi, h: (bi, 0, h)),
            pl.BlockSpec((1, s, V_HEAD_DIM), lambda bi, h: (bi, 0, h)),
            pl.BlockSpec((1, s, V_HEAD_DIM), lambda bi, h: (bi, 0, h)),
            pl.BlockSpec((2, 2, t, t), lambda bi, h: (h, 0, 0, 0)),
            pl.BlockSpec(memory_space=pltpu.SMEM),
            pl.BlockSpec(lam.shape, lambda bi, h: (0, 0)),
            pl.BlockSpec((1, V_HEAD_DIM), lambda bi, h: (0, 0)),
        ],
        out_specs=pl.BlockSpec((1, s, V_HEAD_DIM), lambda bi, h: (bi, 0, h)),
        out_shape=jax.ShapeDtypeStruct((b, s, width), BF16),
        compiler_params=_params("parallel", "parallel"),
        name="diff_attention",
    )(q, k, v, bias_tiles, rel_bias, lam, subln_g)


def _out_proj_kernel(o_ref, w_ref, x_ref, g_ref, y_ref):
    y_ref[...] = jnp.dot(o_ref[...], w_ref[...], preferred_element_type=F32)
    gain = g_ref[...]

    def body(rows):
        y_ref[rows, :] = x_ref[rows, :] + _rms(y_ref[rows, :], gain, RMS_EPS)
    _for_row_chunks(y_ref.shape[0], body)


def _out_proj(o, w, x, gain):
    m, d = x.shape
    kdim = o.shape[1]
    tm = PROJ_ROWS
    assert m % tm == 0
    return pl.pallas_call(
        _out_proj_kernel,
        grid=(m // tm,),
        in_specs=[
            pl.BlockSpec((tm, kdim), lambda i: (i, 0)),
            pl.BlockSpec((kdim, d), lambda i: (0, 0)),
            pl.BlockSpec((tm, d), lambda i: (i, 0)),
            pl.BlockSpec((1, d), lambda i: (0, 0)),
        ],
        out_specs=pl.BlockSpec((tm, d), lambda i: (i, 0)),
        out_shape=jax.ShapeDtypeStruct((m, d), F32),
        compiler_params=_params("parallel"),
        name="out_proj",
    )(o, w, x, gain)


def kernel(x, norm_gains, ffn_w_gate, ffn_w_up, ffn_w_down, pool_w, pool_scale, kv_norm, w_k, w_v,
           rel_bias, w_q, w_o, lambdas, subln_gain):
    b, s, d = x.shape
    depth = norm_gains.shape[0]
    n_a = pool_w.shape[0]
    gains = norm_gains.reshape(depth, 3, 2, 1, d)
    wg, wu, wd = ffn_w_gate, ffn_w_up, ffn_w_down

    xf = x.reshape(b * s, d)
    k = v = bias_tiles = None
    for l in range(depth):
        g = gains[l]
        if l == n_a:
            k, v = _norm_proj(xf, kv_norm.reshape(1, d), [w_k, w_v])
            bias_tiles = _bias_tiles(rel_bias, ATT_TILE)
        xf = _ffn(xf, g[0, 0], g[0, 1], wg, wu, wd, l, 0)
        if l < n_a:
            xf = _pool_mixer(xf.reshape(b, s, d), g[1, 0], g[1, 1], pool_w[l].astype(BF16),
                             pool_scale[l].reshape(1, d)).reshape(b * s, d)
        else:
            j = l - n_a
            lambda_init = 0.8 - 0.6 * math.exp(-0.3 * l)
            (q,) = _norm_proj(xf, g[1, 0], [w_q[j]], out_scale=HEAD_DIM ** -0.5)
            o = _attention(q.reshape(b, s, -1), k.reshape(b, s, -1), v.reshape(b, s, -1), bias_tiles,
                           rel_bias, lambdas[j], subln_gain[j].reshape(1, -1), lambda_init)
            xf = _out_proj(o.reshape(b * s, -1), w_o[j].astype(BF16), xf, g[1, 1])
        xf = _ffn(xf, g[2, 0], g[2, 1], wg, wu, wd, l, 1)
    return xf.reshape(b, s, d)
```

```python
import functools
import math

import numpy as np
import jax
import jax.numpy as jnp
from jax import lax
from jax.experimental import pallas as pl
from jax.experimental.pallas import tpu as pltpu

F32 = jnp.float32
BF16 = jnp.bfloat16

CHUNK = 64
POOL_WINDOWS = (2, 4, 8, 16)
HEAD_DIM = 128
V_HEAD_DIM = 2 * HEAD_DIM
NUM_BUCKETS = 32
MAX_DISTANCE = 128
RMS_EPS = 1e-6
SUBLN_EPS = 1e-5

LANES = 128

VMEM_LIMIT_BYTES = 60 * 1024 * 1024
FFN_ROWS = 1024
FFN_COLS = 256
FFN_DOWN_COLS = 512
NORM_PROJ_ROWS = 1024
PROJ_ROWS = 512
PROJ_COLS = 512
NORM_ROW_CHUNK = 16
POOL_ROWS = 256
POOL_STAGES = 4
POOL_HALO = 8 * POOL_STAGES
ATT_TILE = 256
STAGGER = 1


def _rms(x, gain, eps):
    ms = jnp.mean(x * x, axis=-1, keepdims=True)
    return x * lax.rsqrt(ms + eps) * gain


def _for_row_chunks(n_rows, body):
    for r0 in range(0, n_rows, NORM_ROW_CHUNK):
        body(slice(r0, r0 + NORM_ROW_CHUNK))


def _norm_rows_to(dst_ref, src_ref, gain, eps):
    def body(rows):
        dst_ref[rows, :] = _rms(src_ref[rows, :], gain, eps).astype(dst_ref.dtype)
    _for_row_chunks(src_ref.shape[0], body)


def _params(*semantics):
    return pltpu.CompilerParams(dimension_semantics=semantics,
                                vmem_limit_bytes=VMEM_LIMIT_BYTES)


def _ffn_kernel(x_ref, gpre_ref, gpost_ref, wg_ref, wu_ref, wd_ref, o_ref, h_ref):
    j = pl.program_id(1)

    def swiglu_block(first):
        h = h_ref[...]
        g = jnp.dot(h, wg_ref[...].astype(BF16), preferred_element_type=F32)
        u = jnp.dot(h, wu_ref[...].astype(BF16), preferred_element_type=F32)
        a = (g * (1.0 / (1.0 + jnp.exp(-g))) * u).astype(BF16)
        for c0 in range(0, o_ref.shape[1], FFN_DOWN_COLS):
            cols = slice(c0, c0 + FFN_DOWN_COLS)
            d = jnp.dot(a, wd_ref[:, cols].astype(BF16), preferred_element_type=F32)
            if first:
                o_ref[:, cols] = d
            else:
                o_ref[:, cols] += d

    @pl.when(j == 0)
    def _():
        _norm_rows_to(h_ref, x_ref, gpre_ref[...], RMS_EPS)
        swiglu_block(first=True)

    @pl.when(j > 0)
    def _():
        swiglu_block(first=False)

    @pl.when(j == pl.num_programs(1) - 1)
    def _():
        half_gain = 0.5 * gpost_ref[...]

        def body(rows):
            o_ref[rows, :] = x_ref[rows, :] + _rms(o_ref[rows, :], half_gain, RMS_EPS)
        _for_row_chunks(o_ref.shape[0], body)


def _ffn(x, gpre, gpost, wg, wu, wd, layer, half):
    m, d = x.shape
    ff = wg.shape[-1]
    tm, tf = FFN_ROWS, FFN_COLS
    assert m % tm == 0 and ff % tf == 0
    return pl.pallas_call(
        _ffn_kernel,
        grid=(m // tm, ff // tf),
        in_specs=[
            pl.BlockSpec((tm, d), lambda i, j: (i, 0)),
            pl.BlockSpec((1, d), lambda i, j: (0, 0)),
            pl.BlockSpec((1, d), lambda i, j: (0, 0)),
            pl.BlockSpec((None, None, d, tf), lambda i, j: (layer, half, 0, j)),
            pl.BlockSpec((None, None, d, tf), lambda i, j: (layer, half, 0, j)),
            pl.BlockSpec((None, None, tf, d), lambda i, j: (layer, half, j, 0)),
        ],
        out_specs=pl.BlockSpec((tm, d), lambda i, j: (i, 0)),
        out_shape=jax.ShapeDtypeStruct((m, d), F32),
        scratch_shapes=[pltpu.VMEM((tm, d), BF16)],
        compiler_params=_params("parallel", "arbitrary"),
        name="ffn",
    )(x, gpre, gpost, wg, wu, wd)


def _pool_kernel(x_ref, halo_ref, gpre_ref, gpost_ref, w_ref, scale_ref, o_ref, *sum_refs):
    i = pl.program_id(1)
    ts = x_ref.shape[1]
    d = x_ref.shape[2]
    gdim = w_ref.shape[1]
    rows_all = POOL_HALO + ts
    hx_ref = sum_refs[0]
    gpre = gpre_ref[...]

    hh = _rms(halo_ref[0], gpre, RMS_EPS)
    hx_ref[0:POOL_HALO, :] = jnp.where(i > 0, hh, 0.0)
    for r0 in range(0, ts, NORM_ROW_CHUNK):
        hx_ref[POOL_HALO + r0:POOL_HALO + r0 + NORM_ROW_CHUNK, :] = _rms(
            x_ref[0, r0:r0 + NORM_ROW_CHUNK, :], gpre, RMS_EPS)

    for s in range(1, len(sum_refs)):
        shift = 2 ** (s - 1)
        r0 = 8 * s
        c0 = (s - 1) * gdim
        prev = sum_refs[s - 1]
        sum_refs[s][r0:rows_all, c0:d] = (prev[r0:rows_all, c0:d]
                                          + prev[r0 - shift:rows_all - shift, c0:d])

    t = i * ts + lax.broadcasted_iota(jnp.int32, (ts, 1), 0)
    main = slice(POOL_HALO, rows_all)
    for gi, win in enumerate(POOL_WINDOWS):
        cols = slice(gi * gdim, (gi + 1) * gdim)
        stage = win.bit_length() - 1
        if stage < len(sum_refs):
            total = sum_refs[stage][main, cols]
        else:
            prev = sum_refs[stage - 1]
            half = win // 2
            total = prev[main, cols] + prev[POOL_HALO - half:rows_all - half, cols]
        inv_cnt = 1.0 / jnp.minimum(t + 1, win).astype(F32)
        dg = total * inv_cnt - hx_ref[main, cols]
        o_ref[0, :, cols] = (jnp.dot(dg.astype(BF16), w_ref[gi], preferred_element_type=F32)
                             * scale_ref[:, cols])

    gpost = gpost_ref[...]
    for r0 in range(0, ts, NORM_ROW_CHUNK):
        rows = slice(r0, r0 + NORM_ROW_CHUNK)
        o_ref[0, rows, :] = x_ref[0, rows, :] + _rms(o_ref[0, rows, :], gpost, RMS_EPS)


def _pool_mixer(x, gpre, gpost, w, scale):
    b, s, d = x.shape
    ts = POOL_ROWS
    assert s % ts == 0 and ts % POOL_HALO == 0
    halo_blocks = ts // POOL_HALO
    return pl.pallas_call(
        _pool_kernel,
        grid=(b, s // ts),
        in_specs=[
            pl.BlockSpec((1, ts, d), lambda bi, i: (bi, i, 0)),
            pl.BlockSpec((1, POOL_HALO, d),
                         lambda bi, i: (bi, jnp.maximum(i * halo_blocks - 1, 0), 0)),
            pl.BlockSpec((1, d), lambda bi, i: (0, 0)),
            pl.BlockSpec((1, d), lambda bi, i: (0, 0)),
            pl.BlockSpec(w.shape, lambda bi, i: (0, 0, 0)),
            pl.BlockSpec((1, d), lambda bi, i: (0, 0)),
        ],
        out_specs=pl.BlockSpec((1, ts, d), lambda bi, i: (bi, i, 0)),
        out_shape=jax.ShapeDtypeStruct((b, s, d), F32),
        scratch_shapes=[pltpu.VMEM((POOL_HALO + ts, d), F32)] * POOL_STAGES,
        compiler_params=_params("parallel", "parallel"),
        name="pool_mixer",
    )(x, x, gpre, gpost, w, scale)


def _norm_proj_kernel(n_out, out_scale, x_ref, g_ref, *refs):
    w_refs, o_refs, h_ref = refs[:n_out], refs[n_out:2 * n_out], refs[2 * n_out]

    @pl.when(pl.program_id(1) == 0)
    def _():
        _norm_rows_to(h_ref, x_ref, g_ref[...], RMS_EPS)

    h = h_ref[...]
    for w_ref, o_ref in zip(w_refs, o_refs):
        y = jnp.dot(h, w_ref[...].astype(BF16), preferred_element_type=F32)
        if out_scale != 1.0:
            y = y * out_scale
        o_ref[...] = y.astype(o_ref.dtype)


def _norm_proj(x, gain, weights, out_scale=1.0):
    m, d = x.shape
    n = weights[0].shape[1]
    tm, tn = NORM_PROJ_ROWS, PROJ_COLS
    assert m % tm == 0 and n % tn == 0
    n_out = len(weights)
    outs = pl.pallas_call(
        functools.partial(_norm_proj_kernel, n_out, out_scale),
        grid=(m // tm, n // tn),
        in_specs=[pl.BlockSpec((tm, d), lambda i, j: (i, 0)),
                  pl.BlockSpec((1, d), lambda i, j: (0, 0))]
                 + [pl.BlockSpec((d, tn), lambda i, j: (0, j))] * n_out,
        out_specs=[pl.BlockSpec((tm, tn), lambda i, j: (i, j))] * n_out,
        out_shape=[jax.ShapeDtypeStruct((m, n), BF16)] * n_out,
        scratch_shapes=[pltpu.VMEM((tm, d), BF16)],
        compiler_params=_params("parallel", "arbitrary"),
        name="norm_proj",
    )(x, gain, *weights)
    return outs


def _rel_bucket(rel):
    half = NUM_BUCKETS // 2
    max_exact = half // 2
    ret = jnp.where(rel > 0, half, 0)
    n = jnp.abs(rel)
    nf = jnp.maximum(n, 1).astype(F32)
    large = max_exact + (jnp.log(nf / max_exact) / math.log(MAX_DISTANCE / max_exact)
                         * (half - max_exact)).astype(jnp.int32)
    large = jnp.minimum(large, half - 1)
    return ret + jnp.where(n < max_exact, n, large)


FAR_BUCKET = NUM_BUCKETS // 2 - 1
FAR_DISTANCE = MAX_DISTANCE


def _bias_tile_constants(t):
    assert t + 1 >= FAR_DISTANCE and t % CHUNK == 0
    q = np.arange(t, dtype=np.int32)[:, None]
    k = np.arange(t, dtype=np.int32)[None, :]
    buckets = jnp.stack([_rel_bucket(jnp.asarray(k - q)), _rel_bucket(jnp.asarray(k - t - q))])
    allowed = np.stack([(k // CHUNK) <= (q // CHUNK), np.ones((t, t), bool)])
    mask = np.where(allowed, 0.0, -np.inf).astype(np.float32)
    return buckets, jnp.asarray(mask)


def _bias_kernel(table_ref, bucket_ref, mask_ref, o_ref):
    t = bucket_ref.shape[-1]
    table = jnp.broadcast_to(table_ref[0], (t, LANES))
    for tile in range(bucket_ref.shape[0]):
        for c0 in range(0, t, LANES):
            cols = slice(c0, c0 + LANES)
            looked_up = jnp.take_along_axis(table, bucket_ref[tile, :, cols], axis=1)
            o_ref[0, tile, :, cols] = looked_up + mask_ref[tile, :, cols]


def _bias_tiles(rel_bias, t):
    buckets, mask = _bias_tile_constants(t)
    n_buckets, n_maps = rel_bias.shape
    assert n_buckets <= LANES and t % LANES == 0
    table = jnp.pad(rel_bias.T, ((0, 0), (0, LANES - n_buckets))).reshape(n_maps, 1, LANES)
    return pl.pallas_call(
        _bias_kernel,
        grid=(n_maps,),
        in_specs=[
            pl.BlockSpec((1, 1, LANES), lambda m: (m, 0, 0)),
            pl.BlockSpec((2, t, t), lambda m: (0, 0, 0)),
            pl.BlockSpec((2, t, t), lambda m: (0, 0, 0)),
        ],
        out_specs=pl.BlockSpec((1, 2, t, t), lambda m: (m, 0, 0, 0)),
        out_shape=jax.ShapeDtypeStruct((n_maps, 2, t, t), F32),
        compiler_params=_params("parallel"),
        name="bias_tiles",
    )(table, buckets, mask)


def _softmax_pv(qi, t, q_ref, k_ref, v_ref, cols, far_bias, bias_ref):
    qc = q_ref[0, qi * t:(qi + 1) * t, cols]
    near_bias, diag_bias = bias_ref.at[1], bias_ref.at[0]
    pieces = []
    if qi >= 2:
        pieces.append((0, (qi - 1) * t, None))
    if qi >= 1:
        pieces.append(((qi - 1) * t, t, near_bias))
    pieces.append((qi * t, t, diag_bias))

    scores, row_max = [], []
    for start, size, bias in pieces:
        kc = k_ref[0, start:start + size, cols]
        s = lax.dot_general(qc, kc, (((1,), (1,)), ((), ())), preferred_element_type=F32)
        if bias is None:
            row_max.append(jnp.max(s, axis=-1, keepdims=True) + far_bias)
        else:
            s = s + bias[...]
            row_max.append(jnp.max(s, axis=-1, keepdims=True))
        scores.append(s)
    m = functools.reduce(jnp.maximum, row_max)
    yield
    probs = [jnp.exp(s - ((m - far_bias) if bias is None else m))
             for s, (_, _, bias) in zip(scores, pieces)]
    denom = sum(jnp.sum(p, axis=-1, keepdims=True) for p in probs)
    yield
    acc = sum(jnp.dot(p.astype(BF16), v_ref[0, start:start + size, :], preferred_element_type=F32)
              for p, (start, size, _) in zip(probs, pieces))
    return acc * (1.0 / denom)


def _run_staggered(tasks, n_phases, on_done):
    for step in range(len(tasks) + STAGGER * (n_phases - 1)):
        for phase in range(n_phases):
            n = step - STAGGER * phase
            if 0 <= n < len(tasks):
                try:
                    next(tasks[n])
                except StopIteration as done:
                    on_done(n, done.value)


def _attn_kernel(lambda_init, q_ref, k_ref, v_ref, bias_ref, table_ref, lam_ref, subg_ref, o_ref):
    head = pl.program_id(1)
    t = bias_ref.shape[-1]

    lam = lam_ref[...]
    lam_full = (jnp.exp(jnp.sum(lam[0:1] * lam[1:2], keepdims=True))
                - jnp.exp(jnp.sum(lam[2:3] * lam[3:4], keepdims=True)) + lambda_init)

    tiles = list(reversed(range(q_ref.shape[1] // t)))
    tasks = []
    for i in tiles:
        for c in range(2):
            cols = slice(c * HEAD_DIM, (c + 1) * HEAD_DIM)
            tasks.append(_softmax_pv(i, t, q_ref, k_ref, v_ref, cols,
                                     table_ref[FAR_BUCKET, 2 * head + c], bias_ref.at[c]))

    first_map = {}

    def on_done(n, value):
        if n % 2 == 0:
            first_map[n // 2] = value
            return
        i = tiles[n // 2]
        o = first_map.pop(n // 2) - lam_full * value
        o = o * lax.rsqrt(jnp.mean(o * o, axis=-1, keepdims=True) + SUBLN_EPS)
        o_ref[0, i * t:(i + 1) * t, :] = (o * subg_ref[...] * (1.0 - lambda_init)).astype(o_ref.dtype)

    _run_staggered(tasks, 3, on_done)


def _attention(q, k, v, bias_tiles, rel_bias, lam, subln_g, lambda_init):
    b, s, width = q.shape
    t = ATT_TILE
    n_heads = width // V_HEAD_DIM
    assert s % t == 0
    return pl.pallas_call(
        functools.partial(_attn_kernel, lambda_init),
        grid=(b, n_heads),
        in_specs=[
            pl.BlockSpec((1, s, V_HEAD_DIM), lambda bi, h: (bi, 0, h)),
            pl.BlockSpec((1, s, V_HEAD_DIM), lambda bi, h: (bi, 0, h)),
            pl.BlockSpec((1, s, V_HEAD_DIM), lambda bi, h: (bi, 0, h)),
            pl.BlockSpec((2, 2, t, t), lambda bi, h: (h, 0, 0, 0)),
            pl.BlockSpec(memory_space=pltpu.SMEM),
            pl.BlockSpec(lam.shape, lambda bi, h: (0, 0)),
            pl.BlockSpec((1, V_HEAD_DIM), lambda bi, h: (0, 0)),
        ],
        out_specs=pl.BlockSpec((1, s, V_HEAD_DIM), lambda bi, h: (bi, 0, h)),
        out_shape=jax.ShapeDtypeStruct((b, s, width), BF16),
        compiler_params=_params("parallel", "parallel"),
        name="diff_attention",
    )(q, k, v, bias_tiles, rel_bias, lam, subln_g)


def _out_proj_kernel(o_ref, w_ref, x_ref, g_ref, y_ref):
    y_ref[...] = jnp.dot(o_ref[...], w_ref[...], preferred_element_type=F32)
    gain = g_ref[...]

    def body(rows):
        y_ref[rows, :] = x_ref[rows, :] + _rms(y_ref[rows, :], gain, RMS_EPS)
    _for_row_chunks(y_ref.shape[0], body)


def _out_proj(o, w, x, gain):
    m, d = x.shape
    kdim = o.shape[1]
    tm = PROJ_ROWS
    assert m % tm == 0
    return pl.pallas_call(
        _out_proj_kernel,
        grid=(m // tm,),
        in_specs=[
            pl.BlockSpec((tm, kdim), lambda i: (i, 0)),
            pl.BlockSpec((kdim, d), lambda i: (0, 0)),
            pl.BlockSpec((tm, d), lambda i: (i, 0)),
            pl.BlockSpec((1, d), lambda i: (0, 0)),
        ],
        out_specs=pl.BlockSpec((tm, d), lambda i: (i, 0)),
        out_shape=jax.ShapeDtypeStruct((m, d), F32),
        compiler_params=_params("parallel"),
        name="out_proj",
    )(o, w, x, gain)


def kernel(x, norm_gains, ffn_w_gate, ffn_w_up, ffn_w_down, pool_w, pool_scale, kv_norm, w_k, w_v,
           rel_bias, w_q, w_o, lambdas, subln_gain):
    b, s, d = x.shape
    depth = norm_gains.shape[0]
    n_a = pool_w.shape[0]
    gains = norm_gains.reshape(depth, 3, 2, 1, d)
    wg, wu, wd = ffn_w_gate, ffn_w_up, ffn_w_down

    xf = x.reshape(b * s, d)
    k = v = bias_tiles = None
    for l in range(depth):
        g = gains[l]
        if l == n_a:
            k, v = _norm_proj(xf, kv_norm.reshape(1, d), [w_k, w_v])
            bias_tiles = _bias_tiles(rel_bias, ATT_TILE)
        xf = _ffn(xf, g[0, 0], g[0, 1], wg, wu, wd, l, 0)
        if l < n_a:
            xf = _pool_mixer(xf.reshape(b, s, d), g[1, 0], g[1, 1], pool_w[l].astype(BF16),
                             pool_scale[l].reshape(1, d)).reshape(b * s, d)
        else:
            j = l - n_a
            lambda_init = 0.8 - 0.6 * math.exp(-0.3 * l)
            (q,) = _norm_proj(xf, g[1, 0], [w_q[j]], out_scale=HEAD_DIM ** -0.5)
            o = _attention(q.reshape(b, s, -1), k.reshape(b, s, -1), v.reshape(b, s, -1), bias_tiles,
                           rel_bias, lambdas[j], subln_gain[j].reshape(1, -1), lambda_init)
            xf = _out_proj(o.reshape(b * s, -1), w_o[j].astype(BF16), xf, g[1, 1])
        xf = _ffn(xf, g[2, 0], g[2, 1], wg, wu, wd, l, 1)
    return xf.reshape(b, s, d)
```

```python
import functools
import math

import numpy as np
import jax
import jax.numpy as jnp
from jax import lax
from jax.experimental import pallas as pl
from jax.experimental.pallas import tpu as pltpu

F32 = jnp.float32
BF16 = jnp.bfloat16

CHUNK = 64
POOL_WINDOWS = (2, 4, 8, 16)
HEAD_DIM = 128
V_HEAD_DIM = 2 * HEAD_DIM
NUM_BUCKETS = 32
MAX_DISTANCE = 128
RMS_EPS = 1e-6
SUBLN_EPS = 1e-5

LANES = 128

VMEM_LIMIT_BYTES = 60 * 1024 * 1024
FFN_ROWS = 1024
FFN_COLS = 256
FFN_DOWN_COLS = 512
FFN_X_PREFETCH_STEP = 6
NORM_PROJ_ROWS = 1024
PROJ_ROWS = 512
PROJ_COLS = 512
NORM_ROW_CHUNK = 16
POOL_ROWS = 256
POOL_STAGES = 4
POOL_HALO = 8 * POOL_STAGES
ATT_TILE = 256
STAGGER = 1


def _rms(x, gain, eps):
    ms = jnp.mean(x * x, axis=-1, keepdims=True)
    return x * lax.rsqrt(ms + eps) * gain


def _for_row_chunks(n_rows, body):
    for r0 in range(0, n_rows, NORM_ROW_CHUNK):
        body(slice(r0, r0 + NORM_ROW_CHUNK))


def _norm_rows_to(dst_ref, src_ref, gain, eps):
    def body(rows):
        dst_ref[rows, :] = _rms(src_ref[rows, :], gain, eps).astype(dst_ref.dtype)
    _for_row_chunks(src_ref.shape[0], body)


def _params(*semantics):
    return pltpu.CompilerParams(dimension_semantics=semantics,
                                vmem_limit_bytes=VMEM_LIMIT_BYTES)


def _ffn_kernel(x_hbm, gpre_ref, gpost_ref, wg_ref, wu_ref, wd_ref, o_ref, h_ref, x_buf, x_sem):
    i = pl.program_id(0)
    j = pl.program_id(1)
    tm = o_ref.shape[0]
    slot = i % 2
    x_ref = x_buf.at[slot]

    def x_copy(tile, to_slot):
        rows = pl.ds(pl.multiple_of(tile * tm, tm), tm)
        return pltpu.make_async_copy(x_hbm.at[rows, :], x_buf.at[to_slot], x_sem.at[to_slot])

    def swiglu_block(first):
        h = h_ref[...]
        g = jnp.dot(h, wg_ref[...].astype(BF16), preferred_element_type=F32)
        u = jnp.dot(h, wu_ref[...].astype(BF16), preferred_element_type=F32)
        a = (g * (1.0 / (1.0 + jnp.exp(-g))) * u).astype(BF16)
        for c0 in range(0, o_ref.shape[1], FFN_DOWN_COLS):
            cols = slice(c0, c0 + FFN_DOWN_COLS)
            d = jnp.dot(a, wd_ref[:, cols].astype(BF16), preferred_element_type=F32)
            if first:
                o_ref[:, cols] = d
            else:
                o_ref[:, cols] += d

    @pl.when(jnp.logical_and(i == 0, j == 0))
    def _():
        x_copy(0, 0).start()

    @pl.when(j == 0)
    def _():
        x_copy(i, slot).wait()
        _norm_rows_to(h_ref, x_ref, gpre_ref[...], RMS_EPS)
        swiglu_block(first=True)

    @pl.when(jnp.logical_and(j == FFN_X_PREFETCH_STEP, i + 1 < pl.num_programs(0)))
    def _():
        x_copy(i + 1, 1 - slot).start()

    @pl.when(j > 0)
    def _():
        swiglu_block(first=False)

    @pl.when(j == pl.num_programs(1) - 1)
    def _():
        half_gain = 0.5 * gpost_ref[...]

        def body(rows):
            o_ref[rows, :] = x_ref[rows, :] + _rms(o_ref[rows, :], half_gain, RMS_EPS)
        _for_row_chunks(o_ref.shape[0], body)


def _ffn(x, gpre, gpost, wg, wu, wd, layer, half):
    m, d = x.shape
    ff = wg.shape[-1]
    tm, tf = FFN_ROWS, FFN_COLS
    assert m % tm == 0 and ff % tf == 0
    return pl.pallas_call(
        _ffn_kernel,
        grid=(m // tm, ff // tf),
        in_specs=[
            pl.BlockSpec(memory_space=pl.ANY),
            pl.BlockSpec((1, d), lambda i, j: (0, 0)),
            pl.BlockSpec((1, d), lambda i, j: (0, 0)),
            pl.BlockSpec((None, None, d, tf), lambda i, j: (layer, half, 0, j)),
            pl.BlockSpec((None, None, d, tf), lambda i, j: (layer, half, 0, j)),
            pl.BlockSpec((None, None, tf, d), lambda i, j: (layer, half, j, 0)),
        ],
        out_specs=pl.BlockSpec((tm, d), lambda i, j: (i, 0)),
        out_shape=jax.ShapeDtypeStruct((m, d), F32),
        scratch_shapes=[pltpu.VMEM((tm, d), BF16), pltpu.VMEM((2, tm, d), F32),
                        pltpu.SemaphoreType.DMA((2,))],
        compiler_params=_params("arbitrary", "arbitrary"),
        name="ffn",
    )(x, gpre, gpost, wg, wu, wd)


def _pool_kernel(x_ref, halo_ref, gpre_ref, gpost_ref, w_ref, scale_ref, o_ref, *sum_refs):
    i = pl.program_id(1)
    ts = x_ref.shape[1]
    d = x_ref.shape[2]
    gdim = w_ref.shape[1]
    rows_all = POOL_HALO + ts
    hx_ref = sum_refs[0]
    gpre = gpre_ref[...]

    hh = _rms(halo_ref[0], gpre, RMS_EPS)
    hx_ref[0:POOL_HALO, :] = jnp.where(i > 0, hh, 0.0)
    for r0 in range(0, ts, NORM_ROW_CHUNK):
        hx_ref[POOL_HALO + r0:POOL_HALO + r0 + NORM_ROW_CHUNK, :] = _rms(
            x_ref[0, r0:r0 + NORM_ROW_CHUNK, :], gpre, RMS_EPS)

    for s in range(1, len(sum_refs)):
        shift = 2 ** (s - 1)
        r0 = 8 * s
        c0 = (s - 1) * gdim
        prev = sum_refs[s - 1]
        sum_refs[s][r0:rows_all, c0:d] = (prev[r0:rows_all, c0:d]
                                          + prev[r0 - shift:rows_all - shift, c0:d])

    t = i * ts + lax.broadcasted_iota(jnp.int32, (ts, 1), 0)
    main = slice(POOL_HALO, rows_all)
    for gi, win in enumerate(POOL_WINDOWS):
        cols = slice(gi * gdim, (gi + 1) * gdim)
        stage = win.bit_length() - 1
        if stage < len(sum_refs):
            total = sum_refs[stage][main, cols]
        else:
            prev = sum_refs[stage - 1]
            half = win // 2
            total = prev[main, cols] + prev[POOL_HALO - half:rows_all - half, cols]
        inv_cnt = 1.0 / jnp.minimum(t + 1, win).astype(F32)
        dg = total * inv_cnt - hx_ref[main, cols]
        o_ref[0, :, cols] = (jnp.dot(dg.astype(BF16), w_ref[gi], preferred_element_type=F32)
                             * scale_ref[:, cols])

    gpost = gpost_ref[...]
    for r0 in range(0, ts, NORM_ROW_CHUNK):
        rows = slice(r0, r0 + NORM_ROW_CHUNK)
        o_ref[0, rows, :] = x_ref[0, rows, :] + _rms(o_ref[0, rows, :], gpost, RMS_EPS)


def _pool_mixer(x, gpre, gpost, w, scale):
    b, s, d = x.shape
    ts = POOL_ROWS
    assert s % ts == 0 and ts % POOL_HALO == 0
    halo_blocks = ts // POOL_HALO
    return pl.pallas_call(
        _pool_kernel,
        grid=(b, s // ts),
        in_specs=[
            pl.BlockSpec((1, ts, d), lambda bi, i: (bi, i, 0)),
            pl.BlockSpec((1, POOL_HALO, d),
                         lambda bi, i: (bi, jnp.maximum(i * halo_blocks - 1, 0), 0)),
            pl.BlockSpec((1, d), lambda bi, i: (0, 0)),
            pl.BlockSpec((1, d), lambda bi, i: (0, 0)),
            pl.BlockSpec(w.shape, lambda bi, i: (0, 0, 0)),
            pl.BlockSpec((1, d), lambda bi, i: (0, 0)),
        ],
        out_specs=pl.BlockSpec((1, ts, d), lambda bi, i: (bi, i, 0)),
        out_shape=jax.ShapeDtypeStruct((b, s, d), F32),
        scratch_shapes=[pltpu.VMEM((POOL_HALO + ts, d), F32)] * POOL_STAGES,
        compiler_params=_params("parallel", "parallel"),
        name="pool_mixer",
    )(x, x, gpre, gpost, w, scale)


def _norm_proj_kernel(n_out, out_scale, x_ref, g_ref, *refs):
    w_refs, o_refs, h_ref = refs[:n_out], refs[n_out:2 * n_out], refs[2 * n_out]

    @pl.when(pl.program_id(1) == 0)
    def _():
        _norm_rows_to(h_ref, x_ref, g_ref[...], RMS_EPS)

    h = h_ref[...]
    for w_ref, o_ref in zip(w_refs, o_refs):
        y = jnp.dot(h, w_ref[...].astype(BF16), preferred_element_type=F32)
        if out_scale != 1.0:
            y = y * out_scale
        o_ref[...] = y.astype(o_ref.dtype)


def _norm_proj(x, gain, weights, out_scale=1.0):
    m, d = x.shape
    n = weights[0].shape[1]
    tm, tn = NORM_PROJ_ROWS, PROJ_COLS
    assert m % tm == 0 and n % tn == 0
    n_out = len(weights)
    outs = pl.pallas_call(
        functools.partial(_norm_proj_kernel, n_out, out_scale),
        grid=(m // tm, n // tn),
        in_specs=[pl.BlockSpec((tm, d), lambda i, j: (i, 0)),
                  pl.BlockSpec((1, d), lambda i, j: (0, 0))]
                 + [pl.BlockSpec((d, tn), lambda i, j: (0, j))] * n_out,
        out_specs=[pl.BlockSpec((tm, tn), lambda i, j: (i, j))] * n_out,
        out_shape=[jax.ShapeDtypeStruct((m, n), BF16)] * n_out,
        scratch_shapes=[pltpu.VMEM((tm, d), BF16)],
        compiler_params=_params("parallel", "arbitrary"),
        name="norm_proj",
    )(x, gain, *weights)
    return outs


def _rel_bucket(rel):
    half = NUM_BUCKETS // 2
    max_exact = half // 2
    ret = jnp.where(rel > 0, half, 0)
    n = jnp.abs(rel)
    nf = jnp.maximum(n, 1).astype(F32)
    large = max_exact + (jnp.log(nf / max_exact) / math.log(MAX_DISTANCE / max_exact)
                         * (half - max_exact)).astype(jnp.int32)
    large = jnp.minimum(large, half - 1)
    return ret + jnp.where(n < max_exact, n, large)


FAR_BUCKET = NUM_BUCKETS // 2 - 1
FAR_DISTANCE = MAX_DISTANCE


def _bias_tile_constants(t):
    assert t + 1 >= FAR_DISTANCE and t % CHUNK == 0
    q = np.arange(t, dtype=np.int32)[:, None]
    k = np.arange(t, dtype=np.int32)[None, :]
    buckets = jnp.stack([_rel_bucket(jnp.asarray(k - q)), _rel_bucket(jnp.asarray(k - t - q))])
    allowed = np.stack([(k // CHUNK) <= (q // CHUNK), np.ones((t, t), bool)])
    mask = np.where(allowed, 0.0, -np.inf).astype(np.float32)
    return buckets, jnp.asarray(mask)


def _bias_kernel(table_ref, bucket_ref, mask_ref, o_ref):
    t = bucket_ref.shape[-1]
    table = jnp.broadcast_to(table_ref[0], (t, LANES))
    for tile in range(bucket_ref.shape[0]):
        for c0 in range(0, t, LANES):
            cols = slice(c0, c0 + LANES)
            looked_up = jnp.take_along_axis(table, bucket_ref[tile, :, cols], axis=1)
            o_ref[0, tile, :, cols] = looked_up + mask_ref[tile, :, cols]


def _bias_tiles(rel_bias, t):
    buckets, mask = _bias_tile_constants(t)
    n_buckets, n_maps = rel_bias.shape
    assert n_buckets <= LANES and t % LANES == 0
    table = jnp.pad(rel_bias.T, ((0, 0), (0, LANES - n_buckets))).reshape(n_maps, 1, LANES)
    return pl.pallas_call(
        _bias_kernel,
        grid=(n_maps,),
        in_specs=[
            pl.BlockSpec((1, 1, LANES), lambda m: (m, 0, 0)),
            pl.BlockSpec((2, t, t), lambda m: (0, 0, 0)),
            pl.BlockSpec((2, t, t), lambda m: (0, 0, 0)),
        ],
        out_specs=pl.BlockSpec((1, 2, t, t), lambda m: (m, 0, 0, 0)),
        out_shape=jax.ShapeDtypeStruct((n_maps, 2, t, t), F32),
        compiler_params=_params("parallel"),
        name="bias_tiles",
    )(table, buckets, mask)


def _softmax_pv(qi, t, q_ref, k_ref, v_ref, cols, far_bias, bias_ref):
    qc = q_ref[0, qi * t:(qi + 1) * t, cols]
    near_bias, diag_bias = bias_ref.at[1], bias_ref.at[0]
    pieces = []
    if qi >= 2:
        pieces.append((0, (qi - 1) * t, None))
    if qi >= 1:
        pieces.append(((qi - 1) * t, t, near_bias))
    pieces.append((qi * t, t, diag_bias))

    scores, row_max = [], []
    for start, size, bias in pieces:
        kc = k_ref[0, start:start + size, cols]
        s = lax.dot_general(qc, kc, (((1,), (1,)), ((), ())), preferred_element_type=F32)
        if bias is None:
            row_max.append(jnp.max(s, axis=-1, keepdims=True) + far_bias)
        else:
            s = s + bias[...]
            row_max.append(jnp.max(s, axis=-1, keepdims=True))
        scores.append(s)
    m = functools.reduce(jnp.maximum, row_max)
    yield
    probs = [jnp.exp(s - ((m - far_bias) if bias is None else m))
             for s, (_, _, bias) in zip(scores, pieces)]
    denom = sum(jnp.sum(p, axis=-1, keepdims=True) for p in probs)
    yield
    acc = sum(jnp.dot(p.astype(BF16), v_ref[0, start:start + size, :], preferred_element_type=F32)
              for p, (start, size, _) in zip(probs, pieces))
    return acc * (1.0 / denom)


def _run_staggered(tasks, n_phases, on_done):
    for step in range(len(tasks) + STAGGER * (n_phases - 1)):
        for phase in range(n_phases):
            n = step - STAGGER * phase
            if 0 <= n < len(tasks):
                try:
                    next(tasks[n])
                except StopIteration as done:
                    on_done(n, done.value)


def _attn_kernel(lambda_init, q_ref, k_ref, v_ref, bias_ref, table_ref, lam_ref, subg_ref, o_ref):
    head = pl.program_id(1)
    t = bias_ref.shape[-1]

    lam = lam_ref[...]
    lam_full = (jnp.exp(jnp.sum(lam[0:1] * lam[1:2], keepdims=True))
                - jnp.exp(jnp.sum(lam[2:3] * lam[3:4], keepdims=True)) + lambda_init)

    tiles = list(reversed(range(q_ref.shape[1] // t)))
    tasks = []
    for i in tiles:
        for c in range(2):
            cols = slice(c * HEAD_DIM, (c + 1) * HEAD_DIM)
            tasks.append(_softmax_pv(i, t, q_ref, k_ref, v_ref, cols,
                                     table_ref[FAR_BUCKET, 2 * head + c], bias_ref.at[c]))

    first_map = {}

    def on_done(n, value):
        if n % 2 == 0:
            first_map[n // 2] = value
            return
        i = tiles[n // 2]
        o = first_map.pop(n // 2) - lam_full * value
        o = o * lax.rsqrt(jnp.mean(o * o, axis=-1, keepdims=True) + SUBLN_EPS)
        o_ref[0, i * t:(i + 1) * t, :] = (o * subg_ref[...] * (1.0 - lambda_init)).astype(o_ref.dtype)

    _run_staggered(tasks, 3, on_done)


def _attention(q, k, v, bias_tiles, rel_bias, lam, subln_g, lambda_init):
    b, s, width = q.shape
    t = ATT_TILE
    n_heads = width // V_HEAD_DIM
    assert s % t == 0
    return pl.pallas_call(
        functools.partial(_attn_kernel, lambda_init),
        grid=(b, n_heads),
        in_specs=[
            pl.BlockSpec((1, s, V_HEAD_DIM), lambda bi, h: (bi, 0, h)),
            pl.BlockSpec((1, s, V_HEAD_DIM), lambda bi, h: (bi, 0, h)),
            pl.BlockSpec((1, s, V_HEAD_DIM), lambda bi, h: (bi, 0, h)),
            pl.BlockSpec((2, 2, t, t), lambda bi, h: (h, 0, 0, 0)),
            pl.BlockSpec(memory_space=pltpu.SMEM),
            pl.BlockSpec(lam.shape, lambda bi, h: (0, 0)),
            pl.BlockSpec((1, V_HEAD_DIM), lambda bi, h: (0, 0)),
        ],
        out_specs=pl.BlockSpec((1, s, V_HEAD_DIM), lambda bi, h: (bi, 0, h)),
        out_shape=jax.ShapeDtypeStruct((b, s, width), BF16),
        compiler_params=_params("parallel", "parallel"),
        name="diff_attention",
    )(q, k, v, bias_tiles, rel_bias, lam, subln_g)


def _out_proj_kernel(o_ref, w_ref, x_ref, g_ref, y_ref):
    y_ref[...] = jnp.dot(o_ref[...], w_ref[...], preferred_element_type=F32)
    gain = g_ref[...]

    def body(rows):
        y_ref[rows, :] = x_ref[rows, :] + _rms(y_ref[rows, :], gain, RMS_EPS)
    _for_row_chunks(y_ref.shape[0], body)


def _out_proj(o, w, x, gain):
    m, d = x.shape
    kdim = o.shape[1]
    tm = PROJ_ROWS
    assert m % tm == 0
    return pl.pallas_call(
        _out_proj_kernel,
        grid=(m // tm,),
        in_specs=[
            pl.BlockSpec((tm, kdim), lambda i: (i, 0)),
            pl.BlockSpec((kdim, d), lambda i: (0, 0)),
            pl.BlockSpec((tm, d), lambda i: (i, 0)),
            pl.BlockSpec((1, d), lambda i: (0, 0)),
        ],
        out_specs=pl.BlockSpec((tm, d), lambda i: (i, 0)),
        out_shape=jax.ShapeDtypeStruct((m, d), F32),
        compiler_params=_params("parallel"),
        name="out_proj",
    )(o, w, x, gain)


def kernel(x, norm_gains, ffn_w_gate, ffn_w_up, ffn_w_down, pool_w, pool_scale, kv_norm, w_k, w_v,
           rel_bias, w_q, w_o, lambdas, subln_gain):
    b, s, d = x.shape
    depth = norm_gains.shape[0]
    n_a = pool_w.shape[0]
    gains = norm_gains.reshape(depth, 3, 2, 1, d)
    wg, wu, wd = ffn_w_gate, ffn_w_up, ffn_w_down

    xf = x.reshape(b * s, d)
    k = v = bias_tiles = None
    for l in range(depth):
        g = gains[l]
        if l == n_a:
            k, v = _norm_proj(xf, kv_norm.reshape(1, d), [w_k, w_v])
            bias_tiles = _bias_tiles(rel_bias, ATT_TILE)
        xf = _ffn(xf, g[0, 0], g[0, 1], wg, wu, wd, l, 0)
        if l < n_a:
            xf = _pool_mixer(xf.reshape(b, s, d), g[1, 0], g[1, 1], pool_w[l].astype(BF16),
                             pool_scale[l].reshape(1, d)).reshape(b * s, d)
        else:
            j = l - n_a
            lambda_init = 0.8 - 0.6 * math.exp(-0.3 * l)
            (q,) = _norm_proj(xf, g[1, 0], [w_q[j]], out_scale=HEAD_DIM ** -0.5)
            o = _attention(q.reshape(b, s, -1), k.reshape(b, s, -1), v.reshape(b, s, -1), bias_tiles,
                           rel_bias, lambdas[j], subln_gain[j].reshape(1, -1), lambda_init)
            xf = _out_proj(o.reshape(b * s, -1), w_o[j].astype(BF16), xf, g[1, 1])
        xf = _ffn(xf, g[2, 0], g[2, 1], wg, wu, wd, l, 1)
    return xf.reshape(b, s, d)
```

```python
import functools
import math

import numpy as np
import jax
import jax.numpy as jnp
from jax import lax
from jax.experimental import pallas as pl
from jax.experimental.pallas import tpu as pltpu

F32 = jnp.float32
BF16 = jnp.bfloat16

CHUNK = 64
POOL_WINDOWS = (2, 4, 8, 16)
HEAD_DIM = 128
V_HEAD_DIM = 2 * HEAD_DIM
NUM_BUCKETS = 32
MAX_DISTANCE = 128
RMS_EPS = 1e-6
SUBLN_EPS = 1e-5
LOG2_E = math.log2(math.e)

LANES = 128

VMEM_LIMIT_BYTES = 60 * 1024 * 1024
FFN_ROWS = 1024
FFN_COLS = 256
FFN_DOWN_COLS = 512
FFN_X_PREFETCH_STEP = 6
NORM_PROJ_ROWS = 1024
PROJ_ROWS = 1024
PROJ_COLS_TOTAL = 1024
NORM_ROW_CHUNK = 16
POOL_ROWS = 256
POOL_STAGES = 4
POOL_HALO = 8 * POOL_STAGES
ATT_TILE = 256
STAGGER = 1


def _rms(x, gain, eps):
    ms = jnp.mean(x * x, axis=-1, keepdims=True)
    return x * lax.rsqrt(ms + eps) * gain


def _for_row_chunks(n_rows, body):
    for r0 in range(0, n_rows, NORM_ROW_CHUNK):
        body(slice(r0, r0 + NORM_ROW_CHUNK))


def _norm_rows_to(dst_ref, src_ref, gain, eps):
    def body(rows):
        dst_ref[rows, :] = _rms(src_ref[rows, :], gain, eps).astype(dst_ref.dtype)
    _for_row_chunks(src_ref.shape[0], body)


def _params(*semantics):
    return pltpu.CompilerParams(dimension_semantics=semantics,
                                vmem_limit_bytes=VMEM_LIMIT_BYTES)


def _ffn_kernel(x_hbm, gpre_ref, gpost_ref, wg_ref, wu_ref, wd_ref, o_ref, h_ref, x_buf, x_sem):
    i = pl.program_id(0)
    j = pl.program_id(1)
    tm = o_ref.shape[0]
    slot = i % 2
    x_ref = x_buf.at[slot]

    def x_copy(tile, to_slot):
        rows = pl.ds(pl.multiple_of(tile * tm, tm), tm)
        return pltpu.make_async_copy(x_hbm.at[rows, :], x_buf.at[to_slot], x_sem.at[to_slot])

    def swiglu_block(first):
        h = h_ref[...]
        g = jnp.dot(h, wg_ref[...].astype(BF16), preferred_element_type=F32)
        u = jnp.dot(h, wu_ref[...].astype(BF16), preferred_element_type=F32)
        a = (g * (1.0 / (1.0 + jnp.exp(-g))) * u).astype(BF16)
        for c0 in range(0, o_ref.shape[1], FFN_DOWN_COLS):
            cols = slice(c0, c0 + FFN_DOWN_COLS)
            d = jnp.dot(a, wd_ref[:, cols].astype(BF16), preferred_element_type=F32)
            if first:
                o_ref[:, cols] = d
            else:
                o_ref[:, cols] += d

    @pl.when(jnp.logical_and(i == 0, j == 0))
    def _():
        x_copy(0, 0).start()

    @pl.when(j == 0)
    def _():
        x_copy(i, slot).wait()
        _norm_rows_to(h_ref, x_ref, gpre_ref[...], RMS_EPS)
        swiglu_block(first=True)

    @pl.when(jnp.logical_and(j == FFN_X_PREFETCH_STEP, i + 1 < pl.num_programs(0)))
    def _():
        x_copy(i + 1, 1 - slot).start()

    @pl.when(j > 0)
    def _():
        swiglu_block(first=False)

    @pl.when(j == pl.num_programs(1) - 1)
    def _():
        half_gain = 0.5 * gpost_ref[...]

        def body(rows):
            o_ref[rows, :] = x_ref[rows, :] + _rms(o_ref[rows, :], half_gain, RMS_EPS)
        _for_row_chunks(o_ref.shape[0], body)


def _ffn(x, gpre, gpost, wg, wu, wd, layer, half):
    m, d = x.shape
    ff = wg.shape[-1]
    tm, tf = FFN_ROWS, FFN_COLS
    assert m % tm == 0 and ff % tf == 0
    return pl.pallas_call(
        _ffn_kernel,
        grid=(m // tm, ff // tf),
        in_specs=[
            pl.BlockSpec(memory_space=pl.ANY),
            pl.BlockSpec((1, d), lambda i, j: (0, 0)),
            pl.BlockSpec((1, d), lambda i, j: (0, 0)),
            pl.BlockSpec((None, None, d, tf), lambda i, j: (layer, half, 0, j)),
            pl.BlockSpec((None, None, d, tf), lambda i, j: (layer, half, 0, j)),
            pl.BlockSpec((None, None, tf, d), lambda i, j: (layer, half, j, 0)),
        ],
        out_specs=pl.BlockSpec((tm, d), lambda i, j: (i, 0)),
        out_shape=jax.ShapeDtypeStruct((m, d), F32),
        scratch_shapes=[pltpu.VMEM((tm, d), BF16), pltpu.VMEM((2, tm, d), F32),
                        pltpu.SemaphoreType.DMA((2,))],
        compiler_params=_params("arbitrary", "arbitrary"),
        name="ffn",
    )(x, gpre, gpost, wg, wu, wd)


def _pool_kernel(x_ref, halo_ref, gpre_ref, gpost_ref, w_ref, scale_ref, o_ref, *sum_refs):
    i = pl.program_id(1)
    ts = x_ref.shape[1]
    d = x_ref.shape[2]
    gdim = w_ref.shape[1]
    rows_all = POOL_HALO + ts
    hx_ref = sum_refs[0]
    gpre = gpre_ref[...]

    hh = _rms(halo_ref[0], gpre, RMS_EPS)
    hx_ref[0:POOL_HALO, :] = jnp.where(i > 0, hh, 0.0)
    for r0 in range(0, ts, NORM_ROW_CHUNK):
        hx_ref[POOL_HALO + r0:POOL_HALO + r0 + NORM_ROW_CHUNK, :] = _rms(
            x_ref[0, r0:r0 + NORM_ROW_CHUNK, :], gpre, RMS_EPS)

    for s in range(1, len(sum_refs)):
        shift = 2 ** (s - 1)
        r0 = 8 * s
        c0 = (s - 1) * gdim
        prev = sum_refs[s - 1]
        sum_refs[s][r0:rows_all, c0:d] = (prev[r0:rows_all, c0:d]
                                          + prev[r0 - shift:rows_all - shift, c0:d])

    t = i * ts + lax.broadcasted_iota(jnp.int32, (ts, 1), 0)
    main = slice(POOL_HALO, rows_all)
    for gi, win in enumerate(POOL_WINDOWS):
        cols = slice(gi * gdim, (gi + 1) * gdim)
        stage = win.bit_length() - 1
        if stage < len(sum_refs):
            total = sum_refs[stage][main, cols]
        else:
            prev = sum_refs[stage - 1]
            half = win // 2
            total = prev[main, cols] + prev[POOL_HALO - half:rows_all - half, cols]
        inv_cnt = 1.0 / jnp.minimum(t + 1, win).astype(F32)
        dg = total * inv_cnt - hx_ref[main, cols]
        o_ref[0, :, cols] = (jnp.dot(dg.astype(BF16), w_ref[gi], preferred_element_type=F32)
                             * scale_ref[:, cols])

    gpost = gpost_ref[...]
    for r0 in range(0, ts, NORM_ROW_CHUNK):
        rows = slice(r0, r0 + NORM_ROW_CHUNK)
        o_ref[0, rows, :] = x_ref[0, rows, :] + _rms(o_ref[0, rows, :], gpost, RMS_EPS)


def _pool_mixer(x, gpre, gpost, w, scale):
    b, s, d = x.shape
    ts = POOL_ROWS
    assert s % ts == 0 and ts % POOL_HALO == 0
    halo_blocks = ts // POOL_HALO
    return pl.pallas_call(
        _pool_kernel,
        grid=(b, s // ts),
        in_specs=[
            pl.BlockSpec((1, ts, d), lambda bi, i: (bi, i, 0)),
            pl.BlockSpec((1, POOL_HALO, d),
                         lambda bi, i: (bi, jnp.maximum(i * halo_blocks - 1, 0), 0)),
            pl.BlockSpec((1, d), lambda bi, i: (0, 0)),
            pl.BlockSpec((1, d), lambda bi, i: (0, 0)),
            pl.BlockSpec(w.shape, lambda bi, i: (0, 0, 0)),
            pl.BlockSpec((1, d), lambda bi, i: (0, 0)),
        ],
        out_specs=pl.BlockSpec((1, ts, d), lambda bi, i: (bi, i, 0)),
        out_shape=jax.ShapeDtypeStruct((b, s, d), F32),
        scratch_shapes=[pltpu.VMEM((POOL_HALO + ts, d), F32)] * POOL_STAGES,
        compiler_params=_params("parallel", "parallel"),
        name="pool_mixer",
    )(x, x, gpre, gpost, w, scale)


def _norm_proj_kernel(n_out, out_scale, x_ref, g_ref, *refs):
    w_refs, o_refs, h_ref = refs[:n_out], refs[n_out:2 * n_out], refs[2 * n_out]

    @pl.when(pl.program_id(1) == 0)
    def _():
        _norm_rows_to(h_ref, x_ref, g_ref[...], RMS_EPS)

    h = h_ref[...]
    for w_ref, o_ref in zip(w_refs, o_refs):
        y = jnp.dot(h, w_ref[...].astype(BF16), preferred_element_type=F32)
        if out_scale != 1.0:
            y = y * out_scale
        o_ref[...] = y.astype(o_ref.dtype)


def _norm_proj(x, gain, weights, out_scale=1.0):
    m, d = x.shape
    n = weights[0].shape[1]
    n_out = len(weights)
    tm, tn = NORM_PROJ_ROWS, PROJ_COLS_TOTAL // n_out
    assert m % tm == 0 and n % tn == 0
    outs = pl.pallas_call(
        functools.partial(_norm_proj_kernel, n_out, out_scale),
        grid=(m // tm, n // tn),
        in_specs=[pl.BlockSpec((tm, d), lambda i, j: (i, 0)),
                  pl.BlockSpec((1, d), lambda i, j: (0, 0))]
                 + [pl.BlockSpec((d, tn), lambda i, j: (0, j))] * n_out,
        out_specs=[pl.BlockSpec((tm, tn), lambda i, j: (i, j))] * n_out,
        out_shape=[jax.ShapeDtypeStruct((m, n), BF16)] * n_out,
        scratch_shapes=[pltpu.VMEM((tm, d), BF16)],
        compiler_params=_params("parallel", "arbitrary"),
        name="norm_proj",
    )(x, gain, *weights)
    return outs


def _rel_bucket(rel):
    half = NUM_BUCKETS // 2
    max_exact = half // 2
    ret = jnp.where(rel > 0, half, 0)
    n = jnp.abs(rel)
    nf = jnp.maximum(n, 1).astype(F32)
    large = max_exact + (jnp.log(nf / max_exact) / math.log(MAX_DISTANCE / max_exact)
                         * (half - max_exact)).astype(jnp.int32)
    large = jnp.minimum(large, half - 1)
    return ret + jnp.where(n < max_exact, n, large)


FAR_BUCKET = NUM_BUCKETS // 2 - 1
FAR_DISTANCE = MAX_DISTANCE


def _bias_tile_constants(t):
    assert t + 1 >= FAR_DISTANCE and t % CHUNK == 0
    q = np.arange(t, dtype=np.int32)[:, None]
    k = np.arange(t, dtype=np.int32)[None, :]
    buckets = jnp.stack([_rel_bucket(jnp.asarray(k - q)), _rel_bucket(jnp.asarray(k - t - q))])
    allowed = np.stack([(k // CHUNK) <= (q // CHUNK), np.ones((t, t), bool)])
    mask = np.where(allowed, 0.0, -np.inf).astype(np.float32)
    return buckets, jnp.asarray(mask)


def _bias_kernel(table_ref, bucket_ref, mask_ref, o_ref):
    t = bucket_ref.shape[-1]
    table = jnp.broadcast_to(table_ref[0], (t, LANES))
    for tile in range(bucket_ref.shape[0]):
        for c0 in range(0, t, LANES):
            cols = slice(c0, c0 + LANES)
            looked_up = jnp.take_along_axis(table, bucket_ref[tile, :, cols], axis=1)
            o_ref[0, tile, :, cols] = looked_up * LOG2_E + mask_ref[tile, :, cols]


def _bias_tiles(rel_bias, t):
    buckets, mask = _bias_tile_constants(t)
    n_buckets, n_maps = rel_bias.shape
    assert n_buckets <= LANES and t % LANES == 0
    table = jnp.pad(rel_bias.T, ((0, 0), (0, LANES - n_buckets))).reshape(n_maps, 1, LANES)
    return pl.pallas_call(
        _bias_kernel,
        grid=(n_maps,),
        in_specs=[
            pl.BlockSpec((1, 1, LANES), lambda m: (m, 0, 0)),
            pl.BlockSpec((2, t, t), lambda m: (0, 0, 0)),
            pl.BlockSpec((2, t, t), lambda m: (0, 0, 0)),
        ],
        out_specs=pl.BlockSpec((1, 2, t, t), lambda m: (m, 0, 0, 0)),
        out_shape=jax.ShapeDtypeStruct((n_maps, 2, t, t), F32),
        compiler_params=_params("parallel"),
        name="bias_tiles",
    )(table, buckets, mask)


def _softmax_pv(qi, t, q_ref, k_ref, v_ref, cols, far_bias, bias_ref):
    qc = q_ref[0, qi * t:(qi + 1) * t, cols]
    near_bias, diag_bias = bias_ref.at[1], bias_ref.at[0]
    pieces = []
    if qi >= 2:
        pieces.append((0, (qi - 1) * t, None))
    if qi >= 1:
        pieces.append(((qi - 1) * t, t, near_bias))
    pieces.append((qi * t, t, diag_bias))

    scores, row_max = [], []
    for start, size, bias in pieces:
        kc = k_ref[0, start:start + size, cols]
        s = lax.dot_general(qc, kc, (((1,), (1,)), ((), ())), preferred_element_type=F32)
        if bias is None:
            row_max.append(jnp.max(s, axis=-1, keepdims=True) + far_bias)
        else:
            s = s + bias[...]
            row_max.append(jnp.max(s, axis=-1, keepdims=True))
        scores.append(s)
    m = functools.reduce(jnp.maximum, row_max)
    yield
    probs = []
    for s, (_, _, bias) in zip(scores, pieces):
        probs.append(jnp.exp2(s - ((m - far_bias) if bias is None else m)))
        if bias is None:
            yield
    denom = sum(jnp.sum(p, axis=-1, keepdims=True) for p in probs)
    yield
    acc = sum(jnp.dot(p.astype(BF16), v_ref[0, start:start + size, :], preferred_element_type=F32)
              for p, (start, size, _) in zip(probs, pieces))
    return acc * (1.0 / denom)


def _run_staggered(tasks, n_phases, on_done):
    finished = set()
    for step in range(len(tasks) + STAGGER * (n_phases - 1)):
        for phase in range(n_phases):
            n = step - STAGGER * phase
            if 0 <= n < len(tasks) and n not in finished:
                try:
                    next(tasks[n])
                except StopIteration as done:
                    finished.add(n)
                    on_done(n, done.value)


def _attn_kernel(lambda_init, q_ref, k_ref, v_ref, bias_ref, table_ref, lam_ref, subg_ref, o_ref):
    head = pl.program_id(1)
    t = bias_ref.shape[-1]

    lam = lam_ref[...]
    lam_full = (jnp.exp(jnp.sum(lam[0:1] * lam[1:2], keepdims=True))
                - jnp.exp(jnp.sum(lam[2:3] * lam[3:4], keepdims=True)) + lambda_init)

    tiles = list(reversed(range(q_ref.shape[1] // t)))
    tasks = []
    for i in tiles:
        for c in range(2):
            cols = slice(c * HEAD_DIM, (c + 1) * HEAD_DIM)
            tasks.append(_softmax_pv(i, t, q_ref, k_ref, v_ref, cols,
                                     table_ref[FAR_BUCKET, 2 * head + c] * LOG2_E, bias_ref.at[c]))

    first_map = {}

    def on_done(n, value):
        if n % 2 == 0:
            first_map[n // 2] = value
            return
        i = tiles[n // 2]
        o = first_map.pop(n // 2) - lam_full * value
        o = o * lax.rsqrt(jnp.mean(o * o, axis=-1, keepdims=True) + SUBLN_EPS)
        o_ref[0, i * t:(i + 1) * t, :] = (o * subg_ref[...] * (1.0 - lambda_init)).astype(o_ref.dtype)

    _run_staggered(tasks, 4, on_done)


def _attention(q, k, v, bias_tiles, rel_bias, lam, subln_g, lambda_init):
    b, s, width = q.shape
    t = ATT_TILE
    n_heads = width // V_HEAD_DIM
    assert s % t == 0
    return pl.pallas_call(
        functools.partial(_attn_kernel, lambda_init),
        grid=(b, n_heads),
        in_specs=[
            pl.BlockSpec((1, s, V_HEAD_DIM), lambda bi, h: (bi, 0, h)),
            pl.BlockSpec((1, s, V_HEAD_DIM), lambda bi, h: (bi, 0, h)),
            pl.BlockSpec((1, s, V_HEAD_DIM), lambda bi, h: (bi, 0, h)),
            pl.BlockSpec((2, 2, t, t), lambda bi, h: (h, 0, 0, 0)),
            pl.BlockSpec(memory_space=pltpu.SMEM),
            pl.BlockSpec(lam.shape, lambda bi, h: (0, 0)),
            pl.BlockSpec((1, V_HEAD_DIM), lambda bi, h: (0, 0)),
        ],
        out_specs=pl.BlockSpec((1, s, V_HEAD_DIM), lambda bi, h: (bi, 0, h)),
        out_shape=jax.ShapeDtypeStruct((b, s, width), BF16),
        compiler_params=_params("parallel", "parallel"),
        name="diff_attention",
    )(q, k, v, bias_tiles, rel_bias, lam, subln_g)


def _out_proj_kernel(o_ref, w_ref, x_ref, g_ref, y_ref):
    y_ref[...] = jnp.dot(o_ref[...], w_ref[...], preferred_element_type=F32)
    gain = g_ref[...]

    def body(rows):
        y_ref[rows, :] = x_ref[rows, :] + _rms(y_ref[rows, :], gain, RMS_EPS)
    _for_row_chunks(y_ref.shape[0], body)


def _out_proj(o, w, x, gain):
    m, d = x.shape
    kdim = o.shape[1]
    tm = PROJ_ROWS
    assert m % tm == 0
    return pl.pallas_call(
        _out_proj_kernel,
        grid=(m // tm,),
        in_specs=[
            pl.BlockSpec((tm, kdim), lambda i: (i, 0)),
            pl.BlockSpec((kdim, d), lambda i: (0, 0)),
            pl.BlockSpec((tm, d), lambda i: (i, 0)),
            pl.BlockSpec((1, d), lambda i: (0, 0)),
        ],
        out_specs=pl.BlockSpec((tm, d), lambda i: (i, 0)),
        out_shape=jax.ShapeDtypeStruct((m, d), F32),
        compiler_params=_params("parallel"),
        name="out_proj",
    )(o, w, x, gain)


def kernel(x, norm_gains, ffn_w_gate, ffn_w_up, ffn_w_down, pool_w, pool_scale, kv_norm, w_k, w_v,
           rel_bias, w_q, w_o, lambdas, subln_gain):
    b, s, d = x.shape
    depth = norm_gains.shape[0]
    n_a = pool_w.shape[0]
    gains = norm_gains.reshape(depth, 3, 2, 1, d)
    wg, wu, wd = ffn_w_gate, ffn_w_up, ffn_w_down

    xf = x.reshape(b * s, d)
    k = v = bias_tiles = None
    for l in range(depth):
        g = gains[l]
        if l == n_a:
            k, v = _norm_proj(xf, kv_norm.reshape(1, d), [w_k, w_v])
            bias_tiles = _bias_tiles(rel_bias, ATT_TILE)
        xf = _ffn(xf, g[0, 0], g[0, 1], wg, wu, wd, l, 0)
        if l < n_a:
            xf = _pool_mixer(xf.reshape(b, s, d), g[1, 0], g[1, 1], pool_w[l].astype(BF16),
                             pool_scale[l].reshape(1, d)).reshape(b * s, d)
        else:
            j = l - n_a
            lambda_init = 0.8 - 0.6 * math.exp(-0.3 * l)
            (q,) = _norm_proj(xf, g[1, 0], [w_q[j]], out_scale=HEAD_DIM ** -0.5 * LOG2_E)
            o = _attention(q.reshape(b, s, -1), k.reshape(b, s, -1), v.reshape(b, s, -1), bias_tiles,
                           rel_bias, lambdas[j], subln_gain[j].reshape(1, -1), lambda_init)
            xf = _out_proj(o.reshape(b * s, -1), w_o[j].astype(BF16), xf, g[1, 1])
        xf = _ffn(xf, g[2, 0], g[2, 1], wg, wu, wd, l, 1)
    return xf.reshape(b, s, d)
```

```python
import functools
import math

import numpy as np
import jax
import jax.numpy as jnp
from jax import lax
from jax.experimental import pallas as pl
from jax.experimental.pallas import tpu as pltpu

F32 = jnp.float32
BF16 = jnp.bfloat16

CHUNK = 64
POOL_WINDOWS = (2, 4, 8, 16)
HEAD_DIM = 128
V_HEAD_DIM = 2 * HEAD_DIM
NUM_BUCKETS = 32
MAX_DISTANCE = 128
RMS_EPS = 1e-6
SUBLN_EPS = 1e-5
LOG2_E = math.log2(math.e)

LANES = 128
SUBLANES = 8

VMEM_LIMIT_BYTES = 60 * 1024 * 1024
FFN_ROWS = 1024
FFN_COLS = 256
FFN_X_PREFETCH_STEP = 6
NORM_PROJ_ROWS = 1024
PROJ_ROWS = 1024
PROJ_COLS_TOTAL = 1024
NORM_ROW_CHUNK = 16
POOL_ROWS = 256
POOL_STAGES = 4
POOL_HALO = SUBLANES * POOL_STAGES
ATT_TILE = 256


def _rms(x, gain, eps):
    ms = jnp.mean(x * x, axis=-1, keepdims=True)
    return x * lax.rsqrt(ms + eps) * gain


def _for_row_chunks(n_rows, body):
    for r0 in range(0, n_rows, NORM_ROW_CHUNK):
        body(slice(r0, r0 + NORM_ROW_CHUNK))


def _norm_rows_to(dst_ref, src_ref, gain, eps):
    def body(rows):
        dst_ref[rows, :] = _rms(src_ref[rows, :], gain, eps).astype(dst_ref.dtype)
    _for_row_chunks(src_ref.shape[0], body)


def _params(*semantics):
    return pltpu.CompilerParams(dimension_semantics=semantics,
                                vmem_limit_bytes=VMEM_LIMIT_BYTES)


def _ffn_kernel(x_hbm, gpre_ref, gpost_ref, wg_ref, wu_ref, wd_ref, o_ref, h_ref, x_buf, x_sem):
    i = pl.program_id(0)
    j = pl.program_id(1)
    tm = o_ref.shape[0]
    slot = i % 2
    x_ref = x_buf.at[slot]

    def x_copy(tile, to_slot):
        rows = pl.ds(pl.multiple_of(tile * tm, tm), tm)
        return pltpu.make_async_copy(x_hbm.at[rows, :], x_buf.at[to_slot], x_sem.at[to_slot])

    def swiglu_block(first):
        h = h_ref[...]
        g = jnp.dot(h, wg_ref[...].astype(BF16), preferred_element_type=F32)
        u = jnp.dot(h, wu_ref[...].astype(BF16), preferred_element_type=F32)
        a = (g * (1.0 / (1.0 + jnp.exp(-g))) * u).astype(BF16)
        d = jnp.dot(a, wd_ref[...].astype(BF16), preferred_element_type=F32)
        if first:
            o_ref[...] = d
        else:
            o_ref[...] += d

    @pl.when(jnp.logical_and(i == 0, j == 0))
    def _():
        x_copy(0, 0).start()

    @pl.when(j == 0)
    def _():
        x_copy(i, slot).wait()
        _norm_rows_to(h_ref, x_ref, gpre_ref[...], RMS_EPS)
        swiglu_block(first=True)

    @pl.when(jnp.logical_and(j == FFN_X_PREFETCH_STEP, i + 1 < pl.num_programs(0)))
    def _():
        x_copy(i + 1, 1 - slot).start()

    @pl.when(j > 0)
    def _():
        swiglu_block(first=False)

    @pl.when(j == pl.num_programs(1) - 1)
    def _():
        half_gain = 0.5 * gpost_ref[...]

        def body(rows):
            o_ref[rows, :] = x_ref[rows, :] + _rms(o_ref[rows, :], half_gain, RMS_EPS)
        _for_row_chunks(o_ref.shape[0], body)


def _ffn(x, gpre, gpost, wg, wu, wd, layer, half):
    m, d = x.shape
    ff = wg.shape[-1]
    tm, tf = FFN_ROWS, FFN_COLS
    assert m % tm == 0 and ff % tf == 0
    return pl.pallas_call(
        _ffn_kernel,
        grid=(m // tm, ff // tf),
        in_specs=[
            pl.BlockSpec(memory_space=pl.ANY),
            pl.BlockSpec((1, d), lambda i, j: (0, 0)),
            pl.BlockSpec((1, d), lambda i, j: (0, 0)),
            pl.BlockSpec((None, None, d, tf), lambda i, j: (layer, half, 0, j)),
            pl.BlockSpec((None, None, d, tf), lambda i, j: (layer, half, 0, j)),
            pl.BlockSpec((None, None, tf, d), lambda i, j: (layer, half, j, 0)),
        ],
        out_specs=pl.BlockSpec((tm, d), lambda i, j: (i, 0)),
        out_shape=jax.ShapeDtypeStruct((m, d), F32),
        scratch_shapes=[pltpu.VMEM((tm, d), BF16), pltpu.VMEM((2, tm, d), F32),
                        pltpu.SemaphoreType.DMA((2,))],
        compiler_params=_params("arbitrary", "arbitrary"),
        name="ffn",
    )(x, gpre, gpost, wg, wu, wd)


def _pool_kernel(x_ref, halo_ref, gpre_ref, gpost_ref, w_ref, scale_ref, o_ref, *sum_refs):
    i = pl.program_id(1)
    ts = x_ref.shape[1]
    d = x_ref.shape[2]
    gdim = w_ref.shape[1]
    rows_all = POOL_HALO + ts
    hx_ref = sum_refs[0]
    gpre = gpre_ref[...]

    hh = _rms(halo_ref[0], gpre, RMS_EPS)
    hx_ref[0:POOL_HALO, :] = jnp.where(i > 0, hh, 0.0)
    for r0 in range(0, ts, NORM_ROW_CHUNK):
        hx_ref[POOL_HALO + r0:POOL_HALO + r0 + NORM_ROW_CHUNK, :] = _rms(
            x_ref[0, r0:r0 + NORM_ROW_CHUNK, :], gpre, RMS_EPS)

    for s in range(1, len(sum_refs)):
        shift = 2 ** (s - 1)
        r0 = SUBLANES * s
        c0 = (s - 1) * gdim
        prev = sum_refs[s - 1]
        sum_refs[s][r0:rows_all, c0:d] = (prev[r0:rows_all, c0:d]
                                          + prev[r0 - shift:rows_all - shift, c0:d])

    t = i * ts + lax.broadcasted_iota(jnp.int32, (ts, 1), 0)
    main = slice(POOL_HALO, rows_all)
    for gi, win in enumerate(POOL_WINDOWS):
        cols = slice(gi * gdim, (gi + 1) * gdim)
        stage = win.bit_length() - 1
        if stage < len(sum_refs):
            total = sum_refs[stage][main, cols]
        else:
            prev = sum_refs[stage - 1]
            half = win // 2
            total = prev[main, cols] + prev[POOL_HALO - half:rows_all - half, cols]
        inv_cnt = 1.0 / jnp.minimum(t + 1, win).astype(F32)
        dg = total * inv_cnt - hx_ref[main, cols]
        o_ref[0, :, cols] = (jnp.dot(dg.astype(BF16), w_ref[gi], preferred_element_type=F32)
                             * scale_ref[:, cols])

    gpost = gpost_ref[...]
    for r0 in range(0, ts, NORM_ROW_CHUNK):
        rows = slice(r0, r0 + NORM_ROW_CHUNK)
        o_ref[0, rows, :] = x_ref[0, rows, :] + _rms(o_ref[0, rows, :], gpost, RMS_EPS)


def _pool_mixer(x, gpre, gpost, w, scale):
    b, s, d = x.shape
    ts = POOL_ROWS
    assert s % ts == 0 and ts % POOL_HALO == 0
    assert POOL_WINDOWS == tuple(2 ** (g + 1) for g in range(POOL_STAGES))
    assert w.shape[0] == len(POOL_WINDOWS) and w.shape[1] * w.shape[0] == d
    halo_blocks = ts // POOL_HALO
    return pl.pallas_call(
        _pool_kernel,
        grid=(b, s // ts),
        in_specs=[
            pl.BlockSpec((1, ts, d), lambda bi, i: (bi, i, 0)),
            pl.BlockSpec((1, POOL_HALO, d),
                         lambda bi, i: (bi, jnp.maximum(i * halo_blocks - 1, 0), 0)),
            pl.BlockSpec((1, d), lambda bi, i: (0, 0)),
            pl.BlockSpec((1, d), lambda bi, i: (0, 0)),
            pl.BlockSpec(w.shape, lambda bi, i: (0, 0, 0)),
            pl.BlockSpec((1, d), lambda bi, i: (0, 0)),
        ],
        out_specs=pl.BlockSpec((1, ts, d), lambda bi, i: (bi, i, 0)),
        out_shape=jax.ShapeDtypeStruct((b, s, d), F32),
        scratch_shapes=[pltpu.VMEM((POOL_HALO + ts, d), F32)] * POOL_STAGES,
        compiler_params=_params("parallel", "parallel"),
        name="pool_mixer",
    )(x, x, gpre, gpost, w, scale)


def _norm_proj_kernel(n_out, out_scale, x_ref, g_ref, *refs):
    w_refs, o_refs, h_ref = refs[:n_out], refs[n_out:2 * n_out], refs[2 * n_out]

    @pl.when(pl.program_id(1) == 0)
    def _():
        _norm_rows_to(h_ref, x_ref, g_ref[...], RMS_EPS)

    h = h_ref[...]
    for w_ref, o_ref in zip(w_refs, o_refs):
        y = jnp.dot(h, w_ref[...].astype(BF16), preferred_element_type=F32)
        if out_scale != 1.0:
            y = y * out_scale
        o_ref[...] = y.astype(o_ref.dtype)


def _norm_proj(x, gain, weights, out_scale=1.0):
    m, d = x.shape
    n = weights[0].shape[1]
    n_out = len(weights)
    tm, tn = NORM_PROJ_ROWS, PROJ_COLS_TOTAL // n_out
    assert m % tm == 0 and n % tn == 0
    outs = pl.pallas_call(
        functools.partial(_norm_proj_kernel, n_out, out_scale),
        grid=(m // tm, n // tn),
        in_specs=[pl.BlockSpec((tm, d), lambda i, j: (i, 0)),
                  pl.BlockSpec((1, d), lambda i, j: (0, 0))]
                 + [pl.BlockSpec((d, tn), lambda i, j: (0, j))] * n_out,
        out_specs=[pl.BlockSpec((tm, tn), lambda i, j: (i, j))] * n_out,
        out_shape=[jax.ShapeDtypeStruct((m, n), BF16)] * n_out,
        scratch_shapes=[pltpu.VMEM((tm, d), BF16)],
        compiler_params=_params("parallel", "arbitrary"),
        name="norm_proj",
    )(x, gain, *weights)
    return outs


def _rel_bucket(rel):
    half = NUM_BUCKETS // 2
    max_exact = half // 2
    ret = jnp.where(rel > 0, half, 0)
    n = jnp.abs(rel)
    nf = jnp.maximum(n, 1).astype(F32)
    large = max_exact + (jnp.log(nf / max_exact) / math.log(MAX_DISTANCE / max_exact)
                         * (half - max_exact)).astype(jnp.int32)
    large = jnp.minimum(large, half - 1)
    return ret + jnp.where(n < max_exact, n, large)


FAR_BUCKET = NUM_BUCKETS // 2 - 1
FAR_DISTANCE = MAX_DISTANCE


def _bias_tile_constants(t):
    assert t + 1 >= FAR_DISTANCE and t % CHUNK == 0
    q = np.arange(t, dtype=np.int32)[:, None]
    k = np.arange(t, dtype=np.int32)[None, :]
    buckets = jnp.stack([_rel_bucket(jnp.asarray(k - q)), _rel_bucket(jnp.asarray(k - t - q))])
    allowed = np.stack([(k // CHUNK) <= (q // CHUNK), np.ones((t, t), bool)])
    mask = np.where(allowed, 0.0, -np.inf).astype(np.float32)
    return buckets, jnp.asarray(mask)


def _bias_kernel(table_ref, bucket_ref, mask_ref, o_ref):
    t = bucket_ref.shape[-1]
    table = jnp.broadcast_to(table_ref[0], (t, LANES))
    for tile in range(bucket_ref.shape[0]):
        for c0 in range(0, t, LANES):
            cols = slice(c0, c0 + LANES)
            looked_up = jnp.take_along_axis(table, bucket_ref[tile, :, cols], axis=1)
            o_ref[0, tile, :, cols] = looked_up * LOG2_E + mask_ref[tile, :, cols]


def _bias_tiles(rel_bias, t):
    buckets, mask = _bias_tile_constants(t)
    n_buckets, n_maps = rel_bias.shape
    assert n_buckets <= LANES and t % LANES == 0
    table = jnp.pad(rel_bias.T, ((0, 0), (0, LANES - n_buckets))).reshape(n_maps, 1, LANES)
    return pl.pallas_call(
        _bias_kernel,
        grid=(n_maps,),
        in_specs=[
            pl.BlockSpec((1, 1, LANES), lambda m: (m, 0, 0)),
            pl.BlockSpec((2, t, t), lambda m: (0, 0, 0)),
            pl.BlockSpec((2, t, t), lambda m: (0, 0, 0)),
        ],
        out_specs=pl.BlockSpec((1, 2, t, t), lambda m: (m, 0, 0, 0)),
        out_shape=jax.ShapeDtypeStruct((n_maps, 2, t, t), F32),
        compiler_params=_params("parallel"),
        name="bias_tiles",
    )(table, buckets, mask)


def _softmax_pv(qi, t, q_ref, k_ref, v_ref, cols, far_bias, bias_ref):
    qc = q_ref[0, qi * t:(qi + 1) * t, cols]
    near_bias, diag_bias = bias_ref.at[1], bias_ref.at[0]
    pieces = []
    if qi >= 2:
        pieces.append((0, (qi - 1) * t, None))
    if qi >= 1:
        pieces.append(((qi - 1) * t, t, near_bias))
    pieces.append((qi * t, t, diag_bias))

    scores, row_max = [], []
    for start, size, bias in pieces:
        kc = k_ref[0, start:start + size, cols]
        s = lax.dot_general(qc, kc, (((1,), (1,)), ((), ())), preferred_element_type=F32)
        if bias is None:
            row_max.append(jnp.max(s, axis=-1, keepdims=True) + far_bias)
        else:
            s = s + bias[...]
            row_max.append(jnp.max(s, axis=-1, keepdims=True))
        scores.append(s)
    m = functools.reduce(jnp.maximum, row_max)
    yield
    probs = []
    for s, (_, _, bias) in zip(scores, pieces):
        probs.append(jnp.exp2(s - ((m - far_bias) if bias is None else m)))
        if bias is None:
            yield
    denom = sum(jnp.sum(p, axis=-1, keepdims=True) for p in probs)
    yield
    acc = sum(jnp.dot(p.astype(BF16), v_ref[0, start:start + size, :], preferred_element_type=F32)
              for p, (start, size, _) in zip(probs, pieces))
    return acc * (1.0 / denom)


def _run_staggered(tasks, n_phases, on_done):
    finished = set()
    for step in range(len(tasks) + n_phases - 1):
        for phase in range(n_phases):
            n = step - phase
            if 0 <= n < len(tasks) and n not in finished:
                try:
                    next(tasks[n])
                except StopIteration as done:
                    finished.add(n)
                    on_done(n, done.value)


def _attn_kernel(lambda_init, q_ref, k_ref, v_ref, bias_ref, table_ref, lam_ref, subg_ref, o_ref):
    head = pl.program_id(1)
    t = bias_ref.shape[-1]

    lam = lam_ref[...]
    lam_full = (jnp.exp(jnp.sum(lam[0:1] * lam[1:2], keepdims=True))
                - jnp.exp(jnp.sum(lam[2:3] * lam[3:4], keepdims=True)) + lambda_init)

    tiles = list(reversed(range(q_ref.shape[1] // t)))
    tasks = []
    for i in tiles:
        for c in range(2):
            cols = slice(c * HEAD_DIM, (c + 1) * HEAD_DIM)
            tasks.append(_softmax_pv(i, t, q_ref, k_ref, v_ref, cols,
                                     table_ref[FAR_BUCKET, 2 * head + c] * LOG2_E, bias_ref.at[c]))

    first_map = {}

    def on_done(n, value):
        if n % 2 == 0:
            first_map[n // 2] = value
            return
        i = tiles[n // 2]
        o = first_map.pop(n // 2) - lam_full * value
        o = o * lax.rsqrt(jnp.mean(o * o, axis=-1, keepdims=True) + SUBLN_EPS)
        o_ref[0, i * t:(i + 1) * t, :] = (o * subg_ref[...] * (1.0 - lambda_init)).astype(o_ref.dtype)

    _run_staggered(tasks, 4, on_done)


def _attention(q, k, v, bias_tiles, rel_bias, lam, subln_g, lambda_init):
    b, s, width = q.shape
    t = ATT_TILE
    n_heads = width // V_HEAD_DIM
    assert s % t == 0
    return pl.pallas_call(
        functools.partial(_attn_kernel, lambda_init),
        grid=(b, n_heads),
        in_specs=[
            pl.BlockSpec((1, s, V_HEAD_DIM), lambda bi, h: (bi, 0, h)),
            pl.BlockSpec((1, s, V_HEAD_DIM), lambda bi, h: (bi, 0, h)),
            pl.BlockSpec((1, s, V_HEAD_DIM), lambda bi, h: (bi, 0, h)),
            pl.BlockSpec((2, 2, t, t), lambda bi, h: (h, 0, 0, 0)),
            pl.BlockSpec(memory_space=pltpu.SMEM),
            pl.BlockSpec(lam.shape, lambda bi, h: (0, 0)),
            pl.BlockSpec((1, V_HEAD_DIM), lambda bi, h: (0, 0)),
        ],
        out_specs=pl.BlockSpec((1, s, V_HEAD_DIM), lambda bi, h: (bi, 0, h)),
        out_shape=jax.ShapeDtypeStruct((b, s, width), BF16),
        compiler_params=_params("parallel", "parallel"),
        name="diff_attention",
    )(q, k, v, bias_tiles, rel_bias, lam, subln_g)


def _out_proj_kernel(o_ref, w_ref, x_ref, g_ref, y_ref):
    y_ref[...] = jnp.dot(o_ref[...], w_ref[...], preferred_element_type=F32)
    gain = g_ref[...]

    def body(rows):
        y_ref[rows, :] = x_ref[rows, :] + _rms(y_ref[rows, :], gain, RMS_EPS)
    _for_row_chunks(y_ref.shape[0], body)


def _out_proj(o, w, x, gain):
    m, d = x.shape
    kdim = o.shape[1]
    tm = PROJ_ROWS
    assert m % tm == 0
    return pl.pallas_call(
        _out_proj_kernel,
        grid=(m // tm,),
        in_specs=[
            pl.BlockSpec((tm, kdim), lambda i: (i, 0)),
            pl.BlockSpec((kdim, d), lambda i: (0, 0)),
            pl.BlockSpec((tm, d), lambda i: (i, 0)),
            pl.BlockSpec((1, d), lambda i: (0, 0)),
        ],
        out_specs=pl.BlockSpec((tm, d), lambda i: (i, 0)),
        out_shape=jax.ShapeDtypeStruct((m, d), F32),
        compiler_params=_params("parallel"),
        name="out_proj",
    )(o, w, x, gain)


def kernel(x, norm_gains, ffn_w_gate, ffn_w_up, ffn_w_down, pool_w, pool_scale, kv_norm, w_k, w_v,
           rel_bias, w_q, w_o, lambdas, subln_gain):
    b, s, d = x.shape
    depth = norm_gains.shape[0]
    n_a = pool_w.shape[0]
    gains = norm_gains.reshape(depth, 3, 2, 1, d)
    wg, wu, wd = ffn_w_gate, ffn_w_up, ffn_w_down

    xf = x.reshape(b * s, d)
    k = v = bias_tiles = None
    for l in range(depth):
        g = gains[l]
        if l == n_a:
            k, v = _norm_proj(xf, kv_norm.reshape(1, d), [w_k, w_v])
            bias_tiles = _bias_tiles(rel_bias, ATT_TILE)
        xf = _ffn(xf, g[0, 0], g[0, 1], wg, wu, wd, l, 0)
        if l < n_a:
            xf = _pool_mixer(xf.reshape(b, s, d), g[1, 0], g[1, 1], pool_w[l].astype(BF16),
                             pool_scale[l].reshape(1, d)).reshape(b * s, d)
        else:
            j = l - n_a
            lambda_init = 0.8 - 0.6 * math.exp(-0.3 * l)
            (q,) = _norm_proj(xf, g[1, 0], [w_q[j]], out_scale=HEAD_DIM ** -0.5 * LOG2_E)
            o = _attention(q.reshape(b, s, -1), k.reshape(b, s, -1), v.reshape(b, s, -1), bias_tiles,
                           rel_bias, lambdas[j], subln_gain[j].reshape(1, -1), lambda_init)
            xf = _out_proj(o.reshape(b * s, -1), w_o[j].astype(BF16), xf, g[1, 1])
        xf = _ffn(xf, g[2, 0], g[2, 1], wg, wu, wd, l, 1)
    return xf.reshape(b, s, d)
```

```python
import functools
import math

import numpy as np
import jax
import jax.numpy as jnp
from jax import lax
from jax.experimental import pallas as pl
from jax.experimental.pallas import tpu as pltpu

F32 = jnp.float32
BF16 = jnp.bfloat16

CHUNK = 64
POOL_WINDOWS = (2, 4, 8, 16)
HEAD_DIM = 128
V_HEAD_DIM = 2 * HEAD_DIM
NUM_BUCKETS = 32
MAX_DISTANCE = 128
RMS_EPS = 1e-6
SUBLN_EPS = 1e-5
LOG2_E = math.log2(math.e)

LANES = 128
SUBLANES = 8

VMEM_LIMIT_BYTES = 60 * 1024 * 1024
FFN_ROWS = 1024
FFN_COLS = 256
FFN_X_PREFETCH_STEP = 6
NORM_PROJ_ROWS = 1024
PROJ_ROWS = 1024
PROJ_COLS_TOTAL = 1024
NORM_ROW_CHUNK = 16
POOL_ROWS = 256
POOL_STAGES = 4
POOL_HALO = SUBLANES * POOL_STAGES
ATT_TILE = 256


def _rms(x, gain, eps):
    ms = jnp.mean(x * x, axis=-1, keepdims=True)
    return x * lax.rsqrt(ms + eps) * gain


def _for_row_chunks(n_rows, body):
    for r0 in range(0, n_rows, NORM_ROW_CHUNK):
        body(slice(r0, r0 + NORM_ROW_CHUNK))


def _norm_rows_to(dst_ref, src_ref, gain, eps):
    def body(rows):
        dst_ref[rows, :] = _rms(src_ref[rows, :], gain, eps).astype(dst_ref.dtype)
    _for_row_chunks(src_ref.shape[0], body)


def _params(*semantics):
    return pltpu.CompilerParams(dimension_semantics=semantics,
                                vmem_limit_bytes=VMEM_LIMIT_BYTES)


def _ffn_kernel(x_hbm, gpre_ref, gpost_ref, wg_ref, wu_ref, wd_ref, o_ref, h_ref, x_buf, x_sem):
    i = pl.program_id(0)
    j = pl.program_id(1)
    tm = o_ref.shape[0]
    slot = i % 2
    x_ref = x_buf.at[slot]

    def x_copy(tile, to_slot):
        rows = pl.ds(pl.multiple_of(tile * tm, tm), tm)
        return pltpu.make_async_copy(x_hbm.at[rows, :], x_buf.at[to_slot], x_sem.at[to_slot])

    def swiglu_block(first):
        h = h_ref[...]
        g = jnp.dot(h, wg_ref[...].astype(BF16), preferred_element_type=F32)
        u = jnp.dot(h, wu_ref[...].astype(BF16), preferred_element_type=F32)
        a = (g * (1.0 / (1.0 + jnp.exp(-g))) * u).astype(BF16)
        d = jnp.dot(a, wd_ref[...].astype(BF16), preferred_element_type=F32)
        if first:
            o_ref[...] = d
        else:
            o_ref[...] += d

    @pl.when(jnp.logical_and(i == 0, j == 0))
    def _():
        x_copy(0, 0).start()

    @pl.when(j == 0)
    def _():
        x_copy(i, slot).wait()
        _norm_rows_to(h_ref, x_ref, gpre_ref[...], RMS_EPS)
        swiglu_block(first=True)

    @pl.when(jnp.logical_and(j == FFN_X_PREFETCH_STEP, i + 1 < pl.num_programs(0)))
    def _():
        x_copy(i + 1, 1 - slot).start()

    @pl.when(j > 0)
    def _():
        swiglu_block(first=False)

    @pl.when(j == pl.num_programs(1) - 1)
    def _():
        half_gain = 0.5 * gpost_ref[...]

        def body(rows):
            o_ref[rows, :] = x_ref[rows, :] + _rms(o_ref[rows, :], half_gain, RMS_EPS)
        _for_row_chunks(o_ref.shape[0], body)


def _ffn(x, gpre, gpost, wg, wu, wd, layer, half):
    m, d = x.shape
    ff = wg.shape[-1]
    tm, tf = FFN_ROWS, FFN_COLS
    assert m % tm == 0 and ff % tf == 0
    return pl.pallas_call(
        _ffn_kernel,
        grid=(m // tm, ff // tf),
        in_specs=[
            pl.BlockSpec(memory_space=pl.ANY),
            pl.BlockSpec((1, d), lambda i, j: (0, 0)),
            pl.BlockSpec((1, d), lambda i, j: (0, 0)),
            pl.BlockSpec((None, None, d, tf), lambda i, j: (layer, half, 0, j)),
            pl.BlockSpec((None, None, d, tf), lambda i, j: (layer, half, 0, j)),
            pl.BlockSpec((None, None, tf, d), lambda i, j: (layer, half, j, 0)),
        ],
        out_specs=pl.BlockSpec((tm, d), lambda i, j: (i, 0)),
        out_shape=jax.ShapeDtypeStruct((m, d), F32),
        scratch_shapes=[pltpu.VMEM((tm, d), BF16), pltpu.VMEM((2, tm, d), F32),
                        pltpu.SemaphoreType.DMA((2,))],
        compiler_params=_params("arbitrary", "arbitrary"),
        name="ffn",
    )(x, gpre, gpost, wg, wu, wd)


def _pool_kernel(x_ref, halo_ref, gpre_ref, gpost_ref, w_ref, scale_ref, o_ref, *sum_refs):
    i = pl.program_id(1)
    ts = x_ref.shape[1]
    d = x_ref.shape[2]
    gdim = w_ref.shape[1]
    rows_all = POOL_HALO + ts
    hx_ref = sum_refs[0]
    gpre = gpre_ref[...]

    hh = _rms(halo_ref[0], gpre, RMS_EPS)
    hx_ref[0:POOL_HALO, :] = jnp.where(i > 0, hh, 0.0)
    for r0 in range(0, ts, NORM_ROW_CHUNK):
        hx_ref[POOL_HALO + r0:POOL_HALO + r0 + NORM_ROW_CHUNK, :] = _rms(
            x_ref[0, r0:r0 + NORM_ROW_CHUNK, :], gpre, RMS_EPS)

    for s in range(1, len(sum_refs)):
        shift = 2 ** (s - 1)
        r0 = SUBLANES * s
        c0 = (s - 1) * gdim
        prev = sum_refs[s - 1]
        sum_refs[s][r0:rows_all, c0:d] = (prev[r0:rows_all, c0:d]
                                          + prev[r0 - shift:rows_all - shift, c0:d])

    t = i * ts + lax.broadcasted_iota(jnp.int32, (ts, 1), 0)
    main = slice(POOL_HALO, rows_all)
    for gi, win in enumerate(POOL_WINDOWS):
        cols = slice(gi * gdim, (gi + 1) * gdim)
        stage = win.bit_length() - 1
        if stage < len(sum_refs):
            total = sum_refs[stage][main, cols]
        else:
            prev = sum_refs[stage - 1]
            half = win // 2
            total = prev[main, cols] + prev[POOL_HALO - half:rows_all - half, cols]
        inv_cnt = 1.0 / jnp.minimum(t + 1, win).astype(F32)
        dg = total * inv_cnt - hx_ref[main, cols]
        o_ref[0, :, cols] = (jnp.dot(dg.astype(BF16), w_ref[gi], preferred_element_type=F32)
                             * scale_ref[:, cols])

    gpost = gpost_ref[...]
    for r0 in range(0, ts, NORM_ROW_CHUNK):
        rows = slice(r0, r0 + NORM_ROW_CHUNK)
        o_ref[0, rows, :] = x_ref[0, rows, :] + _rms(o_ref[0, rows, :], gpost, RMS_EPS)


def _pool_mixer(x, gpre, gpost, w, scale):
    b, s, d = x.shape
    ts = POOL_ROWS
    assert s % ts == 0 and ts % POOL_HALO == 0
    assert POOL_WINDOWS == tuple(2 ** (g + 1) for g in range(POOL_STAGES))
    assert w.shape[0] == len(POOL_WINDOWS) and w.shape[1] * w.shape[0] == d
    halo_blocks = ts // POOL_HALO
    return pl.pallas_call(
        _pool_kernel,
        grid=(b, s // ts),
        in_specs=[
            pl.BlockSpec((1, ts, d), lambda bi, i: (bi, i, 0)),
            pl.BlockSpec((1, POOL_HALO, d),
                         lambda bi, i: (bi, jnp.maximum(i * halo_blocks - 1, 0), 0)),
            pl.BlockSpec((1, d), lambda bi, i: (0, 0)),
            pl.BlockSpec((1, d), lambda bi, i: (0, 0)),
            pl.BlockSpec(w.shape, lambda bi, i: (0, 0, 0)),
            pl.BlockSpec((1, d), lambda bi, i: (0, 0)),
        ],
        out_specs=pl.BlockSpec((1, ts, d), lambda bi, i: (bi, i, 0)),
        out_shape=jax.ShapeDtypeStruct((b, s, d), F32),
        scratch_shapes=[pltpu.VMEM((POOL_HALO + ts, d), F32)] * POOL_STAGES,
        compiler_params=_params("parallel", "parallel"),
        name="pool_mixer",
    )(x, x, gpre, gpost, w, scale)


def _norm_proj_kernel(n_out, out_scale, x_ref, g_ref, *refs):
    w_refs, o_refs, h_ref = refs[:n_out], refs[n_out:2 * n_out], refs[2 * n_out]

    @pl.when(pl.program_id(1) == 0)
    def _():
        _norm_rows_to(h_ref, x_ref, g_ref[...], RMS_EPS)

    h = h_ref[...]
    for w_ref, o_ref in zip(w_refs, o_refs):
        y = jnp.dot(h, w_ref[...].astype(BF16), preferred_element_type=F32)
        if out_scale != 1.0:
            y = y * out_scale
        o_ref[...] = y.astype(o_ref.dtype)


def _norm_proj(x, gain, weights, out_scale=1.0):
    m, d = x.shape
    n = weights[0].shape[1]
    n_out = len(weights)
    tm, tn = NORM_PROJ_ROWS, PROJ_COLS_TOTAL // n_out
    assert m % tm == 0 and n % tn == 0
    outs = pl.pallas_call(
        functools.partial(_norm_proj_kernel, n_out, out_scale),
        grid=(m // tm, n // tn),
        in_specs=[pl.BlockSpec((tm, d), lambda i, j: (i, 0)),
                  pl.BlockSpec((1, d), lambda i, j: (0, 0))]
                 + [pl.BlockSpec((d, tn), lambda i, j: (0, j))] * n_out,
        out_specs=[pl.BlockSpec((tm, tn), lambda i, j: (i, j))] * n_out,
        out_shape=[jax.ShapeDtypeStruct((m, n), BF16)] * n_out,
        scratch_shapes=[pltpu.VMEM((tm, d), BF16)],
        compiler_params=_params("parallel", "arbitrary"),
        name="norm_proj",
    )(x, gain, *weights)
    return outs


def _rel_bucket(rel):
    half = NUM_BUCKETS // 2
    max_exact = half // 2
    ret = jnp.where(rel > 0, half, 0)
    n = jnp.abs(rel)
    nf = jnp.maximum(n, 1).astype(F32)
    large = max_exact + (jnp.log(nf / max_exact) / math.log(MAX_DISTANCE / max_exact)
                         * (half - max_exact)).astype(jnp.int32)
    large = jnp.minimum(large, half - 1)
    return ret + jnp.where(n < max_exact, n, large)


FAR_BUCKET = NUM_BUCKETS // 2 - 1
FAR_DISTANCE = MAX_DISTANCE


def _bias_tile_constants(t):
    assert t + 1 >= FAR_DISTANCE and t % CHUNK == 0
    q = np.arange(t, dtype=np.int32)[:, None]
    k = np.arange(t, dtype=np.int32)[None, :]
    buckets = jnp.stack([_rel_bucket(jnp.asarray(k - q)), _rel_bucket(jnp.asarray(k - t - q))])
    allowed = np.stack([(k // CHUNK) <= (q // CHUNK), np.ones((t, t), bool)])
    mask = np.where(allowed, 0.0, -np.inf).astype(np.float32)
    return buckets, jnp.asarray(mask)


def _bias_kernel(table_ref, bucket_ref, mask_ref, o_ref):
    t = bucket_ref.shape[-1]
    table = jnp.broadcast_to(table_ref[0], (t, LANES))
    for tile in range(bucket_ref.shape[0]):
        for c0 in range(0, t, LANES):
            cols = slice(c0, c0 + LANES)
            looked_up = jnp.take_along_axis(table, bucket_ref[tile, :, cols], axis=1)
            o_ref[0, tile, :, cols] = looked_up * LOG2_E + mask_ref[tile, :, cols]


def _bias_tiles(rel_bias, t):
    buckets, mask = _bias_tile_constants(t)
    n_buckets, n_maps = rel_bias.shape
    assert n_buckets <= LANES and t % LANES == 0
    table = jnp.pad(rel_bias.T, ((0, 0), (0, LANES - n_buckets))).reshape(n_maps, 1, LANES)
    return pl.pallas_call(
        _bias_kernel,
        grid=(n_maps,),
        in_specs=[
            pl.BlockSpec((1, 1, LANES), lambda m: (m, 0, 0)),
            pl.BlockSpec((2, t, t), lambda m: (0, 0, 0)),
            pl.BlockSpec((2, t, t), lambda m: (0, 0, 0)),
        ],
        out_specs=pl.BlockSpec((1, 2, t, t), lambda m: (m, 0, 0, 0)),
        out_shape=jax.ShapeDtypeStruct((n_maps, 2, t, t), F32),
        compiler_params=_params("parallel"),
        name="bias_tiles",
    )(table, buckets, mask)


def _softmax_pv(qi, t, q_ref, k_ref, v_ref, cols, far_bias, bias_ref):
    qc = q_ref[0, qi * t:(qi + 1) * t, cols]
    near_bias, diag_bias = bias_ref.at[1], bias_ref.at[0]
    pieces = []
    if qi >= 2:
        pieces.append((0, (qi - 1) * t, None))
    if qi >= 1:
        pieces.append(((qi - 1) * t, t, near_bias))
    pieces.append((qi * t, t, diag_bias))

    scores, row_max = [], []
    for start, size, bias in pieces:
        kc = k_ref[0, start:start + size, cols]
        s = lax.dot_general(qc, kc, (((1,), (1,)), ((), ())), preferred_element_type=F32)
        if bias is None:
            row_max.append(jnp.max(s, axis=-1, keepdims=True) + far_bias)
        else:
            s = s + bias[...]
            row_max.append(jnp.max(s, axis=-1, keepdims=True))
        scores.append(s)
    m = functools.reduce(jnp.maximum, row_max)
    yield
    probs = []
    for s, (_, _, bias) in zip(scores, pieces):
        probs.append(jnp.exp2(s - ((m - far_bias) if bias is None else m)))
        if bias is None:
            yield
    denom = sum(jnp.sum(p, axis=-1, keepdims=True) for p in probs)
    return probs, denom, [(start, size) for start, size, _ in pieces]


def _run_staggered(tasks, n_phases, on_done):
    finished = set()
    for step in range(len(tasks) + n_phases - 1):
        for phase in range(n_phases):
            n = step - phase
            if 0 <= n < len(tasks) and n not in finished:
                try:
                    next(tasks[n])
                except StopIteration as done:
                    finished.add(n)
                    on_done(n, done.value)


def _attn_kernel(lambda_init, q_ref, k_ref, v_ref, bias_ref, table_ref, lam_ref, subg_ref, o_ref):
    head = pl.program_id(1)
    t = bias_ref.shape[-1]

    lam = lam_ref[...]
    lam_full = (jnp.exp(jnp.sum(lam[0:1] * lam[1:2], keepdims=True))
                - jnp.exp(jnp.sum(lam[2:3] * lam[3:4], keepdims=True)) + lambda_init)

    tiles = list(reversed(range(q_ref.shape[1] // t)))
    tasks = []
    for i in tiles:
        for c in range(2):
            cols = slice(c * HEAD_DIM, (c + 1) * HEAD_DIM)
            tasks.append(_softmax_pv(i, t, q_ref, k_ref, v_ref, cols,
                                     table_ref[FAR_BUCKET, 2 * head + c] * LOG2_E, bias_ref.at[c]))

    first_map = {}

    def on_done(n, value):
        if n % 2 == 0:
            first_map[n // 2] = value
            return
        i = tiles[n // 2]
        probs0, denom0, ranges = first_map.pop(n // 2)
        probs1, denom1, _ = value
        w0 = 1.0 / denom0
        w1 = lam_full / denom1
        o = sum(jnp.dot((p0 * w0 - p1 * w1).astype(BF16), v_ref[0, start:start + size, :],
                        preferred_element_type=F32)
                for p0, p1, (start, size) in zip(probs0, probs1, ranges))
        o = o * lax.rsqrt(jnp.mean(o * o, axis=-1, keepdims=True) + SUBLN_EPS)
        o_ref[0, i * t:(i + 1) * t, :] = (o * subg_ref[...] * (1.0 - lambda_init)).astype(o_ref.dtype)

    _run_staggered(tasks, 3, on_done)


def _attention(q, k, v, bias_tiles, rel_bias, lam, subln_g, lambda_init):
    b, s, width = q.shape
    t = ATT_TILE
    n_heads = width // V_HEAD_DIM
    assert s % t == 0
    return pl.pallas_call(
        functools.partial(_attn_kernel, lambda_init),
        grid=(b, n_heads),
        in_specs=[
            pl.BlockSpec((1, s, V_HEAD_DIM), lambda bi, h: (bi, 0, h)),
            pl.BlockSpec((1, s, V_HEAD_DIM), lambda bi, h: (bi, 0, h)),
            pl.BlockSpec((1, s, V_HEAD_DIM), lambda bi, h: (bi, 0, h)),
            pl.BlockSpec((2, 2, t, t), lambda bi, h: (h, 0, 0, 0)),
            pl.BlockSpec(memory_space=pltpu.SMEM),
            pl.BlockSpec(lam.shape, lambda bi, h: (0, 0)),
            pl.BlockSpec((1, V_HEAD_DIM), lambda bi, h: (0, 0)),
        ],
        out_specs=pl.BlockSpec((1, s, V_HEAD_DIM), lambda bi, h: (bi, 0, h)),
        out_shape=jax.ShapeDtypeStruct((b, s, width), BF16),
        compiler_params=_params("parallel", "parallel"),
        name="diff_attention",
    )(q, k, v, bias_tiles, rel_bias, lam, subln_g)


def _out_proj_kernel(o_ref, w_ref, x_ref, g_ref, y_ref):
    y_ref[...] = jnp.dot(o_ref[...], w_ref[...], preferred_element_type=F32)
    gain = g_ref[...]

    def body(rows):
        y_ref[rows, :] = x_ref[rows, :] + _rms(y_ref[rows, :], gain, RMS_EPS)
    _for_row_chunks(y_ref.shape[0], body)


def _out_proj(o, w, x, gain):
    m, d = x.shape
    kdim = o.shape[1]
    tm = PROJ_ROWS
    assert m % tm == 0
    return pl.pallas_call(
        _out_proj_kernel,
        grid=(m // tm,),
        in_specs=[
            pl.BlockSpec((tm, kdim), lambda i: (i, 0)),
            pl.BlockSpec((kdim, d), lambda i: (0, 0)),
            pl.BlockSpec((tm, d), lambda i: (i, 0)),
            pl.BlockSpec((1, d), lambda i: (0, 0)),
        ],
        out_specs=pl.BlockSpec((tm, d), lambda i: (i, 0)),
        out_shape=jax.ShapeDtypeStruct((m, d), F32),
        compiler_params=_params("parallel"),
        name="out_proj",
    )(o, w, x, gain)


def kernel(x, norm_gains, ffn_w_gate, ffn_w_up, ffn_w_down, pool_w, pool_scale, kv_norm, w_k, w_v,
           rel_bias, w_q, w_o, lambdas, subln_gain):
    b, s, d = x.shape
    depth = norm_gains.shape[0]
    n_a = pool_w.shape[0]
    gains = norm_gains.reshape(depth, 3, 2, 1, d)
    wg, wu, wd = ffn_w_gate, ffn_w_up, ffn_w_down

    xf = x.reshape(b * s, d)
    k = v = bias_tiles = None
    for l in range(depth):
        g = gains[l]
        if l == n_a:
            k, v = _norm_proj(xf, kv_norm.reshape(1, d), [w_k, w_v])
            bias_tiles = _bias_tiles(rel_bias, ATT_TILE)
        xf = _ffn(xf, g[0, 0], g[0, 1], wg, wu, wd, l, 0)
        if l < n_a:
            xf = _pool_mixer(xf.reshape(b, s, d), g[1, 0], g[1, 1], pool_w[l].astype(BF16),
                             pool_scale[l].reshape(1, d)).reshape(b * s, d)
        else:
            j = l - n_a
            lambda_init = 0.8 - 0.6 * math.exp(-0.3 * l)
            (q,) = _norm_proj(xf, g[1, 0], [w_q[j]], out_scale=HEAD_DIM ** -0.5 * LOG2_E)
            o = _attention(q.reshape(b, s, -1), k.reshape(b, s, -1), v.reshape(b, s, -1), bias_tiles,
                           rel_bias, lambdas[j], subln_gain[j].reshape(1, -1), lambda_init)
            xf = _out_proj(o.reshape(b * s, -1), w_o[j].astype(BF16), xf, g[1, 1])
        xf = _ffn(xf, g[2, 0], g[2, 1], wg, wu, wd, l, 1)
    return xf.reshape(b, s, d)
```

```python
import functools
import math

import numpy as np
import jax
import jax.numpy as jnp
from jax import lax
from jax.experimental import pallas as pl
from jax.experimental.pallas import tpu as pltpu

F32 = jnp.float32
BF16 = jnp.bfloat16

CHUNK = 64
POOL_WINDOWS = (2, 4, 8, 16)
HEAD_DIM = 128
V_HEAD_DIM = 2 * HEAD_DIM
NUM_BUCKETS = 32
MAX_DISTANCE = 128
RMS_EPS = 1e-6
SUBLN_EPS = 1e-5
LOG2_E = math.log2(math.e)

LANES = 128
SUBLANES = 8

VMEM_LIMIT_BYTES = 60 * 1024 * 1024
FFN_ROWS = 1024
FFN_COLS = 256
FFN_X_PREFETCH_STEP = 6
NORM_PROJ_ROWS = 1024
PROJ_ROWS = 512
PROJ_COLS_TOTAL = 1024
NORM_ROW_CHUNK = 16
POOL_ROWS = 256
POOL_STAGES = 4
POOL_HALO = SUBLANES * POOL_STAGES
ATT_TILE = 256


def _rms(x, gain, eps):
    ms = jnp.mean(x * x, axis=-1, keepdims=True)
    return x * lax.rsqrt(ms + eps) * gain


def _for_row_chunks(n_rows, body):
    for r0 in range(0, n_rows, NORM_ROW_CHUNK):
        body(slice(r0, r0 + NORM_ROW_CHUNK))


def _norm_rows_to(dst_ref, src_ref, gain, eps):
    def body(rows):
        dst_ref[rows, :] = _rms(src_ref[rows, :], gain, eps).astype(dst_ref.dtype)
    _for_row_chunks(src_ref.shape[0], body)


def _params(*semantics):
    return pltpu.CompilerParams(dimension_semantics=semantics,
                                vmem_limit_bytes=VMEM_LIMIT_BYTES)


def _ffn_kernel(x_hbm, gpre_ref, gpost_ref, wg_ref, wu_ref, wd_ref, o_ref, h_ref, x_buf, x_sem):
    i = pl.program_id(0)
    j = pl.program_id(1)
    tm = o_ref.shape[0]
    slot = i % 2
    x_ref = x_buf.at[slot]

    def x_copy(tile, to_slot):
        rows = pl.ds(pl.multiple_of(tile * tm, tm), tm)
        return pltpu.make_async_copy(x_hbm.at[rows, :], x_buf.at[to_slot], x_sem.at[to_slot])

    def swiglu_block(first):
        h = h_ref[...]
        g = jnp.dot(h, wg_ref[...].astype(BF16), preferred_element_type=F32)
        u = jnp.dot(h, wu_ref[...].astype(BF16), preferred_element_type=F32)
        a = (g * (1.0 / (1.0 + jnp.exp(-g))) * u).astype(BF16)
        d = jnp.dot(a, wd_ref[...].astype(BF16), preferred_element_type=F32)
        if first:
            o_ref[...] = d
        else:
            o_ref[...] += d

    @pl.when(jnp.logical_and(i == 0, j == 0))
    def _():
        x_copy(0, 0).start()

    @pl.when(j == 0)
    def _():
        x_copy(i, slot).wait()
        _norm_rows_to(h_ref, x_ref, gpre_ref[...], RMS_EPS)
        swiglu_block(first=True)

    @pl.when(jnp.logical_and(j == FFN_X_PREFETCH_STEP, i + 1 < pl.num_programs(0)))
    def _():
        x_copy(i + 1, 1 - slot).start()

    @pl.when(j > 0)
    def _():
        swiglu_block(first=False)

    @pl.when(j == pl.num_programs(1) - 1)
    def _():
        half_gain = 0.5 * gpost_ref[...]

        def body(rows):
            o_ref[rows, :] = x_ref[rows, :] + _rms(o_ref[rows, :], half_gain, RMS_EPS)
        _for_row_chunks(o_ref.shape[0], body)


def _ffn(x, gpre, gpost, wg, wu, wd, layer, half):
    m, d = x.shape
    ff = wg.shape[-1]
    tm, tf = FFN_ROWS, FFN_COLS
    assert m % tm == 0 and ff % tf == 0 and 0 < FFN_X_PREFETCH_STEP < ff // tf
    return pl.pallas_call(
        _ffn_kernel,
        grid=(m // tm, ff // tf),
        in_specs=[
            pl.BlockSpec(memory_space=pl.ANY),
            pl.BlockSpec((1, d), lambda i, j: (0, 0)),
            pl.BlockSpec((1, d), lambda i, j: (0, 0)),
            pl.BlockSpec((None, None, d, tf), lambda i, j: (layer, half, 0, j)),
            pl.BlockSpec((None, None, d, tf), lambda i, j: (layer, half, 0, j)),
            pl.BlockSpec((None, None, tf, d), lambda i, j: (layer, half, j, 0)),
        ],
        out_specs=pl.BlockSpec((tm, d), lambda i, j: (i, 0)),
        out_shape=jax.ShapeDtypeStruct((m, d), F32),
        scratch_shapes=[pltpu.VMEM((tm, d), BF16), pltpu.VMEM((2, tm, d), F32),
                        pltpu.SemaphoreType.DMA((2,))],
        compiler_params=_params("arbitrary", "arbitrary"),
        name="ffn",
    )(x, gpre, gpost, wg, wu, wd)


def _pool_kernel(x_ref, halo_ref, gpre_ref, gpost_ref, w_ref, scale_ref, o_ref, *sum_refs):
    i = pl.program_id(1)
    ts = x_ref.shape[1]
    d = x_ref.shape[2]
    gdim = w_ref.shape[1]
    rows_all = POOL_HALO + ts
    hx_ref = sum_refs[0]
    gpre = gpre_ref[...]

    hh = _rms(halo_ref[0], gpre, RMS_EPS)
    hx_ref[0:POOL_HALO, :] = jnp.where(i > 0, hh, 0.0)
    for r0 in range(0, ts, NORM_ROW_CHUNK):
        hx_ref[POOL_HALO + r0:POOL_HALO + r0 + NORM_ROW_CHUNK, :] = _rms(
            x_ref[0, r0:r0 + NORM_ROW_CHUNK, :], gpre, RMS_EPS)

    for s in range(1, len(sum_refs)):
        shift = 2 ** (s - 1)
        r0 = SUBLANES * s
        c0 = (s - 1) * gdim
        prev = sum_refs[s - 1]
        sum_refs[s][r0:rows_all, c0:d] = (prev[r0:rows_all, c0:d]
                                          + prev[r0 - shift:rows_all - shift, c0:d])

    t = i * ts + lax.broadcasted_iota(jnp.int32, (ts, 1), 0)
    main = slice(POOL_HALO, rows_all)
    for gi, win in enumerate(POOL_WINDOWS):
        cols = slice(gi * gdim, (gi + 1) * gdim)
        stage = win.bit_length() - 1
        if stage < len(sum_refs):
            total = sum_refs[stage][main, cols]
        else:
            prev = sum_refs[stage - 1]
            half = win // 2
            total = prev[main, cols] + prev[POOL_HALO - half:rows_all - half, cols]
        inv_cnt = 1.0 / jnp.minimum(t + 1, win).astype(F32)
        dg = total * inv_cnt - hx_ref[main, cols]
        o_ref[0, :, cols] = (jnp.dot(dg.astype(BF16), w_ref[gi], preferred_element_type=F32)
                             * scale_ref[:, cols])

    gpost = gpost_ref[...]
    for r0 in range(0, ts, NORM_ROW_CHUNK):
        rows = slice(r0, r0 + NORM_ROW_CHUNK)
        o_ref[0, rows, :] = x_ref[0, rows, :] + _rms(o_ref[0, rows, :], gpost, RMS_EPS)


def _pool_mixer(x, gpre, gpost, w, scale):
    b, s, d = x.shape
    ts = POOL_ROWS
    assert s % ts == 0 and ts % POOL_HALO == 0
    assert POOL_WINDOWS == tuple(2 ** (g + 1) for g in range(POOL_STAGES))
    assert w.shape[0] == len(POOL_WINDOWS) and w.shape[1] * w.shape[0] == d
    halo_blocks = ts // POOL_HALO
    return pl.pallas_call(
        _pool_kernel,
        grid=(b, s // ts),
        in_specs=[
            pl.BlockSpec((1, ts, d), lambda bi, i: (bi, i, 0)),
            pl.BlockSpec((1, POOL_HALO, d),
                         lambda bi, i: (bi, jnp.maximum(i * halo_blocks - 1, 0), 0)),
            pl.BlockSpec((1, d), lambda bi, i: (0, 0)),
            pl.BlockSpec((1, d), lambda bi, i: (0, 0)),
            pl.BlockSpec(w.shape, lambda bi, i: (0, 0, 0)),
            pl.BlockSpec((1, d), lambda bi, i: (0, 0)),
        ],
        out_specs=pl.BlockSpec((1, ts, d), lambda bi, i: (bi, i, 0)),
        out_shape=jax.ShapeDtypeStruct((b, s, d), F32),
        scratch_shapes=[pltpu.VMEM((POOL_HALO + ts, d), F32)] * POOL_STAGES,
        compiler_params=_params("parallel", "parallel"),
        name="pool_mixer",
    )(x, x, gpre, gpost, w, scale)


def _norm_proj_kernel(n_out, out_scale, x_ref, g_ref, *refs):
    w_refs, o_refs, h_ref = refs[:n_out], refs[n_out:2 * n_out], refs[2 * n_out]

    @pl.when(pl.program_id(1) == 0)
    def _():
        _norm_rows_to(h_ref, x_ref, g_ref[...], RMS_EPS)

    h = h_ref[...]
    for w_ref, o_ref in zip(w_refs, o_refs):
        y = jnp.dot(h, w_ref[...].astype(BF16), preferred_element_type=F32)
        if out_scale != 1.0:
            y = y * out_scale
        o_ref[...] = y.astype(o_ref.dtype)


def _norm_proj(x, gain, weights, out_scale=1.0):
    m, d = x.shape
    n = weights[0].shape[1]
    n_out = len(weights)
    tm, tn = NORM_PROJ_ROWS, PROJ_COLS_TOTAL // n_out
    assert m % tm == 0 and n % tn == 0
    outs = pl.pallas_call(
        functools.partial(_norm_proj_kernel, n_out, out_scale),
        grid=(m // tm, n // tn),
        in_specs=[pl.BlockSpec((tm, d), lambda i, j: (i, 0)),
                  pl.BlockSpec((1, d), lambda i, j: (0, 0))]
                 + [pl.BlockSpec((d, tn), lambda i, j: (0, j))] * n_out,
        out_specs=[pl.BlockSpec((tm, tn), lambda i, j: (i, j))] * n_out,
        out_shape=[jax.ShapeDtypeStruct((m, n), BF16)] * n_out,
        scratch_shapes=[pltpu.VMEM((tm, d), BF16)],
        compiler_params=_params("parallel", "arbitrary"),
        name="norm_proj",
    )(x, gain, *weights)
    return outs


def _rel_bucket(rel):
    half = NUM_BUCKETS // 2
    max_exact = half // 2
    ret = jnp.where(rel > 0, half, 0)
    n = jnp.abs(rel)
    nf = jnp.maximum(n, 1).astype(F32)
    large = max_exact + (jnp.log(nf / max_exact) / math.log(MAX_DISTANCE / max_exact)
                         * (half - max_exact)).astype(jnp.int32)
    large = jnp.minimum(large, half - 1)
    return ret + jnp.where(n < max_exact, n, large)


FAR_BUCKET = NUM_BUCKETS // 2 - 1
FAR_DISTANCE = MAX_DISTANCE


def _bias_tile_constants(t):
    assert t + 1 >= FAR_DISTANCE and t % CHUNK == 0
    q = np.arange(t, dtype=np.int32)[:, None]
    k = np.arange(t, dtype=np.int32)[None, :]
    buckets = jnp.stack([_rel_bucket(jnp.asarray(k - q)), _rel_bucket(jnp.asarray(k - t - q))])
    allowed = np.stack([(k // CHUNK) <= (q // CHUNK), np.ones((t, t), bool)])
    mask = np.where(allowed, 0.0, -np.inf).astype(np.float32)
    return buckets, jnp.asarray(mask)


def _bias_kernel(table_ref, bucket_ref, mask_ref, o_ref):
    t = bucket_ref.shape[-1]
    table = jnp.broadcast_to(table_ref[0], (t, LANES))
    for tile in range(bucket_ref.shape[0]):
        for c0 in range(0, t, LANES):
            cols = slice(c0, c0 + LANES)
            looked_up = jnp.take_along_axis(table, bucket_ref[tile, :, cols], axis=1)
            o_ref[0, tile, :, cols] = looked_up * LOG2_E + mask_ref[tile, :, cols]


def _bias_tiles(rel_bias, t):
    buckets, mask = _bias_tile_constants(t)
    n_buckets, n_maps = rel_bias.shape
    assert n_buckets <= LANES and t % LANES == 0
    table = jnp.pad(rel_bias.T, ((0, 0), (0, LANES - n_buckets))).reshape(n_maps, 1, LANES)
    return pl.pallas_call(
        _bias_kernel,
        grid=(n_maps,),
        in_specs=[
            pl.BlockSpec((1, 1, LANES), lambda m: (m, 0, 0)),
            pl.BlockSpec((2, t, t), lambda m: (0, 0, 0)),
            pl.BlockSpec((2, t, t), lambda m: (0, 0, 0)),
        ],
        out_specs=pl.BlockSpec((1, 2, t, t), lambda m: (m, 0, 0, 0)),
        out_shape=jax.ShapeDtypeStruct((n_maps, 2, t, t), F32),
        compiler_params=_params("parallel"),
        name="bias_tiles",
    )(table, buckets, mask)


def _softmax_pv(qi, t, q_ref, k_ref, v_ref, cols, far_bias, bias_ref):
    qc = q_ref[0, qi * t:(qi + 1) * t, cols]
    near_bias, diag_bias = bias_ref.at[1], bias_ref.at[0]
    pieces = []
    if qi >= 2:
        pieces.append((0, (qi - 1) * t, None))
    if qi >= 1:
        pieces.append(((qi - 1) * t, t, near_bias))
    pieces.append((qi * t, t, diag_bias))

    scores, row_max = [], []
    for start, size, bias in pieces:
        kc = k_ref[0, start:start + size, cols]
        s = lax.dot_general(qc, kc, (((1,), (1,)), ((), ())), preferred_element_type=F32)
        if bias is None:
            row_max.append(jnp.max(s, axis=-1, keepdims=True) + far_bias)
        else:
            s = s + bias[...]
            row_max.append(jnp.max(s, axis=-1, keepdims=True))
        scores.append(s)
    m = functools.reduce(jnp.maximum, row_max)
    yield
    probs = []
    for s, (_, _, bias) in zip(scores, pieces):
        probs.append(jnp.exp2(s - ((m - far_bias) if bias is None else m)))
        if bias is None:
            yield
    denom = sum(jnp.sum(p, axis=-1, keepdims=True) for p in probs)
    yield
    acc = sum(jnp.dot(p.astype(BF16), v_ref[0, start:start + size, :], preferred_element_type=F32)
              for p, (start, size, _) in zip(probs, pieces))
    return acc * (1.0 / denom)


def _run_staggered(tasks, n_phases, on_done):
    finished = set()
    for step in range(len(tasks) + n_phases - 1):
        for phase in range(n_phases):
            n = step - phase
            if 0 <= n < len(tasks) and n not in finished:
                try:
                    next(tasks[n])
                except StopIteration as done:
                    finished.add(n)
                    on_done(n, done.value)


def _attn_kernel(lambda_init, q_ref, k_ref, v_ref, bias_ref, table_ref, lam_ref, subg_ref, o_ref):
    head = pl.program_id(1)
    t = bias_ref.shape[-1]

    lam = lam_ref[...]
    lam_full = (jnp.exp(jnp.sum(lam[0:1] * lam[1:2], keepdims=True))
                - jnp.exp(jnp.sum(lam[2:3] * lam[3:4], keepdims=True)) + lambda_init)

    tiles = list(reversed(range(q_ref.shape[1] // t)))
    tasks = []
    for i in tiles:
        for c in range(2):
            cols = slice(c * HEAD_DIM, (c + 1) * HEAD_DIM)
            tasks.append(_softmax_pv(i, t, q_ref, k_ref, v_ref, cols,
                                     table_ref[FAR_BUCKET, 2 * head + c] * LOG2_E, bias_ref.at[c]))

    first_map = {}

    def on_done(n, value):
        if n % 2 == 0:
            first_map[n // 2] = value
            return
        i = tiles[n // 2]
        o = first_map.pop(n // 2) - lam_full * value
        o = o * lax.rsqrt(jnp.mean(o * o, axis=-1, keepdims=True) + SUBLN_EPS)
        o_ref[0, i * t:(i + 1) * t, :] = (o * subg_ref[...] * (1.0 - lambda_init)).astype(o_ref.dtype)

    _run_staggered(tasks, 4, on_done)


def _attention(q, k, v, bias_tiles, rel_bias, lam, subln_g, lambda_init):
    b, s, width = q.shape
    t = ATT_TILE
    n_heads = width // V_HEAD_DIM
    assert s % t == 0
    return pl.pallas_call(
        functools.partial(_attn_kernel, lambda_init),
        grid=(b, n_heads),
        in_specs=[
            pl.BlockSpec((1, s, V_HEAD_DIM), lambda bi, h: (bi, 0, h)),
            pl.BlockSpec((1, s, V_HEAD_DIM), lambda bi, h: (bi, 0, h)),
            pl.BlockSpec((1, s, V_HEAD_DIM), lambda bi, h: (bi, 0, h)),
            pl.BlockSpec((2, 2, t, t), lambda bi, h: (h, 0, 0, 0)),
            pl.BlockSpec(memory_space=pltpu.SMEM),
            pl.BlockSpec(lam.shape, lambda bi, h: (0, 0)),
            pl.BlockSpec((1, V_HEAD_DIM), lambda bi, h: (0, 0)),
        ],
        out_specs=pl.BlockSpec((1, s, V_HEAD_DIM), lambda bi, h: (bi, 0, h)),
        out_shape=jax.ShapeDtypeStruct((b, s, width), BF16),
        compiler_params=_params("parallel", "parallel"),
        name="diff_attention",
    )(q, k, v, bias_tiles, rel_bias, lam, subln_g)


def _out_proj_kernel(o_ref, w_ref, x_ref, g_ref, y_ref):
    y_ref[...] = jnp.dot(o_ref[...], w_ref[...], preferred_element_type=F32)
    gain = g_ref[...]

    def body(rows):
        y_ref[rows, :] = x_ref[rows, :] + _rms(y_ref[rows, :], gain, RMS_EPS)
    _for_row_chunks(y_ref.shape[0], body)


def _out_proj(o, w, x, gain):
    m, d = x.shape
    kdim = o.shape[1]
    tm = PROJ_ROWS
    assert m % tm == 0
    return pl.pallas_call(
        _out_proj_kernel,
        grid=(m // tm,),
        in_specs=[
            pl.BlockSpec((tm, kdim), lambda i: (i, 0)),
            pl.BlockSpec((kdim, d), lambda i: (0, 0)),
            pl.BlockSpec((tm, d), lambda i: (i, 0)),
            pl.BlockSpec((1, d), lambda i: (0, 0)),
        ],
        out_specs=pl.BlockSpec((tm, d), lambda i: (i, 0)),
        out_shape=jax.ShapeDtypeStruct((m, d), F32),
        compiler_params=_params("parallel"),
        name="out_proj",
    )(o, w, x, gain)


def kernel(x, norm_gains, ffn_w_gate, ffn_w_up, ffn_w_down, pool_w, pool_scale, kv_norm, w_k, w_v,
           rel_bias, w_q, w_o, lambdas, subln_gain):
    b, s, d = x.shape
    depth = norm_gains.shape[0]
    n_a = pool_w.shape[0]
    gains = norm_gains.reshape(depth, 3, 2, 1, d)
    wg, wu, wd = ffn_w_gate, ffn_w_up, ffn_w_down

    xf = x.reshape(b * s, d)
    k = v = bias_tiles = None
    for l in range(depth):
        g = gains[l]
        if l == n_a:
            k, v = _norm_proj(xf, kv_norm.reshape(1, d), [w_k, w_v])
            bias_tiles = _bias_tiles(rel_bias, ATT_TILE)
        xf = _ffn(xf, g[0, 0], g[0, 1], wg, wu, wd, l, 0)
        if l < n_a:
            xf = _pool_mixer(xf.reshape(b, s, d), g[1, 0], g[1, 1], pool_w[l].astype(BF16),
                             pool_scale[l].reshape(1, d)).reshape(b * s, d)
        else:
            j = l - n_a
            lambda_init = 0.8 - 0.6 * math.exp(-0.3 * l)
            (q,) = _norm_proj(xf, g[1, 0], [w_q[j]], out_scale=HEAD_DIM ** -0.5 * LOG2_E)
            o = _attention(q.reshape(b, s, -1), k.reshape(b, s, -1), v.reshape(b, s, -1), bias_tiles,
                           rel_bias, lambdas[j], subln_gain[j].reshape(1, -1), lambda_init)
            xf = _out_proj(o.reshape(b * s, -1), w_o[j].astype(BF16), xf, g[1, 1])
        xf = _ffn(xf, g[2, 0], g[2, 1], wg, wu, wd, l, 1)
    return xf.reshape(b, s, d)
```

```python
import functools
import math

import numpy as np
import jax
import jax.numpy as jnp
from jax import lax
from jax.experimental import pallas as pl
from jax.experimental.pallas import tpu as pltpu

F32 = jnp.float32
BF16 = jnp.bfloat16

CHUNK = 64
POOL_WINDOWS = (2, 4, 8, 16)
HEAD_DIM = 128
V_HEAD_DIM = 2 * HEAD_DIM
NUM_BUCKETS = 32
MAX_DISTANCE = 128
RMS_EPS = 1e-6
SUBLN_EPS = 1e-5
LOG2_E = math.log2(math.e)

LANES = 128
SUBLANES = 8

VMEM_LIMIT_BYTES = 60 * 1024 * 1024
FFN_ROWS = 1024
FFN_COLS = 256
FFN_X_PREFETCH_STEP = 6
NORM_PROJ_ROWS = 1024
PROJ_ROWS = 512
PROJ_COLS_TOTAL = 1024
NORM_ROW_CHUNK = 16
POOL_ROWS = 256
POOL_STAGES = 4
POOL_HALO = SUBLANES * POOL_STAGES
ATT_TILE = 256


def _rms(x, gain, eps):
    ms = jnp.mean(x * x, axis=-1, keepdims=True)
    return x * lax.rsqrt(ms + eps) * gain


def _for_row_chunks(n_rows, body):
    for r0 in range(0, n_rows, NORM_ROW_CHUNK):
        body(slice(r0, r0 + NORM_ROW_CHUNK))


def _norm_rows_to(dst_ref, src_ref, gain, eps):
    def body(rows):
        dst_ref[rows, :] = _rms(src_ref[rows, :], gain, eps).astype(dst_ref.dtype)
    _for_row_chunks(src_ref.shape[0], body)


def _params(*semantics):
    return pltpu.CompilerParams(dimension_semantics=semantics,
                                vmem_limit_bytes=VMEM_LIMIT_BYTES)


def _ffn_kernel(x_hbm, gpre_ref, gpost_ref, wg_ref, wu_ref, wd_ref, o_ref, h_ref, a_ref, x_buf,
                x_sem):
    i = pl.program_id(0)
    j = pl.program_id(1)
    last = pl.num_programs(1) - 1
    tm = o_ref.shape[0]
    slot = i % 2
    x_ref = x_buf.at[slot]

    def x_copy(tile, to_slot):
        rows = pl.ds(pl.multiple_of(tile * tm, tm), tm)
        return pltpu.make_async_copy(x_hbm.at[rows, :], x_buf.at[to_slot], x_sem.at[to_slot])

    def gate_up():
        h = h_ref[...]
        g = jnp.dot(h, wg_ref[...].astype(BF16), preferred_element_type=F32)
        u = jnp.dot(h, wu_ref[...].astype(BF16), preferred_element_type=F32)
        return (g * (1.0 / (1.0 + jnp.exp(-g))) * u).astype(BF16)

    def down(a_slot):
        return jnp.dot(a_ref[a_slot], wd_ref[...].astype(BF16), preferred_element_type=F32)

    @pl.when(jnp.logical_and(i == 0, j == 0))
    def _():
        x_copy(0, 0).start()

    @pl.when(j == 0)
    def _():
        x_copy(i, slot).wait()
        _norm_rows_to(h_ref, x_ref, gpre_ref[...], RMS_EPS)
        a_ref[0] = gate_up()

    @pl.when(jnp.logical_and(j == FFN_X_PREFETCH_STEP, i + 1 < pl.num_programs(0)))
    def _():
        x_copy(i + 1, 1 - slot).start()

    @pl.when(j == 1)
    def _():
        a = gate_up()
        o_ref[...] = down(0)
        a_ref[1] = a

    @pl.when(jnp.logical_and(j > 1, j < last))
    def _():
        a = gate_up()
        o_ref[...] += down((j - 1) % 2)
        a_ref[j % 2] = a

    @pl.when(j == last)
    def _():
        o_ref[...] += down((last - 1) % 2)
        half_gain = 0.5 * gpost_ref[...]

        def body(rows):
            o_ref[rows, :] = x_ref[rows, :] + _rms(o_ref[rows, :], half_gain, RMS_EPS)
        _for_row_chunks(o_ref.shape[0], body)


def _ffn(x, gpre, gpost, wg, wu, wd, layer, half):
    m, d = x.shape
    ff = wg.shape[-1]
    tm, tf = FFN_ROWS, FFN_COLS
    n_blocks = ff // tf
    assert m % tm == 0 and ff % tf == 0 and n_blocks >= 2 and 0 < FFN_X_PREFETCH_STEP < n_blocks
    gate_up_block = lambda i, j: (layer, half, 0, jnp.minimum(j, n_blocks - 1))
    down_block = lambda i, j: (layer, half, jnp.maximum(j - 1, 0), 0)
    return pl.pallas_call(
        _ffn_kernel,
        grid=(m // tm, n_blocks + 1),
        in_specs=[
            pl.BlockSpec(memory_space=pl.ANY),
            pl.BlockSpec((1, d), lambda i, j: (0, 0)),
            pl.BlockSpec((1, d), lambda i, j: (0, 0)),
            pl.BlockSpec((None, None, d, tf), gate_up_block),
            pl.BlockSpec((None, None, d, tf), gate_up_block),
            pl.BlockSpec((None, None, tf, d), down_block),
        ],
        out_specs=pl.BlockSpec((tm, d), lambda i, j: (i, 0)),
        out_shape=jax.ShapeDtypeStruct((m, d), F32),
        scratch_shapes=[pltpu.VMEM((tm, d), BF16), pltpu.VMEM((2, tm, tf), BF16),
                        pltpu.VMEM((2, tm, d), F32), pltpu.SemaphoreType.DMA((2,))],
        compiler_params=_params("arbitrary", "arbitrary"),
        name="ffn",
    )(x, gpre, gpost, wg, wu, wd)


def _pool_kernel(x_ref, halo_ref, gpre_ref, gpost_ref, w_ref, scale_ref, o_ref, *sum_refs):
    i = pl.program_id(1)
    ts = x_ref.shape[1]
    d = x_ref.shape[2]
    gdim = w_ref.shape[1]
    rows_all = POOL_HALO + ts
    hx_ref = sum_refs[0]
    gpre = gpre_ref[...]

    hh = _rms(halo_ref[0], gpre, RMS_EPS)
    hx_ref[0:POOL_HALO, :] = jnp.where(i > 0, hh, 0.0)
    for r0 in range(0, ts, NORM_ROW_CHUNK):
        hx_ref[POOL_HALO + r0:POOL_HALO + r0 + NORM_ROW_CHUNK, :] = _rms(
            x_ref[0, r0:r0 + NORM_ROW_CHUNK, :], gpre, RMS_EPS)

    for s in range(1, len(sum_refs)):
        shift = 2 ** (s - 1)
        r0 = SUBLANES * s
        c0 = (s - 1) * gdim
        prev = sum_refs[s - 1]
        sum_refs[s][r0:rows_all, c0:d] = (prev[r0:rows_all, c0:d]
                                          + prev[r0 - shift:rows_all - shift, c0:d])

    t = i * ts + lax.broadcasted_iota(jnp.int32, (ts, 1), 0)
    main = slice(POOL_HALO, rows_all)
    for gi, win in enumerate(POOL_WINDOWS):
        cols = slice(gi * gdim, (gi + 1) * gdim)
        stage = win.bit_length() - 1
        if stage < len(sum_refs):
            total = sum_refs[stage][main, cols]
        else:
            prev = sum_refs[stage - 1]
            half = win // 2
            total = prev[main, cols] + prev[POOL_HALO - half:rows_all - half, cols]
        inv_cnt = 1.0 / jnp.minimum(t + 1, win).astype(F32)
        dg = total * inv_cnt - hx_ref[main, cols]
        o_ref[0, :, cols] = (jnp.dot(dg.astype(BF16), w_ref[gi], preferred_element_type=F32)
                             * scale_ref[:, cols])

    gpost = gpost_ref[...]
    for r0 in range(0, ts, NORM_ROW_CHUNK):
        rows = slice(r0, r0 + NORM_ROW_CHUNK)
        o_ref[0, rows, :] = x_ref[0, rows, :] + _rms(o_ref[0, rows, :], gpost, RMS_EPS)


def _pool_mixer(x, gpre, gpost, w, scale):
    b, s, d = x.shape
    ts = POOL_ROWS
    assert s % ts == 0 and ts % POOL_HALO == 0
    assert POOL_WINDOWS == tuple(2 ** (g + 1) for g in range(POOL_STAGES))
    assert w.shape[0] == len(POOL_WINDOWS) and w.shape[1] * w.shape[0] == d
    halo_blocks = ts // POOL_HALO
    return pl.pallas_call(
        _pool_kernel,
        grid=(b, s // ts),
        in_specs=[
            pl.BlockSpec((1, ts, d), lambda bi, i: (bi, i, 0)),
            pl.BlockSpec((1, POOL_HALO, d),
                         lambda bi, i: (bi, jnp.maximum(i * halo_blocks - 1, 0), 0)),
            pl.BlockSpec((1, d), lambda bi, i: (0, 0)),
            pl.BlockSpec((1, d), lambda bi, i: (0, 0)),
            pl.BlockSpec(w.shape, lambda bi, i: (0, 0, 0)),
            pl.BlockSpec((1, d), lambda bi, i: (0, 0)),
        ],
        out_specs=pl.BlockSpec((1, ts, d), lambda bi, i: (bi, i, 0)),
        out_shape=jax.ShapeDtypeStruct((b, s, d), F32),
        scratch_shapes=[pltpu.VMEM((POOL_HALO + ts, d), F32)] * POOL_STAGES,
        compiler_params=_params("parallel", "parallel"),
        name="pool_mixer",
    )(x, x, gpre, gpost, w, scale)


def _norm_proj_kernel(n_out, out_scale, x_ref, g_ref, *refs):
    w_refs, o_refs, h_ref = refs[:n_out], refs[n_out:2 * n_out], refs[2 * n_out]

    @pl.when(pl.program_id(1) == 0)
    def _():
        _norm_rows_to(h_ref, x_ref, g_ref[...], RMS_EPS)

    h = h_ref[...]
    for w_ref, o_ref in zip(w_refs, o_refs):
        y = jnp.dot(h, w_ref[...].astype(BF16), preferred_element_type=F32)
        if out_scale != 1.0:
            y = y * out_scale
        o_ref[...] = y.astype(o_ref.dtype)


def _norm_proj(x, gain, weights, out_scale=1.0):
    m, d = x.shape
    n = weights[0].shape[1]
    n_out = len(weights)
    tm, tn = NORM_PROJ_ROWS, PROJ_COLS_TOTAL // n_out
    assert m % tm == 0 and n % tn == 0
    outs = pl.pallas_call(
        functools.partial(_norm_proj_kernel, n_out, out_scale),
        grid=(m // tm, n // tn),
        in_specs=[pl.BlockSpec((tm, d), lambda i, j: (i, 0)),
                  pl.BlockSpec((1, d), lambda i, j: (0, 0))]
                 + [pl.BlockSpec((d, tn), lambda i, j: (0, j))] * n_out,
        out_specs=[pl.BlockSpec((tm, tn), lambda i, j: (i, j))] * n_out,
        out_shape=[jax.ShapeDtypeStruct((m, n), BF16)] * n_out,
        scratch_shapes=[pltpu.VMEM((tm, d), BF16)],
        compiler_params=_params("parallel", "arbitrary"),
        name="norm_proj",
    )(x, gain, *weights)
    return outs


def _rel_bucket(rel):
    half = NUM_BUCKETS // 2
    max_exact = half // 2
    ret = jnp.where(rel > 0, half, 0)
    n = jnp.abs(rel)
    nf = jnp.maximum(n, 1).astype(F32)
    large = max_exact + (jnp.log(nf / max_exact) / math.log(MAX_DISTANCE / max_exact)
                         * (half - max_exact)).astype(jnp.int32)
    large = jnp.minimum(large, half - 1)
    return ret + jnp.where(n < max_exact, n, large)


FAR_BUCKET = NUM_BUCKETS // 2 - 1
FAR_DISTANCE = MAX_DISTANCE


def _bias_tile_constants(t):
    assert t + 1 >= FAR_DISTANCE and t % CHUNK == 0
    q = np.arange(t, dtype=np.int32)[:, None]
    k = np.arange(t, dtype=np.int32)[None, :]
    buckets = jnp.stack([_rel_bucket(jnp.asarray(k - q)), _rel_bucket(jnp.asarray(k - t - q))])
    allowed = np.stack([(k // CHUNK) <= (q // CHUNK), np.ones((t, t), bool)])
    mask = np.where(allowed, 0.0, -np.inf).astype(np.float32)
    return buckets, jnp.asarray(mask)


def _bias_kernel(table_ref, bucket_ref, mask_ref, o_ref):
    t = bucket_ref.shape[-1]
    table = jnp.broadcast_to(table_ref[0], (t, LANES))
    for tile in range(bucket_ref.shape[0]):
        for c0 in range(0, t, LANES):
            cols = slice(c0, c0 + LANES)
            looked_up = jnp.take_along_axis(table, bucket_ref[tile, :, cols], axis=1)
            o_ref[0, tile, :, cols] = looked_up * LOG2_E + mask_ref[tile, :, cols]


def _bias_tiles(rel_bias, t):
    buckets, mask = _bias_tile_constants(t)
    n_buckets, n_maps = rel_bias.shape
    assert n_buckets <= LANES and t % LANES == 0
    table = jnp.pad(rel_bias.T, ((0, 0), (0, LANES - n_buckets))).reshape(n_maps, 1, LANES)
    return pl.pallas_call(
        _bias_kernel,
        grid=(n_maps,),
        in_specs=[
            pl.BlockSpec((1, 1, LANES), lambda m: (m, 0, 0)),
            pl.BlockSpec((2, t, t), lambda m: (0, 0, 0)),
            pl.BlockSpec((2, t, t), lambda m: (0, 0, 0)),
        ],
        out_specs=pl.BlockSpec((1, 2, t, t), lambda m: (m, 0, 0, 0)),
        out_shape=jax.ShapeDtypeStruct((n_maps, 2, t, t), F32),
        compiler_params=_params("parallel"),
        name="bias_tiles",
    )(table, buckets, mask)


def _softmax_pv(qi, t, q_ref, k_ref, v_ref, cols, far_bias, bias_ref):
    qc = q_ref[0, qi * t:(qi + 1) * t, cols]
    near_bias, diag_bias = bias_ref.at[1], bias_ref.at[0]
    pieces = []
    if qi >= 2:
        pieces.append((0, (qi - 1) * t, None))
    if qi >= 1:
        pieces.append(((qi - 1) * t, t, near_bias))
    pieces.append((qi * t, t, diag_bias))

    scores, row_max = [], []
    for start, size, bias in pieces:
        kc = k_ref[0, start:start + size, cols]
        s = lax.dot_general(qc, kc, (((1,), (1,)), ((), ())), preferred_element_type=F32)
        if bias is None:
            row_max.append(jnp.max(s, axis=-1, keepdims=True) + far_bias)
        else:
            s = s + bias[...]
            row_max.append(jnp.max(s, axis=-1, keepdims=True))
        scores.append(s)
    m = functools.reduce(jnp.maximum, row_max)
    yield
    probs = []
    for s, (_, _, bias) in zip(scores, pieces):
        probs.append(jnp.exp2(s - ((m - far_bias) if bias is None else m)))
        if bias is None:
            yield
    denom = sum(jnp.sum(p, axis=-1, keepdims=True) for p in probs)
    yield
    acc = sum(jnp.dot(p.astype(BF16), v_ref[0, start:start + size, :], preferred_element_type=F32)
              for p, (start, size, _) in zip(probs, pieces))
    return acc * (1.0 / denom)


def _run_staggered(tasks, n_phases, on_done):
    finished = set()
    for step in range(len(tasks) + n_phases - 1):
        for phase in range(n_phases):
            n = step - phase
            if 0 <= n < len(tasks) and n not in finished:
                try:
                    next(tasks[n])
                except StopIteration as done:
                    finished.add(n)
                    on_done(n, done.value)


def _attn_kernel(lambda_init, q_ref, k_ref, v_ref, bias_ref, table_ref, lam_ref, subg_ref, o_ref):
    head = pl.program_id(1)
    t = bias_ref.shape[-1]

    lam = lam_ref[...]
    lam_full = (jnp.exp(jnp.sum(lam[0:1] * lam[1:2], keepdims=True))
                - jnp.exp(jnp.sum(lam[2:3] * lam[3:4], keepdims=True)) + lambda_init)

    tiles = list(reversed(range(q_ref.shape[1] // t)))
    tasks = []
    for i in tiles:
        for c in range(2):
            cols = slice(c * HEAD_DIM, (c + 1) * HEAD_DIM)
            tasks.append(_softmax_pv(i, t, q_ref, k_ref, v_ref, cols,
                                     table_ref[FAR_BUCKET, 2 * head + c] * LOG2_E, bias_ref.at[c]))

    first_map = {}

    def on_done(n, value):
        if n % 2 == 0:
            first_map[n // 2] = value
            return
        i = tiles[n // 2]
        o = first_map.pop(n // 2) - lam_full * value
        o = o * lax.rsqrt(jnp.mean(o * o, axis=-1, keepdims=True) + SUBLN_EPS)
        o_ref[0, i * t:(i + 1) * t, :] = (o * subg_ref[...] * (1.0 - lambda_init)).astype(o_ref.dtype)

    _run_staggered(tasks, 4, on_done)


def _attention(q, k, v, bias_tiles, rel_bias, lam, subln_g, lambda_init):
    b, s, width = q.shape
    t = ATT_TILE
    n_heads = width // V_HEAD_DIM
    assert s % t == 0
    return pl.pallas_call(
        functools.partial(_attn_kernel, lambda_init),
        grid=(b, n_heads),
        in_specs=[
            pl.BlockSpec((1, s, V_HEAD_DIM), lambda bi, h: (bi, 0, h)),
            pl.BlockSpec((1, s, V_HEAD_DIM), lambda bi, h: (bi, 0, h)),
            pl.BlockSpec((1, s, V_HEAD_DIM), lambda bi, h: (bi, 0, h)),
            pl.BlockSpec((2, 2, t, t), lambda bi, h: (h, 0, 0, 0)),
            pl.BlockSpec(memory_space=pltpu.SMEM),
            pl.BlockSpec(lam.shape, lambda bi, h: (0, 0)),
            pl.BlockSpec((1, V_HEAD_DIM), lambda bi, h: (0, 0)),
        ],
        out_specs=pl.BlockSpec((1, s, V_HEAD_DIM), lambda bi, h: (bi, 0, h)),
        out_shape=jax.ShapeDtypeStruct((b, s, width), BF16),
        compiler_params=_params("parallel", "parallel"),
        name="diff_attention",
    )(q, k, v, bias_tiles, rel_bias, lam, subln_g)


def _out_proj_kernel(o_ref, w_ref, x_ref, g_ref, y_ref):
    y_ref[...] = jnp.dot(o_ref[...], w_ref[...], preferred_element_type=F32)
    gain = g_ref[...]

    def body(rows):
        y_ref[rows, :] = x_ref[rows, :] + _rms(y_ref[rows, :], gain, RMS_EPS)
    _for_row_chunks(y_ref.shape[0], body)


def _out_proj(o, w, x, gain):
    m, d = x.shape
    kdim = o.shape[1]
    tm = PROJ_ROWS
    assert m % tm == 0
    return pl.pallas_call(
        _out_proj_kernel,
        grid=(m // tm,),
        in_specs=[
            pl.BlockSpec((tm, kdim), lambda i: (i, 0)),
            pl.BlockSpec((kdim, d), lambda i: (0, 0)),
            pl.BlockSpec((tm, d), lambda i: (i, 0)),
            pl.BlockSpec((1, d), lambda i: (0, 0)),
        ],
        out_specs=pl.BlockSpec((tm, d), lambda i: (i, 0)),
        out_shape=jax.ShapeDtypeStruct((m, d), F32),
        compiler_params=_params("parallel"),
        name="out_proj",
    )(o, w, x, gain)


def kernel(x, norm_gains, ffn_w_gate, ffn_w_up, ffn_w_down, pool_w, pool_scale, kv_norm, w_k, w_v,
           rel_bias, w_q, w_o, lambdas, subln_gain):
    b, s, d = x.shape
    depth = norm_gains.shape[0]
    n_a = pool_w.shape[0]
    gains = norm_gains.reshape(depth, 3, 2, 1, d)
    wg, wu, wd = ffn_w_gate, ffn_w_up, ffn_w_down

    xf = x.reshape(b * s, d)
    k = v = bias_tiles = None
    for l in range(depth):
        g = gains[l]
        if l == n_a:
            k, v = _norm_proj(xf, kv_norm.reshape(1, d), [w_k, w_v])
            bias_tiles = _bias_tiles(rel_bias, ATT_TILE)
        xf = _ffn(xf, g[0, 0], g[0, 1], wg, wu, wd, l, 0)
        if l < n_a:
            xf = _pool_mixer(xf.reshape(b, s, d), g[1, 0], g[1, 1], pool_w[l].astype(BF16),
                             pool_scale[l].reshape(1, d)).reshape(b * s, d)
        else:
            j = l - n_a
            lambda_init = 0.8 - 0.6 * math.exp(-0.3 * l)
            (q,) = _norm_proj(xf, g[1, 0], [w_q[j]], out_scale=HEAD_DIM ** -0.5 * LOG2_E)
            o = _attention(q.reshape(b, s, -1), k.reshape(b, s, -1), v.reshape(b, s, -1), bias_tiles,
                           rel_bias, lambdas[j], subln_gain[j].reshape(1, -1), lambda_init)
            xf = _out_proj(o.reshape(b * s, -1), w_o[j].astype(BF16), xf, g[1, 1])
        xf = _ffn(xf, g[2, 0], g[2, 1], wg, wu, wd, l, 1)
    return xf.reshape(b, s, d)
```

```python
import functools
import math

import numpy as np
import jax
import jax.numpy as jnp
from jax import lax
from jax.experimental import pallas as pl
from jax.experimental.pallas import tpu as pltpu

F32 = jnp.float32
BF16 = jnp.bfloat16

CHUNK = 64
POOL_WINDOWS = (2, 4, 8, 16)
HEAD_DIM = 128
V_HEAD_DIM = 2 * HEAD_DIM
NUM_BUCKETS = 32
MAX_DISTANCE = 128
RMS_EPS = 1e-6
SUBLN_EPS = 1e-5
LOG2_E = math.log2(math.e)

LANES = 128
SUBLANES = 8

VMEM_LIMIT_BYTES = 60 * 1024 * 1024
FFN_ROWS = 1024
FFN_COLS = 256
FFN_X_PREFETCH_STEP = 6
NORM_PROJ_ROWS = 1024
PROJ_ROWS = 512
PROJ_COLS_TOTAL = 1024
NORM_ROW_CHUNK = 16
POOL_ROWS = 512
POOL_STAGES = 4
POOL_HALO = SUBLANES * POOL_STAGES
ATT_TILE = 256


def _rms(x, gain, eps):
    ms = jnp.mean(x * x, axis=-1, keepdims=True)
    return x * lax.rsqrt(ms + eps) * gain


def _for_row_chunks(n_rows, body):
    for r0 in range(0, n_rows, NORM_ROW_CHUNK):
        body(slice(r0, r0 + NORM_ROW_CHUNK))


def _norm_rows_to(dst_ref, src_ref, gain, eps):
    def body(rows):
        dst_ref[rows, :] = _rms(src_ref[rows, :], gain, eps).astype(dst_ref.dtype)
    _for_row_chunks(src_ref.shape[0], body)


def _params(*semantics):
    return pltpu.CompilerParams(dimension_semantics=semantics,
                                vmem_limit_bytes=VMEM_LIMIT_BYTES)


def _ffn_kernel(x_hbm, gpre_ref, gpost_ref, wg_ref, wu_ref, wd_ref, o_ref, h_ref, x_buf, x_sem):
    i = pl.program_id(0)
    j = pl.program_id(1)
    tm = o_ref.shape[0]
    slot = i % 2
    x_ref = x_buf.at[slot]

    def x_copy(tile, to_slot):
        rows = pl.ds(pl.multiple_of(tile * tm, tm), tm)
        return pltpu.make_async_copy(x_hbm.at[rows, :], x_buf.at[to_slot], x_sem.at[to_slot])

    def swiglu_block(first):
        h = h_ref[...]
        g = jnp.dot(h, wg_ref[...].astype(BF16), preferred_element_type=F32)
        u = jnp.dot(h, wu_ref[...].astype(BF16), preferred_element_type=F32)
        a = (g * (1.0 / (1.0 + jnp.exp(-g))) * u).astype(BF16)
        d = jnp.dot(a, wd_ref[...].astype(BF16), preferred_element_type=F32)
        if first:
            o_ref[...] = d
        else:
            o_ref[...] += d

    @pl.when(jnp.logical_and(i == 0, j == 0))
    def _():
        x_copy(0, 0).start()

    @pl.when(j == 0)
    def _():
        x_copy(i, slot).wait()
        _norm_rows_to(h_ref, x_ref, gpre_ref[...], RMS_EPS)
        swiglu_block(first=True)

    @pl.when(jnp.logical_and(j == FFN_X_PREFETCH_STEP, i + 1 < pl.num_programs(0)))
    def _():
        x_copy(i + 1, 1 - slot).start()

    @pl.when(j > 0)
    def _():
        swiglu_block(first=False)

    @pl.when(j == pl.num_programs(1) - 1)
    def _():
        half_gain = 0.5 * gpost_ref[...]

        def body(rows):
            o_ref[rows, :] = x_ref[rows, :] + _rms(o_ref[rows, :], half_gain, RMS_EPS)
        _for_row_chunks(o_ref.shape[0], body)


def _ffn(x, gpre, gpost, wg, wu, wd, layer, half):
    m, d = x.shape
    ff = wg.shape[-1]
    tm, tf = FFN_ROWS, FFN_COLS
    assert m % tm == 0 and ff % tf == 0 and 0 < FFN_X_PREFETCH_STEP < ff // tf
    return pl.pallas_call(
        _ffn_kernel,
        grid=(m // tm, ff // tf),
        in_specs=[
            pl.BlockSpec(memory_space=pl.ANY),
            pl.BlockSpec((1, d), lambda i, j: (0, 0)),
            pl.BlockSpec((1, d), lambda i, j: (0, 0)),
            pl.BlockSpec((None, None, d, tf), lambda i, j: (layer, half, 0, j)),
            pl.BlockSpec((None, None, d, tf), lambda i, j: (layer, half, 0, j)),
            pl.BlockSpec((None, None, tf, d), lambda i, j: (layer, half, j, 0)),
        ],
        out_specs=pl.BlockSpec((tm, d), lambda i, j: (i, 0)),
        out_shape=jax.ShapeDtypeStruct((m, d), F32),
        scratch_shapes=[pltpu.VMEM((tm, d), BF16), pltpu.VMEM((2, tm, d), F32),
                        pltpu.SemaphoreType.DMA((2,))],
        compiler_params=_params("arbitrary", "arbitrary"),
        name="ffn",
    )(x, gpre, gpost, wg, wu, wd)


def _pool_kernel(x_ref, halo_ref, gpre_ref, gpost_ref, w_ref, scale_ref, o_ref, *sum_refs):
    i = pl.program_id(1)
    ts = x_ref.shape[1]
    d = x_ref.shape[2]
    gdim = w_ref.shape[1]
    rows_all = POOL_HALO + ts
    hx_ref = sum_refs[0]
    gpre = gpre_ref[...]

    hh = _rms(halo_ref[0], gpre, RMS_EPS)
    hx_ref[0:POOL_HALO, :] = jnp.where(i > 0, hh, 0.0)
    for r0 in range(0, ts, NORM_ROW_CHUNK):
        hx_ref[POOL_HALO + r0:POOL_HALO + r0 + NORM_ROW_CHUNK, :] = _rms(
            x_ref[0, r0:r0 + NORM_ROW_CHUNK, :], gpre, RMS_EPS)

    for s in range(1, len(sum_refs)):
        shift = 2 ** (s - 1)
        r0 = SUBLANES * s
        c0 = (s - 1) * gdim
        prev = sum_refs[s - 1]
        sum_refs[s][r0:rows_all, c0:d] = (prev[r0:rows_all, c0:d]
                                          + prev[r0 - shift:rows_all - shift, c0:d])

    t = i * ts + lax.broadcasted_iota(jnp.int32, (ts, 1), 0)
    main = slice(POOL_HALO, rows_all)
    for gi, win in enumerate(POOL_WINDOWS):
        cols = slice(gi * gdim, (gi + 1) * gdim)
        stage = win.bit_length() - 1
        if stage < len(sum_refs):
            total = sum_refs[stage][main, cols]
        else:
            prev = sum_refs[stage - 1]
            half = win // 2
            total = prev[main, cols] + prev[POOL_HALO - half:rows_all - half, cols]
        inv_cnt = 1.0 / jnp.minimum(t + 1, win).astype(F32)
        dg = total * inv_cnt - hx_ref[main, cols]
        o_ref[0, :, cols] = (jnp.dot(dg.astype(BF16), w_ref[gi], preferred_element_type=F32)
                             * scale_ref[:, cols])

    gpost = gpost_ref[...]
    for r0 in range(0, ts, NORM_ROW_CHUNK):
        rows = slice(r0, r0 + NORM_ROW_CHUNK)
        o_ref[0, rows, :] = x_ref[0, rows, :] + _rms(o_ref[0, rows, :], gpost, RMS_EPS)


def _pool_mixer(x, gpre, gpost, w, scale):
    b, s, d = x.shape
    ts = POOL_ROWS
    assert s % ts == 0 and ts % POOL_HALO == 0
    assert POOL_WINDOWS == tuple(2 ** (g + 1) for g in range(POOL_STAGES))
    assert w.shape[0] == len(POOL_WINDOWS) and w.shape[1] * w.shape[0] == d
    halo_blocks = ts // POOL_HALO
    return pl.pallas_call(
        _pool_kernel,
        grid=(b, s // ts),
        in_specs=[
            pl.BlockSpec((1, ts, d), lambda bi, i: (bi, i, 0)),
            pl.BlockSpec((1, POOL_HALO, d),
                         lambda bi, i: (bi, jnp.maximum(i * halo_blocks - 1, 0), 0)),
            pl.BlockSpec((1, d), lambda bi, i: (0, 0)),
            pl.BlockSpec((1, d), lambda bi, i: (0, 0)),
            pl.BlockSpec(w.shape, lambda bi, i: (0, 0, 0)),
            pl.BlockSpec((1, d), lambda bi, i: (0, 0)),
        ],
        out_specs=pl.BlockSpec((1, ts, d), lambda bi, i: (bi, i, 0)),
        out_shape=jax.ShapeDtypeStruct((b, s, d), F32),
        scratch_shapes=[pltpu.VMEM((POOL_HALO + ts, d), F32)] * POOL_STAGES,
        compiler_params=_params("parallel", "parallel"),
        name="pool_mixer",
    )(x, x, gpre, gpost, w, scale)


def _norm_proj_kernel(n_out, out_scale, x_ref, g_ref, *refs):
    w_refs, o_refs, h_ref = refs[:n_out], refs[n_out:2 * n_out], refs[2 * n_out]

    def project():
        h = h_ref[...]
        for w_ref, o_ref in zip(w_refs, o_refs):
            y = jnp.dot(h, w_ref[...].astype(BF16), preferred_element_type=F32)
            if out_scale != 1.0:
                y = y * out_scale
            o_ref[...] = y.astype(o_ref.dtype)

    @pl.when(pl.program_id(1) == 0)
    def _():
        _norm_rows_to(h_ref, x_ref, g_ref[...], RMS_EPS)
        project()

    @pl.when(pl.program_id(1) > 0)
    def _():
        project()


def _norm_proj(x, gain, weights, out_scale=1.0):
    m, d = x.shape
    n = weights[0].shape[1]
    n_out = len(weights)
    tm, tn = NORM_PROJ_ROWS, PROJ_COLS_TOTAL // n_out
    assert m % tm == 0 and n % tn == 0
    outs = pl.pallas_call(
        functools.partial(_norm_proj_kernel, n_out, out_scale),
        grid=(m // tm, n // tn),
        in_specs=[pl.BlockSpec((tm, d), lambda i, j: (i, 0)),
                  pl.BlockSpec((1, d), lambda i, j: (0, 0))]
                 + [pl.BlockSpec((d, tn), lambda i, j: (0, j))] * n_out,
        out_specs=[pl.BlockSpec((tm, tn), lambda i, j: (i, j))] * n_out,
        out_shape=[jax.ShapeDtypeStruct((m, n), BF16)] * n_out,
        scratch_shapes=[pltpu.VMEM((tm, d), BF16)],
        compiler_params=_params("parallel", "arbitrary"),
        name="norm_proj",
    )(x, gain, *weights)
    return outs


def _rel_bucket(rel):
    half = NUM_BUCKETS // 2
    max_exact = half // 2
    ret = jnp.where(rel > 0, half, 0)
    n = jnp.abs(rel)
    nf = jnp.maximum(n, 1).astype(F32)
    large = max_exact + (jnp.log(nf / max_exact) / math.log(MAX_DISTANCE / max_exact)
                         * (half - max_exact)).astype(jnp.int32)
    large = jnp.minimum(large, half - 1)
    return ret + jnp.where(n < max_exact, n, large)


FAR_BUCKET = NUM_BUCKETS // 2 - 1
FAR_DISTANCE = MAX_DISTANCE


def _bias_tile_constants(t):
    assert t + 1 >= FAR_DISTANCE and t % CHUNK == 0
    q = np.arange(t, dtype=np.int32)[:, None]
    k = np.arange(t, dtype=np.int32)[None, :]
    buckets = jnp.stack([_rel_bucket(jnp.asarray(k - q)), _rel_bucket(jnp.asarray(k - t - q))])
    allowed = np.stack([(k // CHUNK) <= (q // CHUNK), np.ones((t, t), bool)])
    mask = np.where(allowed, 0.0, -np.inf).astype(np.float32)
    return buckets, jnp.asarray(mask)


def _bias_kernel(table_ref, bucket_ref, mask_ref, o_ref):
    t = bucket_ref.shape[-1]
    table = jnp.broadcast_to(table_ref[0], (t, LANES))
    for tile in range(bucket_ref.shape[0]):
        for c0 in range(0, t, LANES):
            cols = slice(c0, c0 + LANES)
            looked_up = jnp.take_along_axis(table, bucket_ref[tile, :, cols], axis=1)
            o_ref[0, tile, :, cols] = looked_up * LOG2_E + mask_ref[tile, :, cols]


def _bias_tiles(rel_bias, t):
    buckets, mask = _bias_tile_constants(t)
    n_buckets, n_maps = rel_bias.shape
    assert n_buckets <= LANES and t % LANES == 0
    table = jnp.pad(rel_bias.T, ((0, 0), (0, LANES - n_buckets))).reshape(n_maps, 1, LANES)
    return pl.pallas_call(
        _bias_kernel,
        grid=(n_maps,),
        in_specs=[
            pl.BlockSpec((1, 1, LANES), lambda m: (m, 0, 0)),
            pl.BlockSpec((2, t, t), lambda m: (0, 0, 0)),
            pl.BlockSpec((2, t, t), lambda m: (0, 0, 0)),
        ],
        out_specs=pl.BlockSpec((1, 2, t, t), lambda m: (m, 0, 0, 0)),
        out_shape=jax.ShapeDtypeStruct((n_maps, 2, t, t), F32),
        compiler_params=_params("parallel"),
        name="bias_tiles",
    )(table, buckets, mask)


def _softmax_pv(qi, t, q_ref, k_ref, v_ref, cols, far_bias, bias_ref):
    qc = q_ref[0, qi * t:(qi + 1) * t, cols]
    near_bias, diag_bias = bias_ref.at[1], bias_ref.at[0]
    pieces = []
    if qi >= 2:
        pieces.append((0, (qi - 1) * t, None))
    if qi >= 1:
        pieces.append(((qi - 1) * t, t, near_bias))
    pieces.append((qi * t, t, diag_bias))

    scores, row_max = [], []
    for start, size, bias in pieces:
        kc = k_ref[0, start:start + size, cols]
        s = lax.dot_general(qc, kc, (((1,), (1,)), ((), ())), preferred_element_type=F32)
        if bias is None:
            row_max.append(jnp.max(s, axis=-1, keepdims=True) + far_bias)
        else:
            s = s + bias[...]
            row_max.append(jnp.max(s, axis=-1, keepdims=True))
        scores.append(s)
    m = functools.reduce(jnp.maximum, row_max)
    yield
    probs = []
    for s, (_, _, bias) in zip(scores, pieces):
        probs.append(jnp.exp2(s - ((m - far_bias) if bias is None else m)))
        if bias is None:
            yield
    denom = sum(jnp.sum(p, axis=-1, keepdims=True) for p in probs)
    yield
    acc = sum(jnp.dot(p.astype(BF16), v_ref[0, start:start + size, :], preferred_element_type=F32)
              for p, (start, size, _) in zip(probs, pieces))
    return acc * (1.0 / denom)


def _run_staggered(tasks, n_phases, on_done):
    finished = set()
    for step in range(len(tasks) + n_phases - 1):
        for phase in range(n_phases):
            n = step - phase
            if 0 <= n < len(tasks) and n not in finished:
                try:
                    next(tasks[n])
                except StopIteration as done:
                    finished.add(n)
                    on_done(n, done.value)


def _attn_kernel(lambda_init, q_ref, k_ref, v_ref, bias_ref, table_ref, lam_ref, subg_ref, o_ref):
    head = pl.program_id(1)
    t = bias_ref.shape[-1]

    lam = lam_ref[...]
    lam_full = (jnp.exp(jnp.sum(lam[0:1] * lam[1:2], keepdims=True))
                - jnp.exp(jnp.sum(lam[2:3] * lam[3:4], keepdims=True)) + lambda_init)

    tiles = list(reversed(range(q_ref.shape[1] // t)))
    tasks = []
    for i in tiles:
        for c in range(2):
            cols = slice(c * HEAD_DIM, (c + 1) * HEAD_DIM)
            tasks.append(_softmax_pv(i, t, q_ref, k_ref, v_ref, cols,
                                     table_ref[FAR_BUCKET, 2 * head + c] * LOG2_E, bias_ref.at[c]))

    first_map = {}

    def on_done(n, value):
        if n % 2 == 0:
            first_map[n // 2] = value
            return
        i = tiles[n // 2]
        o = first_map.pop(n // 2) - lam_full * value
        o = o * lax.rsqrt(jnp.mean(o * o, axis=-1, keepdims=True) + SUBLN_EPS)
        o_ref[0, i * t:(i + 1) * t, :] = (o * subg_ref[...] * (1.0 - lambda_init)).astype(o_ref.dtype)

    _run_staggered(tasks, 4, on_done)


def _attention(q, k, v, bias_tiles, rel_bias, lam, subln_g, lambda_init):
    b, s, width = q.shape
    t = ATT_TILE
    n_heads = width // V_HEAD_DIM
    assert s % t == 0
    return pl.pallas_call(
        functools.partial(_attn_kernel, lambda_init),
        grid=(b, n_heads),
        in_specs=[
            pl.BlockSpec((1, s, V_HEAD_DIM), lambda bi, h: (bi, 0, h)),
            pl.BlockSpec((1, s, V_HEAD_DIM), lambda bi, h: (bi, 0, h)),
            pl.BlockSpec((1, s, V_HEAD_DIM), lambda bi, h: (bi, 0, h)),
            pl.BlockSpec((2, 2, t, t), lambda bi, h: (h, 0, 0, 0)),
            pl.BlockSpec(memory_space=pltpu.SMEM),
            pl.BlockSpec(lam.shape, lambda bi, h: (0, 0)),
            pl.BlockSpec((1, V_HEAD_DIM), lambda bi, h: (0, 0)),
        ],
        out_specs=pl.BlockSpec((1, s, V_HEAD_DIM), lambda bi, h: (bi, 0, h)),
        out_shape=jax.ShapeDtypeStruct((b, s, width), BF16),
        compiler_params=_params("parallel", "parallel"),
        name="diff_attention",
    )(q, k, v, bias_tiles, rel_bias, lam, subln_g)


def _out_proj_kernel(o_ref, w_ref, x_ref, g_ref, y_ref):
    y_ref[...] = jnp.dot(o_ref[...], w_ref[...], preferred_element_type=F32)
    gain = g_ref[...]

    def body(rows):
        y_ref[rows, :] = x_ref[rows, :] + _rms(y_ref[rows, :], gain, RMS_EPS)
    _for_row_chunks(y_ref.shape[0], body)


def _out_proj(o, w, x, gain):
    m, d = x.shape
    kdim = o.shape[1]
    tm = PROJ_ROWS
    assert m % tm == 0
    return pl.pallas_call(
        _out_proj_kernel,
        grid=(m // tm,),
        in_specs=[
            pl.BlockSpec((tm, kdim), lambda i: (i, 0)),
            pl.BlockSpec((kdim, d), lambda i: (0, 0)),
            pl.BlockSpec((tm, d), lambda i: (i, 0)),
            pl.BlockSpec((1, d), lambda i: (0, 0)),
        ],
        out_specs=pl.BlockSpec((tm, d), lambda i: (i, 0)),
        out_shape=jax.ShapeDtypeStruct((m, d), F32),
        compiler_params=_params("parallel"),
        name="out_proj",
    )(o, w, x, gain)


def kernel(x, norm_gains, ffn_w_gate, ffn_w_up, ffn_w_down, pool_w, pool_scale, kv_norm, w_k, w_v,
           rel_bias, w_q, w_o, lambdas, subln_gain):
    b, s, d = x.shape
    depth = norm_gains.shape[0]
    n_a = pool_w.shape[0]
    gains = norm_gains.reshape(depth, 3, 2, 1, d)
    wg, wu, wd = ffn_w_gate, ffn_w_up, ffn_w_down

    xf = x.reshape(b * s, d)
    k = v = bias_tiles = None
    for l in range(depth):
        g = gains[l]
        if l == n_a:
            k, v = _norm_proj(xf, kv_norm.reshape(1, d), [w_k, w_v])
            bias_tiles = _bias_tiles(rel_bias, ATT_TILE)
        xf = _ffn(xf, g[0, 0], g[0, 1], wg, wu, wd, l, 0)
        if l < n_a:
            xf = _pool_mixer(xf.reshape(b, s, d), g[1, 0], g[1, 1], pool_w[l].astype(BF16),
                             pool_scale[l].reshape(1, d)).reshape(b * s, d)
        else:
            j = l - n_a
            lambda_init = 0.8 - 0.6 * math.exp(-0.3 * l)
            (q,) = _norm_proj(xf, g[1, 0], [w_q[j]], out_scale=HEAD_DIM ** -0.5 * LOG2_E)
            o = _attention(q.reshape(b, s, -1), k.reshape(b, s, -1), v.reshape(b, s, -1), bias_tiles,
                           rel_bias, lambdas[j], subln_gain[j].reshape(1, -1), lambda_init)
            xf = _out_proj(o.reshape(b * s, -1), w_o[j].astype(BF16), xf, g[1, 1])
        xf = _ffn(xf, g[2, 0], g[2, 1], wg, wu, wd, l, 1)
    return xf.reshape(b, s, d)
```

```python
import functools
import math

import numpy as np
import jax
import jax.numpy as jnp
from jax import lax
from jax.experimental import pallas as pl
from jax.experimental.pallas import tpu as pltpu

F32 = jnp.float32
BF16 = jnp.bfloat16

CHUNK = 64
POOL_WINDOWS = (2, 4, 8, 16)
HEAD_DIM = 128
V_HEAD_DIM = 2 * HEAD_DIM
NUM_BUCKETS = 32
MAX_DISTANCE = 128
RMS_EPS = 1e-6
SUBLN_EPS = 1e-5
LOG2_E = math.log2(math.e)

LANES = 128
SUBLANES = 8

VMEM_LIMIT_BYTES = 60 * 1024 * 1024
FFN_ROWS = 1024
FFN_COLS = 256
FFN_X_PREFETCH_STEP = 6
NORM_PROJ_ROWS = 1024
PROJ_ROWS = 512
PROJ_COLS_TOTAL = 1024
NORM_ROW_CHUNK = 16
POOL_ROWS = 512
POOL_STAGES = 4
POOL_HALO = SUBLANES * POOL_STAGES
ATT_TILE = 256


def _rms(x, gain, eps):
    ms = jnp.mean(x * x, axis=-1, keepdims=True)
    return x * lax.rsqrt(ms + eps) * gain


def _for_row_chunks(n_rows, body):
    for r0 in range(0, n_rows, NORM_ROW_CHUNK):
        body(slice(r0, r0 + NORM_ROW_CHUNK))


def _norm_rows_to(dst_ref, src_ref, gain, eps):
    def body(rows):
        dst_ref[rows, :] = _rms(src_ref[rows, :], gain, eps).astype(dst_ref.dtype)
    _for_row_chunks(src_ref.shape[0], body)


def _params(*semantics):
    return pltpu.CompilerParams(dimension_semantics=semantics,
                                vmem_limit_bytes=VMEM_LIMIT_BYTES)


def _ffn_kernel(x_hbm, gpre_ref, gpost_ref, wg_ref, wu_ref, wd_ref, o_ref, h_ref, x_buf, x_sem):
    i = pl.program_id(0)
    j = pl.program_id(1)
    tm = o_ref.shape[0]
    slot = i % 2
    x_ref = x_buf.at[slot]

    def x_copy(tile, to_slot):
        rows = pl.ds(pl.multiple_of(tile * tm, tm), tm)
        return pltpu.make_async_copy(x_hbm.at[rows, :], x_buf.at[to_slot], x_sem.at[to_slot])

    def swiglu_block(first):
        h = h_ref[...]
        g = jnp.dot(h, wg_ref[...].astype(BF16), preferred_element_type=F32)
        u = jnp.dot(h, wu_ref[...].astype(BF16), preferred_element_type=F32)
        a = (g * (1.0 / (1.0 + jnp.exp(-g))) * u).astype(BF16)
        d = jnp.dot(a, wd_ref[...].astype(BF16), preferred_element_type=F32)
        if first:
            o_ref[...] = d
        else:
            o_ref[...] += d

    @pl.when(jnp.logical_and(i == 0, j == 0))
    def _():
        x_copy(0, 0).start()

    @pl.when(j == 0)
    def _():
        x_copy(i, slot).wait()
        _norm_rows_to(h_ref, x_ref, gpre_ref[...], RMS_EPS)
        swiglu_block(first=True)

    @pl.when(jnp.logical_and(j == FFN_X_PREFETCH_STEP, i + 1 < pl.num_programs(0)))
    def _():
        x_copy(i + 1, 1 - slot).start()

    @pl.when(j > 0)
    def _():
        swiglu_block(first=False)

    @pl.when(j == pl.num_programs(1) - 1)
    def _():
        half_gain = 0.5 * gpost_ref[...]

        def body(rows):
            o_ref[rows, :] = x_ref[rows, :] + _rms(o_ref[rows, :], half_gain, RMS_EPS)
        _for_row_chunks(o_ref.shape[0], body)


def _ffn(x, gpre, gpost, wg, wu, wd, layer, half):
    m, d = x.shape
    ff = wg.shape[-1]
    tm, tf = FFN_ROWS, FFN_COLS
    assert m % tm == 0 and ff % tf == 0 and 0 < FFN_X_PREFETCH_STEP < ff // tf
    return pl.pallas_call(
        _ffn_kernel,
        grid=(m // tm, ff // tf),
        in_specs=[
            pl.BlockSpec(memory_space=pl.ANY),
            pl.BlockSpec((1, d), lambda i, j: (0, 0)),
            pl.BlockSpec((1, d), lambda i, j: (0, 0)),
            pl.BlockSpec((None, None, d, tf), lambda i, j: (layer, half, 0, j)),
            pl.BlockSpec((None, None, d, tf), lambda i, j: (layer, half, 0, j)),
            pl.BlockSpec((None, None, tf, d), lambda i, j: (layer, half, j, 0)),
        ],
        out_specs=pl.BlockSpec((tm, d), lambda i, j: (i, 0)),
        out_shape=jax.ShapeDtypeStruct((m, d), F32),
        scratch_shapes=[pltpu.VMEM((tm, d), BF16), pltpu.VMEM((2, tm, d), F32),
                        pltpu.SemaphoreType.DMA((2,))],
        compiler_params=_params("arbitrary", "arbitrary"),
        name="ffn",
    )(x, gpre, gpost, wg, wu, wd)


def _pool_kernel(x_ref, halo_ref, gpre_ref, gpost_ref, w_ref, scale_ref, o_ref, wb_ref,
                 *sum_refs):
    i = pl.program_id(1)
    ts = x_ref.shape[1]
    d = x_ref.shape[2]
    gdim = w_ref.shape[1]
    rows_all = POOL_HALO + ts
    hx_ref = sum_refs[0]
    gpre = gpre_ref[...]

    @pl.when(jnp.logical_and(pl.program_id(0) == 0, i == 0))
    def _():
        wb_ref[...] = w_ref[...].astype(BF16)

    hh = _rms(halo_ref[0], gpre, RMS_EPS)
    hx_ref[0:POOL_HALO, :] = jnp.where(i > 0, hh, 0.0)
    for r0 in range(0, ts, NORM_ROW_CHUNK):
        hx_ref[POOL_HALO + r0:POOL_HALO + r0 + NORM_ROW_CHUNK, :] = _rms(
            x_ref[0, r0:r0 + NORM_ROW_CHUNK, :], gpre, RMS_EPS)

    for s in range(1, len(sum_refs)):
        shift = 2 ** (s - 1)
        r0 = SUBLANES * s
        c0 = (s - 1) * gdim
        prev = sum_refs[s - 1]
        sum_refs[s][r0:rows_all, c0:d] = (prev[r0:rows_all, c0:d]
                                          + prev[r0 - shift:rows_all - shift, c0:d])

    t = i * ts + lax.broadcasted_iota(jnp.int32, (ts, 1), 0)
    main = slice(POOL_HALO, rows_all)
    for gi, win in enumerate(POOL_WINDOWS):
        cols = slice(gi * gdim, (gi + 1) * gdim)
        stage = win.bit_length() - 1
        if stage < len(sum_refs):
            total = sum_refs[stage][main, cols]
        else:
            prev = sum_refs[stage - 1]
            half = win // 2
            total = prev[main, cols] + prev[POOL_HALO - half:rows_all - half, cols]
        inv_cnt = 1.0 / jnp.minimum(t + 1, win).astype(F32)
        dg = total * inv_cnt - hx_ref[main, cols]
        o_ref[0, :, cols] = (jnp.dot(dg.astype(BF16), wb_ref[gi], preferred_element_type=F32)
                             * scale_ref[:, cols])

    gpost = gpost_ref[...]
    for r0 in range(0, ts, NORM_ROW_CHUNK):
        rows = slice(r0, r0 + NORM_ROW_CHUNK)
        o_ref[0, rows, :] = x_ref[0, rows, :] + _rms(o_ref[0, rows, :], gpost, RMS_EPS)


def _pool_mixer(x, gpre, gpost, w, scale):
    b, s, d = x.shape
    ts = POOL_ROWS
    assert s % ts == 0 and ts % POOL_HALO == 0
    assert POOL_WINDOWS == tuple(2 ** (g + 1) for g in range(POOL_STAGES))
    assert w.shape[0] == len(POOL_WINDOWS) and w.shape[1] * w.shape[0] == d
    halo_blocks = ts // POOL_HALO
    return pl.pallas_call(
        _pool_kernel,
        grid=(b, s // ts),
        in_specs=[
            pl.BlockSpec((1, ts, d), lambda bi, i: (bi, i, 0)),
            pl.BlockSpec((1, POOL_HALO, d),
                         lambda bi, i: (bi, jnp.maximum(i * halo_blocks - 1, 0), 0)),
            pl.BlockSpec((1, d), lambda bi, i: (0, 0)),
            pl.BlockSpec((1, d), lambda bi, i: (0, 0)),
            pl.BlockSpec(w.shape, lambda bi, i: (0, 0, 0), pipeline_mode=pl.Buffered(1)),
            pl.BlockSpec((1, d), lambda bi, i: (0, 0)),
        ],
        out_specs=pl.BlockSpec((1, ts, d), lambda bi, i: (bi, i, 0)),
        out_shape=jax.ShapeDtypeStruct((b, s, d), F32),
        scratch_shapes=[pltpu.VMEM(w.shape, BF16)]
                       + [pltpu.VMEM((POOL_HALO + ts, d), F32)] * POOL_STAGES,
        compiler_params=_params("arbitrary", "arbitrary"),
        name="pool_mixer",
    )(x, x, gpre, gpost, w, scale)


def _norm_proj_kernel(n_out, out_scale, x_ref, g_ref, *refs):
    w_refs, o_refs, h_ref = refs[:n_out], refs[n_out:2 * n_out], refs[2 * n_out]

    def project():
        h = h_ref[...]
        for w_ref, o_ref in zip(w_refs, o_refs):
            y = jnp.dot(h, w_ref[...].astype(BF16), preferred_element_type=F32)
            if out_scale != 1.0:
                y = y * out_scale
            o_ref[...] = y.astype(o_ref.dtype)

    @pl.when(pl.program_id(1) == 0)
    def _():
        _norm_rows_to(h_ref, x_ref, g_ref[...], RMS_EPS)
        project()

    @pl.when(pl.program_id(1) > 0)
    def _():
        project()


def _norm_proj(x, gain, weights, out_scale=1.0):
    m, d = x.shape
    n = weights[0].shape[1]
    n_out = len(weights)
    tm, tn = NORM_PROJ_ROWS, PROJ_COLS_TOTAL // n_out
    assert m % tm == 0 and n % tn == 0
    outs = pl.pallas_call(
        functools.partial(_norm_proj_kernel, n_out, out_scale),
        grid=(m // tm, n // tn),
        in_specs=[pl.BlockSpec((tm, d), lambda i, j: (i, 0)),
                  pl.BlockSpec((1, d), lambda i, j: (0, 0))]
                 + [pl.BlockSpec((d, tn), lambda i, j: (0, j))] * n_out,
        out_specs=[pl.BlockSpec((tm, tn), lambda i, j: (i, j))] * n_out,
        out_shape=[jax.ShapeDtypeStruct((m, n), BF16)] * n_out,
        scratch_shapes=[pltpu.VMEM((tm, d), BF16)],
        compiler_params=_params("parallel", "arbitrary"),
        name="norm_proj",
    )(x, gain, *weights)
    return outs


def _rel_bucket(rel):
    half = NUM_BUCKETS // 2
    max_exact = half // 2
    ret = jnp.where(rel > 0, half, 0)
    n = jnp.abs(rel)
    nf = jnp.maximum(n, 1).astype(F32)
    large = max_exact + (jnp.log(nf / max_exact) / math.log(MAX_DISTANCE / max_exact)
                         * (half - max_exact)).astype(jnp.int32)
    large = jnp.minimum(large, half - 1)
    return ret + jnp.where(n < max_exact, n, large)


FAR_BUCKET = NUM_BUCKETS // 2 - 1
FAR_DISTANCE = MAX_DISTANCE


def _bias_tile_constants(t):
    assert t + 1 >= FAR_DISTANCE and t % CHUNK == 0
    q = np.arange(t, dtype=np.int32)[:, None]
    k = np.arange(t, dtype=np.int32)[None, :]
    buckets = jnp.stack([_rel_bucket(jnp.asarray(k - q)), _rel_bucket(jnp.asarray(k - t - q))])
    allowed = np.stack([(k // CHUNK) <= (q // CHUNK), np.ones((t, t), bool)])
    mask = np.where(allowed, 0.0, -np.inf).astype(np.float32)
    return buckets, jnp.asarray(mask)


def _bias_kernel(table_ref, bucket_ref, mask_ref, o_ref):
    t = bucket_ref.shape[-1]
    table = jnp.broadcast_to(table_ref[0], (t, LANES))
    for tile in range(bucket_ref.shape[0]):
        for c0 in range(0, t, LANES):
            cols = slice(c0, c0 + LANES)
            looked_up = jnp.take_along_axis(table, bucket_ref[tile, :, cols], axis=1)
            o_ref[0, tile, :, cols] = looked_up * LOG2_E + mask_ref[tile, :, cols]


def _bias_tiles(rel_bias, t):
    buckets, mask = _bias_tile_constants(t)
    n_buckets, n_maps = rel_bias.shape
    assert n_buckets <= LANES and t % LANES == 0
    table = jnp.pad(rel_bias.T, ((0, 0), (0, LANES - n_buckets))).reshape(n_maps, 1, LANES)
    return pl.pallas_call(
        _bias_kernel,
        grid=(n_maps,),
        in_specs=[
            pl.BlockSpec((1, 1, LANES), lambda m: (m, 0, 0)),
            pl.BlockSpec((2, t, t), lambda m: (0, 0, 0)),
            pl.BlockSpec((2, t, t), lambda m: (0, 0, 0)),
        ],
        out_specs=pl.BlockSpec((1, 2, t, t), lambda m: (m, 0, 0, 0)),
        out_shape=jax.ShapeDtypeStruct((n_maps, 2, t, t), F32),
        compiler_params=_params("parallel"),
        name="bias_tiles",
    )(table, buckets, mask)


def _softmax_pv(qi, t, q_ref, k_ref, v_ref, cols, far_bias, bias_ref):
    qc = q_ref[0, qi * t:(qi + 1) * t, cols]
    near_bias, diag_bias = bias_ref.at[1], bias_ref.at[0]
    pieces = []
    if qi >= 2:
        pieces.append((0, (qi - 1) * t, None))
    if qi >= 1:
        pieces.append(((qi - 1) * t, t, near_bias))
    pieces.append((qi * t, t, diag_bias))

    scores, row_max = [], []
    for start, size, bias in pieces:
        kc = k_ref[0, start:start + size, cols]
        s = lax.dot_general(qc, kc, (((1,), (1,)), ((), ())), preferred_element_type=F32)
        if bias is None:
            row_max.append(jnp.max(s, axis=-1, keepdims=True) + far_bias)
        else:
            s = s + bias[...]
            row_max.append(jnp.max(s, axis=-1, keepdims=True))
        scores.append(s)
    m = functools.reduce(jnp.maximum, row_max)
    yield
    probs = []
    for s, (_, _, bias) in zip(scores, pieces):
        probs.append(jnp.exp2(s - ((m - far_bias) if bias is None else m)))
        if bias is None:
            yield
    denom = sum(jnp.sum(p, axis=-1, keepdims=True) for p in probs)
    yield
    acc = sum(jnp.dot(p.astype(BF16), v_ref[0, start:start + size, :], preferred_element_type=F32)
              for p, (start, size, _) in zip(probs, pieces))
    return acc * (1.0 / denom)


def _run_staggered(tasks, n_phases, on_done):
    finished = set()
    for step in range(len(tasks) + n_phases - 1):
        for phase in range(n_phases):
            n = step - phase
            if 0 <= n < len(tasks) and n not in finished:
                try:
                    next(tasks[n])
                except StopIteration as done:
                    finished.add(n)
                    on_done(n, done.value)


def _attn_kernel(lambda_init, q_ref, k_ref, v_ref, bias_ref, table_ref, lam_ref, subg_ref, o_ref):
    head = pl.program_id(1)
    t = bias_ref.shape[-1]

    lam = lam_ref[...]
    lam_full = (jnp.exp(jnp.sum(lam[0:1] * lam[1:2], keepdims=True))
                - jnp.exp(jnp.sum(lam[2:3] * lam[3:4], keepdims=True)) + lambda_init)

    tiles = list(reversed(range(q_ref.shape[1] // t)))
    tasks = []
    for i in tiles:
        for c in range(2):
            cols = slice(c * HEAD_DIM, (c + 1) * HEAD_DIM)
            tasks.append(_softmax_pv(i, t, q_ref, k_ref, v_ref, cols,
                                     table_ref[FAR_BUCKET, 2 * head + c] * LOG2_E, bias_ref.at[c]))

    first_map = {}

    def on_done(n, value):
        if n % 2 == 0:
            first_map[n // 2] = value
            return
        i = tiles[n // 2]
        o = first_map.pop(n // 2) - lam_full * value
        o = o * lax.rsqrt(jnp.mean(o * o, axis=-1, keepdims=True) + SUBLN_EPS)
        o_ref[0, i * t:(i + 1) * t, :] = (o * subg_ref[...] * (1.0 - lambda_init)).astype(o_ref.dtype)

    _run_staggered(tasks, 4, on_done)


def _attention(q, k, v, bias_tiles, rel_bias, lam, subln_g, lambda_init):
    b, s, width = q.shape
    t = ATT_TILE
    n_heads = width // V_HEAD_DIM
    assert s % t == 0
    return pl.pallas_call(
        functools.partial(_attn_kernel, lambda_init),
        grid=(b, n_heads),
        in_specs=[
            pl.BlockSpec((1, s, V_HEAD_DIM), lambda bi, h: (bi, 0, h)),
            pl.BlockSpec((1, s, V_HEAD_DIM), lambda bi, h: (bi, 0, h)),
            pl.BlockSpec((1, s, V_HEAD_DIM), lambda bi, h: (bi, 0, h)),
            pl.BlockSpec((2, 2, t, t), lambda bi, h: (h, 0, 0, 0)),
            pl.BlockSpec(memory_space=pltpu.SMEM),
            pl.BlockSpec(lam.shape, lambda bi, h: (0, 0)),
            pl.BlockSpec((1, V_HEAD_DIM), lambda bi, h: (0, 0)),
        ],
        out_specs=pl.BlockSpec((1, s, V_HEAD_DIM), lambda bi, h: (bi, 0, h)),
        out_shape=jax.ShapeDtypeStruct((b, s, width), BF16),
        compiler_params=_params("parallel", "parallel"),
        name="diff_attention",
    )(q, k, v, bias_tiles, rel_bias, lam, subln_g)


def _out_proj_kernel(o_ref, w_ref, x_ref, g_ref, y_ref, wb_ref):
    @pl.when(pl.program_id(0) == 0)
    def _():
        wb_ref[...] = w_ref[...].astype(BF16)

    y_ref[...] = jnp.dot(o_ref[...], wb_ref[...], preferred_element_type=F32)
    gain = g_ref[...]

    def body(rows):
        y_ref[rows, :] = x_ref[rows, :] + _rms(y_ref[rows, :], gain, RMS_EPS)
    _for_row_chunks(y_ref.shape[0], body)


def _out_proj(o, w, x, gain):
    m, d = x.shape
    kdim = o.shape[1]
    tm = PROJ_ROWS
    assert m % tm == 0
    return pl.pallas_call(
        _out_proj_kernel,
        grid=(m // tm,),
        in_specs=[
            pl.BlockSpec((tm, kdim), lambda i: (i, 0)),
            pl.BlockSpec((kdim, d), lambda i: (0, 0), pipeline_mode=pl.Buffered(1)),
            pl.BlockSpec((tm, d), lambda i: (i, 0)),
            pl.BlockSpec((1, d), lambda i: (0, 0)),
        ],
        out_specs=pl.BlockSpec((tm, d), lambda i: (i, 0)),
        out_shape=jax.ShapeDtypeStruct((m, d), F32),
        scratch_shapes=[pltpu.VMEM((kdim, d), BF16)],
        compiler_params=_params("arbitrary"),
        name="out_proj",
    )(o, w, x, gain)


def kernel(x, norm_gains, ffn_w_gate, ffn_w_up, ffn_w_down, pool_w, pool_scale, kv_norm, w_k, w_v,
           rel_bias, w_q, w_o, lambdas, subln_gain):
    b, s, d = x.shape
    depth = norm_gains.shape[0]
    n_a = pool_w.shape[0]
    gains = norm_gains.reshape(depth, 3, 2, 1, d)
    wg, wu, wd = ffn_w_gate, ffn_w_up, ffn_w_down

    xf = x.reshape(b * s, d)
    k = v = bias_tiles = None
    for l in range(depth):
        g = gains[l]
        if l == n_a:
            k, v = _norm_proj(xf, kv_norm.reshape(1, d), [w_k, w_v])
            bias_tiles = _bias_tiles(rel_bias, ATT_TILE)
        xf = _ffn(xf, g[0, 0], g[0, 1], wg, wu, wd, l, 0)
        if l < n_a:
            xf = _pool_mixer(xf.reshape(b, s, d), g[1, 0], g[1, 1], pool_w[l],
                             pool_scale[l].reshape(1, d)).reshape(b * s, d)
        else:
            j = l - n_a
            lambda_init = 0.8 - 0.6 * math.exp(-0.3 * l)
            (q,) = _norm_proj(xf, g[1, 0], [w_q[j]], out_scale=HEAD_DIM ** -0.5 * LOG2_E)
            o = _attention(q.reshape(b, s, -1), k.reshape(b, s, -1), v.reshape(b, s, -1), bias_tiles,
                           rel_bias, lambdas[j], subln_gain[j].reshape(1, -1), lambda_init)
            xf = _out_proj(o.reshape(b * s, -1), w_o[j], xf, g[1, 1])
        xf = _ffn(xf, g[2, 0], g[2, 1], wg, wu, wd, l, 1)
    return xf.reshape(b, s, d)
```

```python
import functools
import math

import numpy as np
import jax
import jax.numpy as jnp
from jax import lax
from jax.experimental import pallas as pl
from jax.experimental.pallas import tpu as pltpu

F32 = jnp.float32
BF16 = jnp.bfloat16

CHUNK = 64
POOL_WINDOWS = (2, 4, 8, 16)
HEAD_DIM = 128
V_HEAD_DIM = 2 * HEAD_DIM
NUM_BUCKETS = 32
MAX_DISTANCE = 128
RMS_EPS = 1e-6
SUBLN_EPS = 1e-5
LOG2_E = math.log2(math.e)

LANES = 128
SUBLANES = 8

VMEM_LIMIT_BYTES = 60 * 1024 * 1024
FFN_ROWS = 1024
FFN_COLS = 256
FFN_X_PREFETCH_STEP = 6
NORM_PROJ_ROWS = 1024
PROJ_ROWS = 512
PROJ_COLS_TOTAL = 1024
NORM_ROW_CHUNK = 16
POOL_ROWS = 512
POOL_STAGES = 4
POOL_HALO = SUBLANES * POOL_STAGES
ATT_TILE = 256


def _rms(x, gain, eps):
    ms = jnp.mean(x * x, axis=-1, keepdims=True)
    return x * lax.rsqrt(ms + eps) * gain


def _for_row_chunks(n_rows, body):
    for r0 in range(0, n_rows, NORM_ROW_CHUNK):
        body(slice(r0, r0 + NORM_ROW_CHUNK))


def _norm_rows_to(dst_ref, src_ref, gain, eps):
    def body(rows):
        dst_ref[rows, :] = _rms(src_ref[rows, :], gain, eps).astype(dst_ref.dtype)
    _for_row_chunks(src_ref.shape[0], body)


def _params(*semantics):
    return pltpu.CompilerParams(dimension_semantics=semantics,
                                vmem_limit_bytes=VMEM_LIMIT_BYTES)


def _ffn_kernel(x_hbm, gpre_ref, gpost_ref, wg_ref, wu_ref, wd_ref, o_ref, h_ref, x_buf, x_sem):
    i = pl.program_id(0)
    j = pl.program_id(1)
    tm = o_ref.shape[0]
    slot = i % 2
    x_ref = x_buf.at[slot]

    def x_copy(tile, to_slot):
        rows = pl.ds(pl.multiple_of(tile * tm, tm), tm)
        return pltpu.make_async_copy(x_hbm.at[rows, :], x_buf.at[to_slot], x_sem.at[to_slot])

    def swiglu_block(first):
        h = h_ref[...]
        g = jnp.dot(h, wg_ref[...].astype(BF16), preferred_element_type=F32)
        u = jnp.dot(h, wu_ref[...].astype(BF16), preferred_element_type=F32)
        a = (g * (1.0 / (1.0 + jnp.exp(-g))) * u).astype(BF16)
        d = jnp.dot(a, wd_ref[...].astype(BF16), preferred_element_type=F32)
        if first:
            o_ref[...] = d
        else:
            o_ref[...] += d

    @pl.when(jnp.logical_and(i == 0, j == 0))
    def _():
        x_copy(0, 0).start()

    @pl.when(j == 0)
    def _():
        x_copy(i, slot).wait()
        _norm_rows_to(h_ref, x_ref, gpre_ref[...], RMS_EPS)
        swiglu_block(first=True)

    @pl.when(jnp.logical_and(j == FFN_X_PREFETCH_STEP, i + 1 < pl.num_programs(0)))
    def _():
        x_copy(i + 1, 1 - slot).start()

    @pl.when(j > 0)
    def _():
        swiglu_block(first=False)

    @pl.when(j == pl.num_programs(1) - 1)
    def _():
        half_gain = 0.5 * gpost_ref[...]

        def body(rows):
            o_ref[rows, :] = x_ref[rows, :] + _rms(o_ref[rows, :], half_gain, RMS_EPS)
        _for_row_chunks(o_ref.shape[0], body)


def _ffn(x, gpre, gpost, wg, wu, wd, layer, half):
    m, d = x.shape
    ff = wg.shape[-1]
    tm, tf = FFN_ROWS, FFN_COLS
    assert m % tm == 0 and ff % tf == 0 and 0 < FFN_X_PREFETCH_STEP < ff // tf
    return pl.pallas_call(
        _ffn_kernel,
        grid=(m // tm, ff // tf),
        in_specs=[
            pl.BlockSpec(memory_space=pl.ANY),
            pl.BlockSpec((1, d), lambda i, j: (0, 0)),
            pl.BlockSpec((1, d), lambda i, j: (0, 0)),
            pl.BlockSpec((None, None, d, tf), lambda i, j: (layer, half, 0, j)),
            pl.BlockSpec((None, None, d, tf), lambda i, j: (layer, half, 0, j)),
            pl.BlockSpec((None, None, tf, d), lambda i, j: (layer, half, j, 0)),
        ],
        out_specs=pl.BlockSpec((tm, d), lambda i, j: (i, 0)),
        out_shape=jax.ShapeDtypeStruct((m, d), F32),
        scratch_shapes=[pltpu.VMEM((tm, d), BF16), pltpu.VMEM((2, tm, d), F32),
                        pltpu.SemaphoreType.DMA((2,))],
        compiler_params=_params("arbitrary", "arbitrary"),
        name="ffn",
    )(x, gpre, gpost, wg, wu, wd)


def _pool_kernel(x_ref, halo_ref, gpre_ref, gpost_ref, w_ref, scale_ref, o_ref, wb_ref,
                 *sum_refs):
    i = pl.program_id(1)
    ts = x_ref.shape[1]
    d = x_ref.shape[2]
    gdim = w_ref.shape[1]
    rows_all = POOL_HALO + ts
    hx_ref = sum_refs[0]
    gpre = gpre_ref[...]

    @pl.when(jnp.logical_and(pl.program_id(0) == 0, i == 0))
    def _():
        wb_ref[...] = w_ref[...].astype(BF16)

    hh = _rms(halo_ref[0], gpre, RMS_EPS)
    hx_ref[0:POOL_HALO, :] = jnp.where(i > 0, hh, 0.0)
    for r0 in range(0, ts, NORM_ROW_CHUNK):
        hx_ref[POOL_HALO + r0:POOL_HALO + r0 + NORM_ROW_CHUNK, :] = _rms(
            x_ref[0, r0:r0 + NORM_ROW_CHUNK, :], gpre, RMS_EPS)

    for s in range(1, len(sum_refs)):
        shift = 2 ** (s - 1)
        r0 = SUBLANES * s
        c0 = (s - 1) * gdim
        prev = sum_refs[s - 1]
        sum_refs[s][r0:rows_all, c0:d] = (prev[r0:rows_all, c0:d]
                                          + prev[r0 - shift:rows_all - shift, c0:d])

    t = i * ts + lax.broadcasted_iota(jnp.int32, (ts, 1), 0)
    main = slice(POOL_HALO, rows_all)
    for gi, win in enumerate(POOL_WINDOWS):
        cols = slice(gi * gdim, (gi + 1) * gdim)
        stage = win.bit_length() - 1
        if stage < len(sum_refs):
            total = sum_refs[stage][main, cols]
        else:
            prev = sum_refs[stage - 1]
            half = win // 2
            total = prev[main, cols] + prev[POOL_HALO - half:rows_all - half, cols]
        inv_cnt = 1.0 / jnp.minimum(t + 1, win).astype(F32)
        dg = total * inv_cnt - hx_ref[main, cols]
        o_ref[0, :, cols] = (jnp.dot(dg.astype(BF16), wb_ref[gi], preferred_element_type=F32)
                             * scale_ref[:, cols])

    gpost = gpost_ref[...]
    for r0 in range(0, ts, NORM_ROW_CHUNK):
        rows = slice(r0, r0 + NORM_ROW_CHUNK)
        o_ref[0, rows, :] = x_ref[0, rows, :] + _rms(o_ref[0, rows, :], gpost, RMS_EPS)


def _pool_mixer(x, gpre, gpost, w, scale):
    b, s, d = x.shape
    ts = POOL_ROWS
    assert s % ts == 0 and ts % POOL_HALO == 0
    assert POOL_WINDOWS == tuple(2 ** (g + 1) for g in range(POOL_STAGES))
    assert w.shape[0] == len(POOL_WINDOWS) and w.shape[1] * w.shape[0] == d
    halo_blocks = ts // POOL_HALO
    return pl.pallas_call(
        _pool_kernel,
        grid=(b, s // ts),
        in_specs=[
            pl.BlockSpec((1, ts, d), lambda bi, i: (bi, i, 0)),
            pl.BlockSpec((1, POOL_HALO, d),
                         lambda bi, i: (bi, jnp.maximum(i * halo_blocks - 1, 0), 0)),
            pl.BlockSpec((1, d), lambda bi, i: (0, 0)),
            pl.BlockSpec((1, d), lambda bi, i: (0, 0)),
            pl.BlockSpec(w.shape, lambda bi, i: (0, 0, 0), pipeline_mode=pl.Buffered(1)),
            pl.BlockSpec((1, d), lambda bi, i: (0, 0)),
        ],
        out_specs=pl.BlockSpec((1, ts, d), lambda bi, i: (bi, i, 0)),
        out_shape=jax.ShapeDtypeStruct((b, s, d), F32),
        scratch_shapes=[pltpu.VMEM(w.shape, BF16)]
                       + [pltpu.VMEM((POOL_HALO + ts, d), F32)] * POOL_STAGES,
        compiler_params=_params("arbitrary", "arbitrary"),
        name="pool_mixer",
    )(x, x, gpre, gpost, w, scale)


def _norm_proj_kernel(n_out, out_scale, x_ref, g_ref, *refs):
    w_refs, o_refs, h_ref = refs[:n_out], refs[n_out:2 * n_out], refs[2 * n_out]

    def project():
        h = h_ref[...]
        for w_ref, o_ref in zip(w_refs, o_refs):
            y = jnp.dot(h, w_ref[...].astype(BF16), preferred_element_type=F32)
            if out_scale != 1.0:
                y = y * out_scale
            o_ref[...] = y.astype(o_ref.dtype)

    @pl.when(pl.program_id(1) == 0)
    def _():
        _norm_rows_to(h_ref, x_ref, g_ref[...], RMS_EPS)
        project()

    @pl.when(pl.program_id(1) > 0)
    def _():
        project()


def _norm_proj(x, gain, weights, out_scale=1.0):
    m, d = x.shape
    n = weights[0].shape[1]
    n_out = len(weights)
    tm, tn = NORM_PROJ_ROWS, PROJ_COLS_TOTAL // n_out
    assert m % tm == 0 and n % tn == 0
    outs = pl.pallas_call(
        functools.partial(_norm_proj_kernel, n_out, out_scale),
        grid=(m // tm, n // tn),
        in_specs=[pl.BlockSpec((tm, d), lambda i, j: (i, 0)),
                  pl.BlockSpec((1, d), lambda i, j: (0, 0))]
                 + [pl.BlockSpec((d, tn), lambda i, j: (0, j))] * n_out,
        out_specs=[pl.BlockSpec((tm, tn), lambda i, j: (i, j))] * n_out,
        out_shape=[jax.ShapeDtypeStruct((m, n), BF16)] * n_out,
        scratch_shapes=[pltpu.VMEM((tm, d), BF16)],
        compiler_params=_params("parallel", "arbitrary"),
        name="norm_proj",
    )(x, gain, *weights)
    return outs


def _norm_proj_resident_kernel(out_scale, x_ref, g_ref, w_ref, o_ref, h_ref, wb_ref):
    @pl.when(pl.program_id(0) == 0)
    def _():
        wb_ref[...] = w_ref[...].astype(BF16)

    _norm_rows_to(h_ref, x_ref, g_ref[...], RMS_EPS)
    h = h_ref[...]
    for c0 in range(0, o_ref.shape[1], PROJ_COLS_TOTAL):
        cols = slice(c0, c0 + PROJ_COLS_TOTAL)
        y = jnp.dot(h, wb_ref[:, cols], preferred_element_type=F32)
        o_ref[:, cols] = (y * out_scale).astype(o_ref.dtype)


def _norm_proj_resident(x, gain, w, out_scale):
    m, d = x.shape
    n = w.shape[1]
    tm = NORM_PROJ_ROWS
    assert m % tm == 0 and n % PROJ_COLS_TOTAL == 0
    return pl.pallas_call(
        functools.partial(_norm_proj_resident_kernel, out_scale),
        grid=(m // tm,),
        in_specs=[pl.BlockSpec((tm, d), lambda i: (i, 0)),
                  pl.BlockSpec((1, d), lambda i: (0, 0)),
                  pl.BlockSpec((d, n), lambda i: (0, 0), pipeline_mode=pl.Buffered(1))],
        out_specs=pl.BlockSpec((tm, n), lambda i: (i, 0)),
        out_shape=jax.ShapeDtypeStruct((m, n), BF16),
        scratch_shapes=[pltpu.VMEM((tm, d), BF16), pltpu.VMEM((d, n), BF16)],
        compiler_params=_params("arbitrary"),
        name="norm_proj_resident",
    )(x, gain, w)


def _rel_bucket(rel):
    half = NUM_BUCKETS // 2
    max_exact = half // 2
    ret = jnp.where(rel > 0, half, 0)
    n = jnp.abs(rel)
    nf = jnp.maximum(n, 1).astype(F32)
    large = max_exact + (jnp.log(nf / max_exact) / math.log(MAX_DISTANCE / max_exact)
                         * (half - max_exact)).astype(jnp.int32)
    large = jnp.minimum(large, half - 1)
    return ret + jnp.where(n < max_exact, n, large)


FAR_BUCKET = NUM_BUCKETS // 2 - 1
FAR_DISTANCE = MAX_DISTANCE


def _bias_tile_constants(t):
    assert t + 1 >= FAR_DISTANCE and t % CHUNK == 0
    q = np.arange(t, dtype=np.int32)[:, None]
    k = np.arange(t, dtype=np.int32)[None, :]
    buckets = jnp.stack([_rel_bucket(jnp.asarray(k - q)), _rel_bucket(jnp.asarray(k - t - q))])
    allowed = np.stack([(k // CHUNK) <= (q // CHUNK), np.ones((t, t), bool)])
    mask = np.where(allowed, 0.0, -np.inf).astype(np.float32)
    return buckets, jnp.asarray(mask)


def _bias_kernel(table_ref, bucket_ref, mask_ref, o_ref):
    t = bucket_ref.shape[-1]
    table = jnp.broadcast_to(table_ref[0], (t, LANES))
    for tile in range(bucket_ref.shape[0]):
        for c0 in range(0, t, LANES):
            cols = slice(c0, c0 + LANES)
            looked_up = jnp.take_along_axis(table, bucket_ref[tile, :, cols], axis=1)
            o_ref[0, tile, :, cols] = looked_up * LOG2_E + mask_ref[tile, :, cols]


def _bias_tiles(rel_bias, t):
    buckets, mask = _bias_tile_constants(t)
    n_buckets, n_maps = rel_bias.shape
    assert n_buckets <= LANES and t % LANES == 0
    table = jnp.pad(rel_bias.T, ((0, 0), (0, LANES - n_buckets))).reshape(n_maps, 1, LANES)
    return pl.pallas_call(
        _bias_kernel,
        grid=(n_maps,),
        in_specs=[
            pl.BlockSpec((1, 1, LANES), lambda m: (m, 0, 0)),
            pl.BlockSpec((2, t, t), lambda m: (0, 0, 0)),
            pl.BlockSpec((2, t, t), lambda m: (0, 0, 0)),
        ],
        out_specs=pl.BlockSpec((1, 2, t, t), lambda m: (m, 0, 0, 0)),
        out_shape=jax.ShapeDtypeStruct((n_maps, 2, t, t), F32),
        compiler_params=_params("parallel"),
        name="bias_tiles",
    )(table, buckets, mask)


def _softmax_pv(qi, t, q_ref, k_ref, v_ref, cols, far_bias, bias_ref):
    qc = q_ref[0, qi * t:(qi + 1) * t, cols]
    near_bias, diag_bias = bias_ref.at[1], bias_ref.at[0]
    pieces = []
    if qi >= 2:
        pieces.append((0, (qi - 1) * t, None))
    if qi >= 1:
        pieces.append(((qi - 1) * t, t, near_bias))
    pieces.append((qi * t, t, diag_bias))

    scores, row_max = [], []
    for start, size, bias in pieces:
        kc = k_ref[0, start:start + size, cols]
        s = lax.dot_general(qc, kc, (((1,), (1,)), ((), ())), preferred_element_type=F32)
        if bias is None:
            row_max.append(jnp.max(s, axis=-1, keepdims=True) + far_bias)
        else:
            s = s + bias[...]
            row_max.append(jnp.max(s, axis=-1, keepdims=True))
        scores.append(s)
    m = functools.reduce(jnp.maximum, row_max)
    yield
    probs = []
    for s, (_, _, bias) in zip(scores, pieces):
        probs.append(jnp.exp2(s - ((m - far_bias) if bias is None else m)))
        if bias is None:
            yield
    denom = sum(jnp.sum(p, axis=-1, keepdims=True) for p in probs)
    yield
    acc = sum(jnp.dot(p.astype(BF16), v_ref[0, start:start + size, :], preferred_element_type=F32)
              for p, (start, size, _) in zip(probs, pieces))
    return acc * (1.0 / denom)


def _run_staggered(tasks, n_phases, on_done):
    finished = set()
    for step in range(len(tasks) + n_phases - 1):
        for phase in range(n_phases):
            n = step - phase
            if 0 <= n < len(tasks) and n not in finished:
                try:
                    next(tasks[n])
                except StopIteration as done:
                    finished.add(n)
                    on_done(n, done.value)


def _attn_kernel(lambda_init, q_ref, k_ref, v_ref, bias_ref, table_ref, lam_ref, subg_ref, o_ref):
    head = pl.program_id(1)
    t = bias_ref.shape[-1]

    lam = lam_ref[...]
    lam_full = (jnp.exp(jnp.sum(lam[0:1] * lam[1:2], keepdims=True))
                - jnp.exp(jnp.sum(lam[2:3] * lam[3:4], keepdims=True)) + lambda_init)

    tiles = list(reversed(range(q_ref.shape[1] // t)))
    tasks = []
    for i in tiles:
        for c in range(2):
            cols = slice(c * HEAD_DIM, (c + 1) * HEAD_DIM)
            tasks.append(_softmax_pv(i, t, q_ref, k_ref, v_ref, cols,
                                     table_ref[FAR_BUCKET, 2 * head + c] * LOG2_E, bias_ref.at[c]))

    first_map = {}

    def on_done(n, value):
        if n % 2 == 0:
            first_map[n // 2] = value
            return
        i = tiles[n // 2]
        o = first_map.pop(n // 2) - lam_full * value
        o = o * lax.rsqrt(jnp.mean(o * o, axis=-1, keepdims=True) + SUBLN_EPS)
        o_ref[0, i * t:(i + 1) * t, :] = (o * subg_ref[...] * (1.0 - lambda_init)).astype(o_ref.dtype)

    _run_staggered(tasks, 4, on_done)


def _attention(q, k, v, bias_tiles, rel_bias, lam, subln_g, lambda_init):
    b, s, width = q.shape
    t = ATT_TILE
    n_heads = width // V_HEAD_DIM
    assert s % t == 0
    return pl.pallas_call(
        functools.partial(_attn_kernel, lambda_init),
        grid=(b, n_heads),
        in_specs=[
            pl.BlockSpec((1, s, V_HEAD_DIM), lambda bi, h: (bi, 0, h)),
            pl.BlockSpec((1, s, V_HEAD_DIM), lambda bi, h: (bi, 0, h)),
            pl.BlockSpec((1, s, V_HEAD_DIM), lambda bi, h: (bi, 0, h)),
            pl.BlockSpec((2, 2, t, t), lambda bi, h: (h, 0, 0, 0)),
            pl.BlockSpec(memory_space=pltpu.SMEM),
            pl.BlockSpec(lam.shape, lambda bi, h: (0, 0)),
            pl.BlockSpec((1, V_HEAD_DIM), lambda bi, h: (0, 0)),
        ],
        out_specs=pl.BlockSpec((1, s, V_HEAD_DIM), lambda bi, h: (bi, 0, h)),
        out_shape=jax.ShapeDtypeStruct((b, s, width), BF16),
        compiler_params=_params("parallel", "parallel"),
        name="diff_attention",
    )(q, k, v, bias_tiles, rel_bias, lam, subln_g)


def _out_proj_kernel(o_ref, w_ref, x_ref, g_ref, y_ref, wb_ref):
    @pl.when(pl.program_id(0) == 0)
    def _():
        wb_ref[...] = w_ref[...].astype(BF16)

    y_ref[...] = jnp.dot(o_ref[...], wb_ref[...], preferred_element_type=F32)
    gain = g_ref[...]

    def body(rows):
        y_ref[rows, :] = x_ref[rows, :] + _rms(y_ref[rows, :], gain, RMS_EPS)
    _for_row_chunks(y_ref.shape[0], body)


def _out_proj(o, w, x, gain):
    m, d = x.shape
    kdim = o.shape[1]
    tm = PROJ_ROWS
    assert m % tm == 0
    return pl.pallas_call(
        _out_proj_kernel,
        grid=(m // tm,),
        in_specs=[
            pl.BlockSpec((tm, kdim), lambda i: (i, 0)),
            pl.BlockSpec((kdim, d), lambda i: (0, 0), pipeline_mode=pl.Buffered(1)),
            pl.BlockSpec((tm, d), lambda i: (i, 0)),
            pl.BlockSpec((1, d), lambda i: (0, 0)),
        ],
        out_specs=pl.BlockSpec((tm, d), lambda i: (i, 0)),
        out_shape=jax.ShapeDtypeStruct((m, d), F32),
        scratch_shapes=[pltpu.VMEM((kdim, d), BF16)],
        compiler_params=_params("arbitrary"),
        name="out_proj",
    )(o, w, x, gain)


def kernel(x, norm_gains, ffn_w_gate, ffn_w_up, ffn_w_down, pool_w, pool_scale, kv_norm, w_k, w_v,
           rel_bias, w_q, w_o, lambdas, subln_gain):
    b, s, d = x.shape
    depth = norm_gains.shape[0]
    n_a = pool_w.shape[0]
    gains = norm_gains.reshape(depth, 3, 2, 1, d)
    wg, wu, wd = ffn_w_gate, ffn_w_up, ffn_w_down

    xf = x.reshape(b * s, d)
    k = v = bias_tiles = None
    for l in range(depth):
        g = gains[l]
        if l == n_a:
            k, v = _norm_proj(xf, kv_norm.reshape(1, d), [w_k, w_v])
            bias_tiles = _bias_tiles(rel_bias, ATT_TILE)
        xf = _ffn(xf, g[0, 0], g[0, 1], wg, wu, wd, l, 0)
        if l < n_a:
            xf = _pool_mixer(xf.reshape(b, s, d), g[1, 0], g[1, 1], pool_w[l],
                             pool_scale[l].reshape(1, d)).reshape(b * s, d)
        else:
            j = l - n_a
            lambda_init = 0.8 - 0.6 * math.exp(-0.3 * l)
            q = _norm_proj_resident(xf, g[1, 0], w_q[j], HEAD_DIM ** -0.5 * LOG2_E)
            o = _attention(q.reshape(b, s, -1), k.reshape(b, s, -1), v.reshape(b, s, -1), bias_tiles,
                           rel_bias, lambdas[j], subln_gain[j].reshape(1, -1), lambda_init)
            xf = _out_proj(o.reshape(b * s, -1), w_o[j], xf, g[1, 1])
        xf = _ffn(xf, g[2, 0], g[2, 1], wg, wu, wd, l, 1)
    return xf.reshape(b, s, d)
```

```python
import functools
import math

import numpy as np
import jax
import jax.numpy as jnp
from jax import lax
from jax.experimental import pallas as pl
from jax.experimental.pallas import tpu as pltpu

F32 = jnp.float32
BF16 = jnp.bfloat16

CHUNK = 64
POOL_WINDOWS = (2, 4, 8, 16)
HEAD_DIM = 128
V_HEAD_DIM = 2 * HEAD_DIM
NUM_BUCKETS = 32
MAX_DISTANCE = 128
RMS_EPS = 1e-6
SUBLN_EPS = 1e-5
LOG2_E = math.log2(math.e)

LANES = 128
SUBLANES = 8

VMEM_LIMIT_BYTES = 60 * 1024 * 1024
FFN_ROWS = 1024
FFN_COLS = 256
FFN_X_PREFETCH_STEP = 6
NORM_PROJ_ROWS = 1024
PROJ_ROWS = 512
PROJ_COLS = 1024
NORM_ROW_CHUNK = 16
POOL_ROWS = 512
POOL_STAGES = 4
POOL_HALO = SUBLANES * POOL_STAGES
ATT_TILE = 256


def _rms(x, gain, eps):
    ms = jnp.mean(x * x, axis=-1, keepdims=True)
    return x * lax.rsqrt(ms + eps) * gain


def _for_row_chunks(n_rows, body):
    for r0 in range(0, n_rows, NORM_ROW_CHUNK):
        body(slice(r0, r0 + NORM_ROW_CHUNK))


def _norm_rows_to(dst_ref, src_ref, gain, eps):
    def body(rows):
        dst_ref[rows, :] = _rms(src_ref[rows, :], gain, eps).astype(dst_ref.dtype)
    _for_row_chunks(src_ref.shape[0], body)


def _params(*semantics):
    return pltpu.CompilerParams(dimension_semantics=semantics,
                                vmem_limit_bytes=VMEM_LIMIT_BYTES)


def _ffn_kernel(x_hbm, gpre_ref, gpost_ref, wg_ref, wu_ref, wd_ref, o_ref, h_ref, x_buf, x_sem):
    i = pl.program_id(0)
    j = pl.program_id(1)
    tm = o_ref.shape[0]
    slot = i % 2
    x_ref = x_buf.at[slot]

    def x_copy(tile, to_slot):
        rows = pl.ds(pl.multiple_of(tile * tm, tm), tm)
        return pltpu.make_async_copy(x_hbm.at[rows, :], x_buf.at[to_slot], x_sem.at[to_slot])

    def swiglu_block(first):
        h = h_ref[...]
        g = jnp.dot(h, wg_ref[...].astype(BF16), preferred_element_type=F32)
        u = jnp.dot(h, wu_ref[...].astype(BF16), preferred_element_type=F32)
        a = (g * (1.0 / (1.0 + jnp.exp(-g))) * u).astype(BF16)
        d = jnp.dot(a, wd_ref[...].astype(BF16), preferred_element_type=F32)
        if first:
            o_ref[...] = d
        else:
            o_ref[...] += d

    @pl.when(jnp.logical_and(i == 0, j == 0))
    def _():
        x_copy(0, 0).start()

    @pl.when(j == 0)
    def _():
        x_copy(i, slot).wait()
        _norm_rows_to(h_ref, x_ref, gpre_ref[...], RMS_EPS)
        swiglu_block(first=True)

    @pl.when(jnp.logical_and(j == FFN_X_PREFETCH_STEP, i + 1 < pl.num_programs(0)))
    def _():
        x_copy(i + 1, 1 - slot).start()

    @pl.when(j > 0)
    def _():
        swiglu_block(first=False)

    @pl.when(j == pl.num_programs(1) - 1)
    def _():
        half_gain = 0.5 * gpost_ref[...]

        def body(rows):
            o_ref[rows, :] = x_ref[rows, :] + _rms(o_ref[rows, :], half_gain, RMS_EPS)
        _for_row_chunks(o_ref.shape[0], body)


def _ffn(x, gpre, gpost, wg, wu, wd, layer, half):
    m, d = x.shape
    ff = wg.shape[-1]
    tm, tf = FFN_ROWS, FFN_COLS
    assert m % tm == 0 and ff % tf == 0 and 0 < FFN_X_PREFETCH_STEP < ff // tf
    return pl.pallas_call(
        _ffn_kernel,
        grid=(m // tm, ff // tf),
        in_specs=[
            pl.BlockSpec(memory_space=pl.ANY),
            pl.BlockSpec((1, d), lambda i, j: (0, 0)),
            pl.BlockSpec((1, d), lambda i, j: (0, 0)),
            pl.BlockSpec((None, None, d, tf), lambda i, j: (layer, half, 0, j)),
            pl.BlockSpec((None, None, d, tf), lambda i, j: (layer, half, 0, j)),
            pl.BlockSpec((None, None, tf, d), lambda i, j: (layer, half, j, 0)),
        ],
        out_specs=pl.BlockSpec((tm, d), lambda i, j: (i, 0)),
        out_shape=jax.ShapeDtypeStruct((m, d), F32),
        scratch_shapes=[pltpu.VMEM((tm, d), BF16), pltpu.VMEM((2, tm, d), F32),
                        pltpu.SemaphoreType.DMA((2,))],
        compiler_params=_params("arbitrary", "arbitrary"),
        name="ffn",
    )(x, gpre, gpost, wg, wu, wd)


def _pool_kernel(x_ref, halo_ref, gpre_ref, gpost_ref, w_ref, scale_ref, o_ref, wb_ref,
                 *sum_refs):
    i = pl.program_id(1)
    ts = x_ref.shape[1]
    d = x_ref.shape[2]
    gdim = w_ref.shape[1]
    rows_all = POOL_HALO + ts
    hx_ref = sum_refs[0]
    gpre = gpre_ref[...]

    @pl.when(jnp.logical_and(pl.program_id(0) == 0, i == 0))
    def _():
        wb_ref[...] = w_ref[...].astype(BF16)

    hh = _rms(halo_ref[0], gpre, RMS_EPS)
    hx_ref[0:POOL_HALO, :] = jnp.where(i > 0, hh, 0.0)
    for r0 in range(0, ts, NORM_ROW_CHUNK):
        hx_ref[POOL_HALO + r0:POOL_HALO + r0 + NORM_ROW_CHUNK, :] = _rms(
            x_ref[0, r0:r0 + NORM_ROW_CHUNK, :], gpre, RMS_EPS)

    for s in range(1, len(sum_refs)):
        shift = 2 ** (s - 1)
        r0 = SUBLANES * s
        c0 = (s - 1) * gdim
        prev = sum_refs[s - 1]
        sum_refs[s][r0:rows_all, c0:d] = (prev[r0:rows_all, c0:d]
                                          + prev[r0 - shift:rows_all - shift, c0:d])

    t = i * ts + lax.broadcasted_iota(jnp.int32, (ts, 1), 0)
    main = slice(POOL_HALO, rows_all)
    for gi, win in enumerate(POOL_WINDOWS):
        cols = slice(gi * gdim, (gi + 1) * gdim)
        stage = win.bit_length() - 1
        if stage < len(sum_refs):
            total = sum_refs[stage][main, cols]
        else:
            prev = sum_refs[stage - 1]
            half = win // 2
            total = prev[main, cols] + prev[POOL_HALO - half:rows_all - half, cols]
        inv_cnt = 1.0 / jnp.minimum(t + 1, win).astype(F32)
        dg = total * inv_cnt - hx_ref[main, cols]
        o_ref[0, :, cols] = (jnp.dot(dg.astype(BF16), wb_ref[gi], preferred_element_type=F32)
                             * scale_ref[:, cols])

    gpost = gpost_ref[...]
    for r0 in range(0, ts, NORM_ROW_CHUNK):
        rows = slice(r0, r0 + NORM_ROW_CHUNK)
        o_ref[0, rows, :] = x_ref[0, rows, :] + _rms(o_ref[0, rows, :], gpost, RMS_EPS)


def _pool_mixer(x, gpre, gpost, w, scale):
    b, s, d = x.shape
    ts = POOL_ROWS
    assert s % ts == 0 and ts % POOL_HALO == 0
    assert POOL_WINDOWS == tuple(2 ** (g + 1) for g in range(POOL_STAGES))
    assert w.shape[0] == len(POOL_WINDOWS) and w.shape[1] * w.shape[0] == d
    halo_blocks = ts // POOL_HALO
    return pl.pallas_call(
        _pool_kernel,
        grid=(b, s // ts),
        in_specs=[
            pl.BlockSpec((1, ts, d), lambda bi, i: (bi, i, 0)),
            pl.BlockSpec((1, POOL_HALO, d),
                         lambda bi, i: (bi, jnp.maximum(i * halo_blocks - 1, 0), 0)),
            pl.BlockSpec((1, d), lambda bi, i: (0, 0)),
            pl.BlockSpec((1, d), lambda bi, i: (0, 0)),
            pl.BlockSpec(w.shape, lambda bi, i: (0, 0, 0), pipeline_mode=pl.Buffered(1)),
            pl.BlockSpec((1, d), lambda bi, i: (0, 0)),
        ],
        out_specs=pl.BlockSpec((1, ts, d), lambda bi, i: (bi, i, 0)),
        out_shape=jax.ShapeDtypeStruct((b, s, d), F32),
        scratch_shapes=[pltpu.VMEM(w.shape, BF16)]
                       + [pltpu.VMEM((POOL_HALO + ts, d), F32)] * POOL_STAGES,
        compiler_params=_params("arbitrary", "arbitrary"),
        name="pool_mixer",
    )(x, x, gpre, gpost, w, scale)


def _norm_proj_kernel(out_scale, x_ref, g_ref, w_ref, o_ref, h_ref, wb_ref):
    @pl.when(pl.program_id(0) == 0)
    def _():
        wb_ref[...] = w_ref[...].astype(BF16)

    _norm_rows_to(h_ref, x_ref, g_ref[...], RMS_EPS)
    h = h_ref[...]
    for c0 in range(0, o_ref.shape[1], PROJ_COLS):
        cols = slice(c0, c0 + PROJ_COLS)
        y = jnp.dot(h, wb_ref[:, cols], preferred_element_type=F32)
        if out_scale != 1.0:
            y = y * out_scale
        o_ref[:, cols] = y.astype(o_ref.dtype)


def _norm_proj(x, gain, w, out_scale):
    m, d = x.shape
    n = w.shape[1]
    tm = NORM_PROJ_ROWS
    assert m % tm == 0 and n % PROJ_COLS == 0
    return pl.pallas_call(
        functools.partial(_norm_proj_kernel, out_scale),
        grid=(m // tm,),
        in_specs=[pl.BlockSpec((tm, d), lambda i: (i, 0)),
                  pl.BlockSpec((1, d), lambda i: (0, 0)),
                  pl.BlockSpec((d, n), lambda i: (0, 0), pipeline_mode=pl.Buffered(1))],
        out_specs=pl.BlockSpec((tm, n), lambda i: (i, 0)),
        out_shape=jax.ShapeDtypeStruct((m, n), BF16),
        scratch_shapes=[pltpu.VMEM((tm, d), BF16), pltpu.VMEM((d, n), BF16)],
        compiler_params=_params("arbitrary"),
        name="norm_proj",
    )(x, gain, w)


def _rel_bucket(rel):
    half = NUM_BUCKETS // 2
    max_exact = half // 2
    ret = jnp.where(rel > 0, half, 0)
    n = jnp.abs(rel)
    nf = jnp.maximum(n, 1).astype(F32)
    large = max_exact + (jnp.log(nf / max_exact) / math.log(MAX_DISTANCE / max_exact)
                         * (half - max_exact)).astype(jnp.int32)
    large = jnp.minimum(large, half - 1)
    return ret + jnp.where(n < max_exact, n, large)


FAR_BUCKET = NUM_BUCKETS // 2 - 1
FAR_DISTANCE = MAX_DISTANCE


def _bias_tile_constants(t):
    assert t + 1 >= FAR_DISTANCE and t % CHUNK == 0
    q = np.arange(t, dtype=np.int32)[:, None]
    k = np.arange(t, dtype=np.int32)[None, :]
    buckets = jnp.stack([_rel_bucket(jnp.asarray(k - q)), _rel_bucket(jnp.asarray(k - t - q))])
    allowed = np.stack([(k // CHUNK) <= (q // CHUNK), np.ones((t, t), bool)])
    mask = np.where(allowed, 0.0, -np.inf).astype(np.float32)
    return buckets, jnp.asarray(mask)


def _bias_kernel(table_ref, bucket_ref, mask_ref, o_ref):
    t = bucket_ref.shape[-1]
    table = jnp.broadcast_to(table_ref[0], (t, LANES))
    for tile in range(bucket_ref.shape[0]):
        for c0 in range(0, t, LANES):
            cols = slice(c0, c0 + LANES)
            looked_up = jnp.take_along_axis(table, bucket_ref[tile, :, cols], axis=1)
            o_ref[0, tile, :, cols] = looked_up * LOG2_E + mask_ref[tile, :, cols]


def _bias_tiles(rel_bias, t):
    buckets, mask = _bias_tile_constants(t)
    n_buckets, n_maps = rel_bias.shape
    assert n_buckets <= LANES and t % LANES == 0
    table = jnp.pad(rel_bias.T, ((0, 0), (0, LANES - n_buckets))).reshape(n_maps, 1, LANES)
    return pl.pallas_call(
        _bias_kernel,
        grid=(n_maps,),
        in_specs=[
            pl.BlockSpec((1, 1, LANES), lambda m: (m, 0, 0)),
            pl.BlockSpec((2, t, t), lambda m: (0, 0, 0)),
            pl.BlockSpec((2, t, t), lambda m: (0, 0, 0)),
        ],
        out_specs=pl.BlockSpec((1, 2, t, t), lambda m: (m, 0, 0, 0)),
        out_shape=jax.ShapeDtypeStruct((n_maps, 2, t, t), F32),
        compiler_params=_params("parallel"),
        name="bias_tiles",
    )(table, buckets, mask)


def _softmax_pv(qi, t, q_ref, k_ref, v_ref, cols, far_bias, bias_ref):
    qc = q_ref[0, qi * t:(qi + 1) * t, cols]
    near_bias, diag_bias = bias_ref.at[1], bias_ref.at[0]
    pieces = []
    if qi >= 2:
        pieces.append((0, (qi - 1) * t, None))
    if qi >= 1:
        pieces.append(((qi - 1) * t, t, near_bias))
    pieces.append((qi * t, t, diag_bias))

    scores, row_max = [], []
    for start, size, bias in pieces:
        kc = k_ref[0, start:start + size, cols]
        s = lax.dot_general(qc, kc, (((1,), (1,)), ((), ())), preferred_element_type=F32)
        if bias is None:
            row_max.append(jnp.max(s, axis=-1, keepdims=True) + far_bias)
        else:
            s = s + bias[...]
            row_max.append(jnp.max(s, axis=-1, keepdims=True))
        scores.append(s)
    m = functools.reduce(jnp.maximum, row_max)
    yield
    probs = []
    for s, (_, _, bias) in zip(scores, pieces):
        probs.append(jnp.exp2(s - ((m - far_bias) if bias is None else m)))
        if bias is None:
            yield
    denom = sum(jnp.sum(p, axis=-1, keepdims=True) for p in probs)
    yield
    acc = sum(jnp.dot(p.astype(BF16), v_ref[0, start:start + size, :], preferred_element_type=F32)
              for p, (start, size, _) in zip(probs, pieces))
    return acc * (1.0 / denom)


def _run_staggered(tasks, n_phases, on_done):
    finished = set()
    for step in range(len(tasks) + n_phases - 1):
        for phase in range(n_phases):
            n = step - phase
            if 0 <= n < len(tasks) and n not in finished:
                try:
                    next(tasks[n])
                except StopIteration as done:
                    finished.add(n)
                    on_done(n, done.value)


def _attn_kernel(lambda_init, q_ref, k_ref, v_ref, bias_ref, table_ref, lam_ref, subg_ref, o_ref):
    head = pl.program_id(1)
    t = bias_ref.shape[-1]

    lam = lam_ref[...]
    lam_full = (jnp.exp(jnp.sum(lam[0:1] * lam[1:2], keepdims=True))
                - jnp.exp(jnp.sum(lam[2:3] * lam[3:4], keepdims=True)) + lambda_init)

    tiles = list(reversed(range(q_ref.shape[1] // t)))
    tasks = []
    for i in tiles:
        for c in range(2):
            cols = slice(c * HEAD_DIM, (c + 1) * HEAD_DIM)
            tasks.append(_softmax_pv(i, t, q_ref, k_ref, v_ref, cols,
                                     table_ref[FAR_BUCKET, 2 * head + c] * LOG2_E, bias_ref.at[c]))

    first_map = {}

    def on_done(n, value):
        if n % 2 == 0:
            first_map[n // 2] = value
            return
        i = tiles[n // 2]
        o = first_map.pop(n // 2) - lam_full * value
        o = o * lax.rsqrt(jnp.mean(o * o, axis=-1, keepdims=True) + SUBLN_EPS)
        o_ref[0, i * t:(i + 1) * t, :] = (o * subg_ref[...] * (1.0 - lambda_init)).astype(o_ref.dtype)

    _run_staggered(tasks, 4, on_done)


def _attention(q, k, v, bias_tiles, rel_bias, lam, subln_g, lambda_init):
    b, s, width = q.shape
    t = ATT_TILE
    n_heads = width // V_HEAD_DIM
    assert s % t == 0
    return pl.pallas_call(
        functools.partial(_attn_kernel, lambda_init),
        grid=(b, n_heads),
        in_specs=[
            pl.BlockSpec((1, s, V_HEAD_DIM), lambda bi, h: (bi, 0, h)),
            pl.BlockSpec((1, s, V_HEAD_DIM), lambda bi, h: (bi, 0, h)),
            pl.BlockSpec((1, s, V_HEAD_DIM), lambda bi, h: (bi, 0, h)),
            pl.BlockSpec((2, 2, t, t), lambda bi, h: (h, 0, 0, 0)),
            pl.BlockSpec(memory_space=pltpu.SMEM),
            pl.BlockSpec(lam.shape, lambda bi, h: (0, 0)),
            pl.BlockSpec((1, V_HEAD_DIM), lambda bi, h: (0, 0)),
        ],
        out_specs=pl.BlockSpec((1, s, V_HEAD_DIM), lambda bi, h: (bi, 0, h)),
        out_shape=jax.ShapeDtypeStruct((b, s, width), BF16),
        compiler_params=_params("parallel", "parallel"),
        name="diff_attention",
    )(q, k, v, bias_tiles, rel_bias, lam, subln_g)


def _out_proj_kernel(o_ref, w_ref, x_ref, g_ref, y_ref, wb_ref):
    @pl.when(pl.program_id(0) == 0)
    def _():
        wb_ref[...] = w_ref[...].astype(BF16)

    y_ref[...] = jnp.dot(o_ref[...], wb_ref[...], preferred_element_type=F32)
    gain = g_ref[...]

    def body(rows):
        y_ref[rows, :] = x_ref[rows, :] + _rms(y_ref[rows, :], gain, RMS_EPS)
    _for_row_chunks(y_ref.shape[0], body)


def _out_proj(o, w, x, gain):
    m, d = x.shape
    kdim = o.shape[1]
    tm = PROJ_ROWS
    assert m % tm == 0
    return pl.pallas_call(
        _out_proj_kernel,
        grid=(m // tm,),
        in_specs=[
            pl.BlockSpec((tm, kdim), lambda i: (i, 0)),
            pl.BlockSpec((kdim, d), lambda i: (0, 0), pipeline_mode=pl.Buffered(1)),
            pl.BlockSpec((tm, d), lambda i: (i, 0)),
            pl.BlockSpec((1, d), lambda i: (0, 0)),
        ],
        out_specs=pl.BlockSpec((tm, d), lambda i: (i, 0)),
        out_shape=jax.ShapeDtypeStruct((m, d), F32),
        scratch_shapes=[pltpu.VMEM((kdim, d), BF16)],
        compiler_params=_params("arbitrary"),
        name="out_proj",
    )(o, w, x, gain)


def kernel(x, norm_gains, ffn_w_gate, ffn_w_up, ffn_w_down, pool_w, pool_scale, kv_norm, w_k, w_v,
           rel_bias, w_q, w_o, lambdas, subln_gain):
    b, s, d = x.shape
    depth = norm_gains.shape[0]
    n_a = pool_w.shape[0]
    gains = norm_gains.reshape(depth, 3, 2, 1, d)
    wg, wu, wd = ffn_w_gate, ffn_w_up, ffn_w_down

    xf = x.reshape(b * s, d)
    k = v = bias_tiles = None
    for l in range(depth):
        g = gains[l]
        if l == n_a:
            k = _norm_proj(xf, kv_norm.reshape(1, d), w_k, 1.0)
            v = _norm_proj(xf, kv_norm.reshape(1, d), w_v, 1.0)
            bias_tiles = _bias_tiles(rel_bias, ATT_TILE)
        xf = _ffn(xf, g[0, 0], g[0, 1], wg, wu, wd, l, 0)
        if l < n_a:
            xf = _pool_mixer(xf.reshape(b, s, d), g[1, 0], g[1, 1], pool_w[l],
                             pool_scale[l].reshape(1, d)).reshape(b * s, d)
        else:
            j = l - n_a
            lambda_init = 0.8 - 0.6 * math.exp(-0.3 * l)
            q = _norm_proj(xf, g[1, 0], w_q[j], HEAD_DIM ** -0.5 * LOG2_E)
            o = _attention(q.reshape(b, s, -1), k.reshape(b, s, -1), v.reshape(b, s, -1), bias_tiles,
                           rel_bias, lambdas[j], subln_gain[j].reshape(1, -1), lambda_init)
            xf = _out_proj(o.reshape(b * s, -1), w_o[j], xf, g[1, 1])
        xf = _ffn(xf, g[2, 0], g[2, 1], wg, wu, wd, l, 1)
    return xf.reshape(b, s, d)
```

```python
import functools
import math

import numpy as np
import jax
import jax.numpy as jnp
from jax import lax
from jax.experimental import pallas as pl
from jax.experimental.pallas import tpu as pltpu

F32 = jnp.float32
BF16 = jnp.bfloat16

CHUNK = 64
POOL_WINDOWS = (2, 4, 8, 16)
HEAD_DIM = 128
V_HEAD_DIM = 2 * HEAD_DIM
NUM_BUCKETS = 32
MAX_DISTANCE = 128
RMS_EPS = 1e-6
SUBLN_EPS = 1e-5
LOG2_E = math.log2(math.e)

LANES = 128
SUBLANES = 8

VMEM_LIMIT_BYTES = 60 * 1024 * 1024
FFN_ROWS = 1024
FFN_COLS = 512
FFN_X_PREFETCH_STEP = 6
NORM_PROJ_ROWS = 1024
PROJ_ROWS = 512
PROJ_COLS = 1024
NORM_ROW_CHUNK = 16
POOL_ROWS = 512
POOL_STAGES = 4
POOL_HALO = SUBLANES * POOL_STAGES
ATT_TILE = 256


def _rms(x, gain, eps):
    ms = jnp.mean(x * x, axis=-1, keepdims=True)
    return x * lax.rsqrt(ms + eps) * gain


def _for_row_chunks(n_rows, body):
    for r0 in range(0, n_rows, NORM_ROW_CHUNK):
        body(slice(r0, r0 + NORM_ROW_CHUNK))


def _norm_rows_to(dst_ref, src_ref, gain, eps):
    def body(rows):
        dst_ref[rows, :] = _rms(src_ref[rows, :], gain, eps).astype(dst_ref.dtype)
    _for_row_chunks(src_ref.shape[0], body)


def _params(*semantics):
    return pltpu.CompilerParams(dimension_semantics=semantics,
                                vmem_limit_bytes=VMEM_LIMIT_BYTES)


def _ffn_kernel(x_hbm, gpre_ref, gpost_ref, wg_ref, wu_ref, wd_ref, o_hbm, h_ref, a_ref, acc_ref,
                x_buf, x_sem, o_sem):
    i = pl.program_id(0)
    j = pl.program_id(1)
    n_tiles = pl.num_programs(0)
    tm = acc_ref.shape[0]
    slot = i % 2
    x_ref = x_buf.at[slot]

    def x_copy(tile, to_slot):
        rows = pl.ds(pl.multiple_of(tile * tm, tm), tm)
        return pltpu.make_async_copy(x_hbm.at[rows, :], x_buf.at[to_slot], x_sem.at[to_slot])

    def out_copy(tile):
        rows = pl.ds(pl.multiple_of(tile * tm, tm), tm)
        return pltpu.make_async_copy(acc_ref, o_hbm.at[rows, :], o_sem.at[0])

    def gate_up():
        h = h_ref[...]
        g = jnp.dot(h, wg_ref[...].astype(BF16), preferred_element_type=F32)
        u = jnp.dot(h, wu_ref[...].astype(BF16), preferred_element_type=F32)
        return (g * (1.0 / (1.0 + jnp.exp(-g))) * u).astype(BF16)

    def down(a):
        return jnp.dot(a, wd_ref[...].astype(BF16), preferred_element_type=F32)

    @pl.when(jnp.logical_and(i == 0, j == 0))
    def _():
        x_copy(0, 0).start()

    @pl.when(j == 0)
    def _():
        x_copy(i, slot).wait()
        _norm_rows_to(h_ref, x_ref, gpre_ref[...], RMS_EPS)
        a_ref[...] = gate_up()

    @pl.when(jnp.logical_and(j == 0, i > 0))
    def _():
        out_copy(i - 1).wait()

    @pl.when(j == 0)
    def _():
        acc_ref[...] = down(a_ref[...])

    @pl.when(jnp.logical_and(j == FFN_X_PREFETCH_STEP, i + 1 < n_tiles))
    def _():
        x_copy(i + 1, 1 - slot).start()

    @pl.when(j > 0)
    def _():
        acc_ref[...] += down(gate_up())

    @pl.when(j == pl.num_programs(1) - 1)
    def _():
        half_gain = 0.5 * gpost_ref[...]

        def body(rows):
            acc_ref[rows, :] = x_ref[rows, :] + _rms(acc_ref[rows, :], half_gain, RMS_EPS)
        _for_row_chunks(tm, body)
        out_copy(i).start()

    @pl.when(jnp.logical_and(j == pl.num_programs(1) - 1, i == n_tiles - 1))
    def _():
        out_copy(i).wait()


def _ffn(x, gpre, gpost, wg, wu, wd, layer, half):
    m, d = x.shape
    ff = wg.shape[-1]
    tm, tf = FFN_ROWS, FFN_COLS
    assert m % tm == 0 and ff % tf == 0 and 0 < FFN_X_PREFETCH_STEP < ff // tf
    return pl.pallas_call(
        _ffn_kernel,
        grid=(m // tm, ff // tf),
        in_specs=[
            pl.BlockSpec(memory_space=pl.ANY),
            pl.BlockSpec((1, d), lambda i, j: (0, 0)),
            pl.BlockSpec((1, d), lambda i, j: (0, 0)),
            pl.BlockSpec((None, None, d, tf), lambda i, j: (layer, half, 0, j)),
            pl.BlockSpec((None, None, d, tf), lambda i, j: (layer, half, 0, j)),
            pl.BlockSpec((None, None, tf, d), lambda i, j: (layer, half, j, 0)),
        ],
        out_specs=pl.BlockSpec(memory_space=pl.ANY),
        out_shape=jax.ShapeDtypeStruct((m, d), F32),
        scratch_shapes=[pltpu.VMEM((tm, d), BF16), pltpu.VMEM((tm, tf), BF16),
                        pltpu.VMEM((tm, d), F32), pltpu.VMEM((2, tm, d), F32),
                        pltpu.SemaphoreType.DMA((2,)), pltpu.SemaphoreType.DMA((1,))],
        compiler_params=_params("arbitrary", "arbitrary"),
        name="ffn",
    )(x, gpre, gpost, wg, wu, wd)


def _pool_kernel(x_ref, halo_ref, gpre_ref, gpost_ref, w_ref, scale_ref, o_ref, wb_ref,
                 *sum_refs):
    i = pl.program_id(1)
    ts = x_ref.shape[1]
    d = x_ref.shape[2]
    gdim = w_ref.shape[1]
    rows_all = POOL_HALO + ts
    hx_ref = sum_refs[0]
    gpre = gpre_ref[...]

    @pl.when(jnp.logical_and(pl.program_id(0) == 0, i == 0))
    def _():
        wb_ref[...] = w_ref[...].astype(BF16)

    hh = _rms(halo_ref[0], gpre, RMS_EPS)
    hx_ref[0:POOL_HALO, :] = jnp.where(i > 0, hh, 0.0)
    for r0 in range(0, ts, NORM_ROW_CHUNK):
        hx_ref[POOL_HALO + r0:POOL_HALO + r0 + NORM_ROW_CHUNK, :] = _rms(
            x_ref[0, r0:r0 + NORM_ROW_CHUNK, :], gpre, RMS_EPS)

    for s in range(1, len(sum_refs)):
        shift = 2 ** (s - 1)
        r0 = SUBLANES * s
        c0 = (s - 1) * gdim
        prev = sum_refs[s - 1]
        sum_refs[s][r0:rows_all, c0:d] = (prev[r0:rows_all, c0:d]
                                          + prev[r0 - shift:rows_all - shift, c0:d])

    t = i * ts + lax.broadcasted_iota(jnp.int32, (ts, 1), 0)
    main = slice(POOL_HALO, rows_all)
    for gi, win in enumerate(POOL_WINDOWS):
        cols = slice(gi * gdim, (gi + 1) * gdim)
        stage = win.bit_length() - 1
        if stage < len(sum_refs):
            total = sum_refs[stage][main, cols]
        else:
            prev = sum_refs[stage - 1]
            half = win // 2
            total = prev[main, cols] + prev[POOL_HALO - half:rows_all - half, cols]
        inv_cnt = 1.0 / jnp.minimum(t + 1, win).astype(F32)
        dg = total * inv_cnt - hx_ref[main, cols]
        o_ref[0, :, cols] = (jnp.dot(dg.astype(BF16), wb_ref[gi], preferred_element_type=F32)
                             * scale_ref[:, cols])

    gpost = gpost_ref[...]
    for r0 in range(0, ts, NORM_ROW_CHUNK):
        rows = slice(r0, r0 + NORM_ROW_CHUNK)
        o_ref[0, rows, :] = x_ref[0, rows, :] + _rms(o_ref[0, rows, :], gpost, RMS_EPS)


def _pool_mixer(x, gpre, gpost, w, scale):
    b, s, d = x.shape
    ts = POOL_ROWS
    assert s % ts == 0 and ts % POOL_HALO == 0
    assert POOL_WINDOWS == tuple(2 ** (g + 1) for g in range(POOL_STAGES))
    assert w.shape[0] == len(POOL_WINDOWS) and w.shape[1] * w.shape[0] == d
    halo_blocks = ts // POOL_HALO
    return pl.pallas_call(
        _pool_kernel,
        grid=(b, s // ts),
        in_specs=[
            pl.BlockSpec((1, ts, d), lambda bi, i: (bi, i, 0)),
            pl.BlockSpec((1, POOL_HALO, d),
                         lambda bi, i: (bi, jnp.maximum(i * halo_blocks - 1, 0), 0)),
            pl.BlockSpec((1, d), lambda bi, i: (0, 0)),
            pl.BlockSpec((1, d), lambda bi, i: (0, 0)),
            pl.BlockSpec(w.shape, lambda bi, i: (0, 0, 0), pipeline_mode=pl.Buffered(1)),
            pl.BlockSpec((1, d), lambda bi, i: (0, 0)),
        ],
        out_specs=pl.BlockSpec((1, ts, d), lambda bi, i: (bi, i, 0)),
        out_shape=jax.ShapeDtypeStruct((b, s, d), F32),
        scratch_shapes=[pltpu.VMEM(w.shape, BF16)]
                       + [pltpu.VMEM((POOL_HALO + ts, d), F32)] * POOL_STAGES,
        compiler_params=_params("arbitrary", "arbitrary"),
        name="pool_mixer",
    )(x, x, gpre, gpost, w, scale)


def _norm_proj_kernel(out_scale, x_ref, g_ref, w_ref, o_ref, h_ref, wb_ref):
    @pl.when(pl.program_id(0) == 0)
    def _():
        wb_ref[...] = w_ref[...].astype(BF16)

    _norm_rows_to(h_ref, x_ref, g_ref[...], RMS_EPS)
    h = h_ref[...]
    for c0 in range(0, o_ref.shape[1], PROJ_COLS):
        cols = slice(c0, c0 + PROJ_COLS)
        y = jnp.dot(h, wb_ref[:, cols], preferred_element_type=F32)
        if out_scale != 1.0:
            y = y * out_scale
        o_ref[:, cols] = y.astype(o_ref.dtype)


def _norm_proj(x, gain, w, out_scale):
    m, d = x.shape
    n = w.shape[1]
    tm = NORM_PROJ_ROWS
    assert m % tm == 0 and n % PROJ_COLS == 0
    return pl.pallas_call(
        functools.partial(_norm_proj_kernel, out_scale),
        grid=(m // tm,),
        in_specs=[pl.BlockSpec((tm, d), lambda i: (i, 0)),
                  pl.BlockSpec((1, d), lambda i: (0, 0)),
                  pl.BlockSpec((d, n), lambda i: (0, 0), pipeline_mode=pl.Buffered(1))],
        out_specs=pl.BlockSpec((tm, n), lambda i: (i, 0)),
        out_shape=jax.ShapeDtypeStruct((m, n), BF16),
        scratch_shapes=[pltpu.VMEM((tm, d), BF16), pltpu.VMEM((d, n), BF16)],
        compiler_params=_params("arbitrary"),
        name="norm_proj",
    )(x, gain, w)


def _rel_bucket(rel):
    half = NUM_BUCKETS // 2
    max_exact = half // 2
    ret = jnp.where(rel > 0, half, 0)
    n = jnp.abs(rel)
    nf = jnp.maximum(n, 1).astype(F32)
    large = max_exact + (jnp.log(nf / max_exact) / math.log(MAX_DISTANCE / max_exact)
                         * (half - max_exact)).astype(jnp.int32)
    large = jnp.minimum(large, half - 1)
    return ret + jnp.where(n < max_exact, n, large)


FAR_BUCKET = NUM_BUCKETS // 2 - 1
FAR_DISTANCE = MAX_DISTANCE


def _bias_tile_constants(t):
    assert t + 1 >= FAR_DISTANCE and t % CHUNK == 0
    q = np.arange(t, dtype=np.int32)[:, None]
    k = np.arange(t, dtype=np.int32)[None, :]
    buckets = jnp.stack([_rel_bucket(jnp.asarray(k - q)), _rel_bucket(jnp.asarray(k - t - q))])
    allowed = np.stack([(k // CHUNK) <= (q // CHUNK), np.ones((t, t), bool)])
    mask = np.where(allowed, 0.0, -np.inf).astype(np.float32)
    return buckets, jnp.asarray(mask)


def _bias_kernel(table_ref, bucket_ref, mask_ref, o_ref):
    t = bucket_ref.shape[-1]
    table = jnp.broadcast_to(table_ref[0], (t, LANES))
    for tile in range(bucket_ref.shape[0]):
        for c0 in range(0, t, LANES):
            cols = slice(c0, c0 + LANES)
            looked_up = jnp.take_along_axis(table, bucket_ref[tile, :, cols], axis=1)
            o_ref[0, tile, :, cols] = looked_up * LOG2_E + mask_ref[tile, :, cols]


def _bias_tiles(rel_bias, t):
    buckets, mask = _bias_tile_constants(t)
    n_buckets, n_maps = rel_bias.shape
    assert n_buckets <= LANES and t % LANES == 0
    table = jnp.pad(rel_bias.T, ((0, 0), (0, LANES - n_buckets))).reshape(n_maps, 1, LANES)
    return pl.pallas_call(
        _bias_kernel,
        grid=(n_maps,),
        in_specs=[
            pl.BlockSpec((1, 1, LANES), lambda m: (m, 0, 0)),
            pl.BlockSpec((2, t, t), lambda m: (0, 0, 0)),
            pl.BlockSpec((2, t, t), lambda m: (0, 0, 0)),
        ],
        out_specs=pl.BlockSpec((1, 2, t, t), lambda m: (m, 0, 0, 0)),
        out_shape=jax.ShapeDtypeStruct((n_maps, 2, t, t), F32),
        compiler_params=_params("parallel"),
        name="bias_tiles",
    )(table, buckets, mask)


def _softmax_pv(qi, t, q_ref, k_ref, v_ref, cols, far_bias, bias_ref):
    qc = q_ref[0, qi * t:(qi + 1) * t, cols]
    near_bias, diag_bias = bias_ref.at[1], bias_ref.at[0]
    pieces = []
    if qi >= 2:
        pieces.append((0, (qi - 1) * t, None))
    if qi >= 1:
        pieces.append(((qi - 1) * t, t, near_bias))
    pieces.append((qi * t, t, diag_bias))

    scores, row_max = [], []
    for start, size, bias in pieces:
        kc = k_ref[0, start:start + size, cols]
        s = lax.dot_general(qc, kc, (((1,), (1,)), ((), ())), preferred_element_type=F32)
        if bias is None:
            row_max.append(jnp.max(s, axis=-1, keepdims=True) + far_bias)
        else:
            s = s + bias[...]
            row_max.append(jnp.max(s, axis=-1, keepdims=True))
        scores.append(s)
    m = functools.reduce(jnp.maximum, row_max)
    yield
    probs = []
    for s, (_, _, bias) in zip(scores, pieces):
        probs.append(jnp.exp2(s - ((m - far_bias) if bias is None else m)))
        if bias is None:
            yield
    denom = sum(jnp.sum(p, axis=-1, keepdims=True) for p in probs)
    yield
    acc = sum(jnp.dot(p.astype(BF16), v_ref[0, start:start + size, :], preferred_element_type=F32)
              for p, (start, size, _) in zip(probs, pieces))
    return acc * (1.0 / denom)


def _run_staggered(tasks, n_phases, on_done):
    finished = set()
    for step in range(len(tasks) + n_phases - 1):
        for phase in range(n_phases):
            n = step - phase
            if 0 <= n < len(tasks) and n not in finished:
                try:
                    next(tasks[n])
                except StopIteration as done:
                    finished.add(n)
                    on_done(n, done.value)


def _attn_kernel(lambda_init, q_ref, k_ref, v_ref, bias_ref, table_ref, lam_ref, subg_ref, o_ref):
    head = pl.program_id(1)
    t = bias_ref.shape[-1]

    lam = lam_ref[...]
    lam_full = (jnp.exp(jnp.sum(lam[0:1] * lam[1:2], keepdims=True))
                - jnp.exp(jnp.sum(lam[2:3] * lam[3:4], keepdims=True)) + lambda_init)

    tiles = list(reversed(range(q_ref.shape[1] // t)))
    tasks = []
    for i in tiles:
        for c in range(2):
            cols = slice(c * HEAD_DIM, (c + 1) * HEAD_DIM)
            tasks.append(_softmax_pv(i, t, q_ref, k_ref, v_ref, cols,
                                     table_ref[FAR_BUCKET, 2 * head + c] * LOG2_E, bias_ref.at[c]))

    first_map = {}

    def on_done(n, value):
        if n % 2 == 0:
            first_map[n // 2] = value
            return
        i = tiles[n // 2]
        o = first_map.pop(n // 2) - lam_full * value
        o = o * lax.rsqrt(jnp.mean(o * o, axis=-1, keepdims=True) + SUBLN_EPS)
        o_ref[0, i * t:(i + 1) * t, :] = (o * subg_ref[...] * (1.0 - lambda_init)).astype(o_ref.dtype)

    _run_staggered(tasks, 4, on_done)


def _attention(q, k, v, bias_tiles, rel_bias, lam, subln_g, lambda_init):
    b, s, width = q.shape
    t = ATT_TILE
    n_heads = width // V_HEAD_DIM
    assert s % t == 0
    return pl.pallas_call(
        functools.partial(_attn_kernel, lambda_init),
        grid=(b, n_heads),
        in_specs=[
            pl.BlockSpec((1, s, V_HEAD_DIM), lambda bi, h: (bi, 0, h)),
            pl.BlockSpec((1, s, V_HEAD_DIM), lambda bi, h: (bi, 0, h)),
            pl.BlockSpec((1, s, V_HEAD_DIM), lambda bi, h: (bi, 0, h)),
            pl.BlockSpec((2, 2, t, t), lambda bi, h: (h, 0, 0, 0)),
            pl.BlockSpec(memory_space=pltpu.SMEM),
            pl.BlockSpec(lam.shape, lambda bi, h: (0, 0)),
            pl.BlockSpec((1, V_HEAD_DIM), lambda bi, h: (0, 0)),
        ],
        out_specs=pl.BlockSpec((1, s, V_HEAD_DIM), lambda bi, h: (bi, 0, h)),
        out_shape=jax.ShapeDtypeStruct((b, s, width), BF16),
        compiler_params=_params("parallel", "parallel"),
        name="diff_attention",
    )(q, k, v, bias_tiles, rel_bias, lam, subln_g)


def _out_proj_kernel(o_ref, w_ref, x_ref, g_ref, y_ref, wb_ref):
    @pl.when(pl.program_id(0) == 0)
    def _():
        wb_ref[...] = w_ref[...].astype(BF16)

    y_ref[...] = jnp.dot(o_ref[...], wb_ref[...], preferred_element_type=F32)
    gain = g_ref[...]

    def body(rows):
        y_ref[rows, :] = x_ref[rows, :] + _rms(y_ref[rows, :], gain, RMS_EPS)
    _for_row_chunks(y_ref.shape[0], body)


def _out_proj(o, w, x, gain):
    m, d = x.shape
    kdim = o.shape[1]
    tm = PROJ_ROWS
    assert m % tm == 0
    return pl.pallas_call(
        _out_proj_kernel,
        grid=(m // tm,),
        in_specs=[
            pl.BlockSpec((tm, kdim), lambda i: (i, 0)),
            pl.BlockSpec((kdim, d), lambda i: (0, 0), pipeline_mode=pl.Buffered(1)),
            pl.BlockSpec((tm, d), lambda i: (i, 0)),
            pl.BlockSpec((1, d), lambda i: (0, 0)),
        ],
        out_specs=pl.BlockSpec((tm, d), lambda i: (i, 0)),
        out_shape=jax.ShapeDtypeStruct((m, d), F32),
        scratch_shapes=[pltpu.VMEM((kdim, d), BF16)],
        compiler_params=_params("arbitrary"),
        name="out_proj",
    )(o, w, x, gain)


def kernel(x, norm_gains, ffn_w_gate, ffn_w_up, ffn_w_down, pool_w, pool_scale, kv_norm, w_k, w_v,
           rel_bias, w_q, w_o, lambdas, subln_gain):
    b, s, d = x.shape
    depth = norm_gains.shape[0]
    n_a = pool_w.shape[0]
    gains = norm_gains.reshape(depth, 3, 2, 1, d)
    wg, wu, wd = ffn_w_gate, ffn_w_up, ffn_w_down

    xf = x.reshape(b * s, d)
    k = v = bias_tiles = None
    for l in range(depth):
        g = gains[l]
        if l == n_a:
            k = _norm_proj(xf, kv_norm.reshape(1, d), w_k, 1.0)
            v = _norm_proj(xf, kv_norm.reshape(1, d), w_v, 1.0)
            bias_tiles = _bias_tiles(rel_bias, ATT_TILE)
        xf = _ffn(xf, g[0, 0], g[0, 1], wg, wu, wd, l, 0)
        if l < n_a:
            xf = _pool_mixer(xf.reshape(b, s, d), g[1, 0], g[1, 1], pool_w[l],
                             pool_scale[l].reshape(1, d)).reshape(b * s, d)
        else:
            j = l - n_a
            lambda_init = 0.8 - 0.6 * math.exp(-0.3 * l)
            q = _norm_proj(xf, g[1, 0], w_q[j], HEAD_DIM ** -0.5 * LOG2_E)
            o = _attention(q.reshape(b, s, -1), k.reshape(b, s, -1), v.reshape(b, s, -1), bias_tiles,
                           rel_bias, lambdas[j], subln_gain[j].reshape(1, -1), lambda_init)
            xf = _out_proj(o.reshape(b * s, -1), w_o[j], xf, g[1, 1])
        xf = _ffn(xf, g[2, 0], g[2, 1], wg, wu, wd, l, 1)
    return xf.reshape(b, s, d)
```

```python
import functools
import math

import numpy as np
import jax
import jax.numpy as jnp
from jax import lax
from jax.experimental import pallas as pl
from jax.experimental.pallas import tpu as pltpu

F32 = jnp.float32
BF16 = jnp.bfloat16

CHUNK = 64
POOL_WINDOWS = (2, 4, 8, 16)
HEAD_DIM = 128
V_HEAD_DIM = 2 * HEAD_DIM
NUM_BUCKETS = 32
MAX_DISTANCE = 128
RMS_EPS = 1e-6
SUBLN_EPS = 1e-5
LOG2_E = math.log2(math.e)

LANES = 128
SUBLANES = 8

VMEM_LIMIT_BYTES = 60 * 1024 * 1024
FFN_ROWS = 1024
FFN_COLS = 512
FFN_X_PREFETCH_STEP = 6
NORM_PROJ_ROWS = 1024
PROJ_ROWS = 512
PROJ_COLS = 1024
NORM_ROW_CHUNK = 16
POOL_ROWS = 512
POOL_STAGES = 4
POOL_HALO = SUBLANES * POOL_STAGES
ATT_TILE = 256


def _rms(x, gain, eps):
    ms = jnp.mean(x * x, axis=-1, keepdims=True)
    return x * lax.rsqrt(ms + eps) * gain


def _for_row_chunks(n_rows, body):
    for r0 in range(0, n_rows, NORM_ROW_CHUNK):
        body(slice(r0, r0 + NORM_ROW_CHUNK))


def _norm_rows_to(dst_ref, src_ref, gain, eps):
    def body(rows):
        dst_ref[rows, :] = _rms(src_ref[rows, :], gain, eps).astype(dst_ref.dtype)
    _for_row_chunks(src_ref.shape[0], body)


def _params(*semantics):
    return pltpu.CompilerParams(dimension_semantics=semantics,
                                vmem_limit_bytes=VMEM_LIMIT_BYTES)


def _ffn_kernel(x_hbm, gpre_ref, gpost_ref, wg_ref, wu_ref, wd_ref, o_hbm, h_ref, a_ref, acc_ref,
                x_buf, x_sem, o_sem):
    i = pl.program_id(0)
    j = pl.program_id(1)
    n_tiles = pl.num_programs(0)
    tm = acc_ref.shape[0]
    slot = i % 2
    x_ref = x_buf.at[slot]

    def x_copy(tile, to_slot):
        rows = pl.ds(pl.multiple_of(tile * tm, tm), tm)
        return pltpu.make_async_copy(x_hbm.at[rows, :], x_buf.at[to_slot], x_sem.at[to_slot])

    def out_copy(tile):
        rows = pl.ds(pl.multiple_of(tile * tm, tm), tm)
        return pltpu.make_async_copy(acc_ref, o_hbm.at[rows, :], o_sem.at[0])

    def gate_up():
        h = h_ref[...]
        g = jnp.dot(h, wg_ref[...].astype(BF16), preferred_element_type=F32)
        u = jnp.dot(h, wu_ref[...].astype(BF16), preferred_element_type=F32)
        return (g * (1.0 / (1.0 + jnp.exp(-g))) * u).astype(BF16)

    def down(a):
        return jnp.dot(a, wd_ref[...].astype(BF16), preferred_element_type=F32)

    @pl.when(jnp.logical_and(i == 0, j == 0))
    def _():
        x_copy(0, 0).start()

    @pl.when(j == 0)
    def _():
        x_copy(i, slot).wait()
        _norm_rows_to(h_ref, x_ref, gpre_ref[...], RMS_EPS)
        a_ref[...] = gate_up()

    @pl.when(jnp.logical_and(j == 0, i > 0))
    def _():
        out_copy(i - 1).wait()

    @pl.when(j == 0)
    def _():
        acc_ref[...] = down(a_ref[...])

    @pl.when(jnp.logical_and(j == FFN_X_PREFETCH_STEP, i + 1 < n_tiles))
    def _():
        x_copy(i + 1, 1 - slot).start()

    @pl.when(j > 0)
    def _():
        acc_ref[...] += down(gate_up())

    @pl.when(j == pl.num_programs(1) - 1)
    def _():
        half_gain = 0.5 * gpost_ref[...]

        def body(rows):
            acc_ref[rows, :] = x_ref[rows, :] + _rms(acc_ref[rows, :], half_gain, RMS_EPS)
        _for_row_chunks(tm, body)
        out_copy(i).start()

    @pl.when(jnp.logical_and(j == pl.num_programs(1) - 1, i == n_tiles - 1))
    def _():
        out_copy(i).wait()


def _ffn(x, gpre, gpost, wg, wu, wd, layer, half):
    m, d = x.shape
    ff = wg.shape[-1]
    tm, tf = FFN_ROWS, FFN_COLS
    assert m % tm == 0 and ff % tf == 0 and 0 < FFN_X_PREFETCH_STEP < ff // tf
    return pl.pallas_call(
        _ffn_kernel,
        grid=(m // tm, ff // tf),
        in_specs=[
            pl.BlockSpec(memory_space=pl.ANY),
            pl.BlockSpec((1, d), lambda i, j: (0, 0)),
            pl.BlockSpec((1, d), lambda i, j: (0, 0)),
            pl.BlockSpec((None, None, d, tf), lambda i, j: (layer, half, 0, j)),
            pl.BlockSpec((None, None, d, tf), lambda i, j: (layer, half, 0, j)),
            pl.BlockSpec((None, None, tf, d), lambda i, j: (layer, half, j, 0)),
        ],
        out_specs=pl.BlockSpec(memory_space=pl.ANY),
        out_shape=jax.ShapeDtypeStruct((m, d), F32),
        scratch_shapes=[pltpu.VMEM((tm, d), BF16), pltpu.VMEM((tm, tf), BF16),
                        pltpu.VMEM((tm, d), F32), pltpu.VMEM((2, tm, d), F32),
                        pltpu.SemaphoreType.DMA((2,)), pltpu.SemaphoreType.DMA((1,))],
        compiler_params=_params("arbitrary", "arbitrary"),
        name="ffn",
    )(x, gpre, gpost, wg, wu, wd)


def _pool_kernel(x_ref, halo_ref, gpre_ref, gpost_ref, w_ref, scale_ref, o_ref, wb_ref,
                 *sum_refs):
    i = pl.program_id(1)
    ts = x_ref.shape[1]
    d = x_ref.shape[2]
    gdim = w_ref.shape[1]
    rows_all = POOL_HALO + ts
    hx_ref = sum_refs[0]
    gpre = gpre_ref[...]

    @pl.when(jnp.logical_and(pl.program_id(0) == 0, i == 0))
    def _():
        wb_ref[...] = w_ref[...].astype(BF16)

    hh = _rms(halo_ref[0], gpre, RMS_EPS)
    hx_ref[0:POOL_HALO, :] = jnp.where(i > 0, hh, 0.0)
    for r0 in range(0, ts, NORM_ROW_CHUNK):
        hx_ref[POOL_HALO + r0:POOL_HALO + r0 + NORM_ROW_CHUNK, :] = _rms(
            x_ref[0, r0:r0 + NORM_ROW_CHUNK, :], gpre, RMS_EPS)

    for s in range(1, len(sum_refs)):
        shift = 2 ** (s - 1)
        r0 = SUBLANES * s
        c0 = (s - 1) * gdim
        prev = sum_refs[s - 1]
        sum_refs[s][r0:rows_all, c0:d] = (prev[r0:rows_all, c0:d]
                                          + prev[r0 - shift:rows_all - shift, c0:d])

    t = i * ts + lax.broadcasted_iota(jnp.int32, (ts, 1), 0)
    main = slice(POOL_HALO, rows_all)
    for gi, win in enumerate(POOL_WINDOWS):
        cols = slice(gi * gdim, (gi + 1) * gdim)
        stage = win.bit_length() - 1
        if stage < len(sum_refs):
            total = sum_refs[stage][main, cols]
        else:
            prev = sum_refs[stage - 1]
            half = win // 2
            total = prev[main, cols] + prev[POOL_HALO - half:rows_all - half, cols]
        inv_cnt = 1.0 / jnp.minimum(t + 1, win).astype(F32)
        dg = total * inv_cnt - hx_ref[main, cols]
        o_ref[0, :, cols] = (jnp.dot(dg.astype(BF16), wb_ref[gi], preferred_element_type=F32)
                             * scale_ref[:, cols])

    gpost = gpost_ref[...]
    for r0 in range(0, ts, NORM_ROW_CHUNK):
        rows = slice(r0, r0 + NORM_ROW_CHUNK)
        o_ref[0, rows, :] = x_ref[0, rows, :] + _rms(o_ref[0, rows, :], gpost, RMS_EPS)


def _pool_mixer(x, gpre, gpost, w, scale):
    b, s, d = x.shape
    ts = POOL_ROWS
    assert s % ts == 0 and ts % POOL_HALO == 0
    assert POOL_WINDOWS == tuple(2 ** (g + 1) for g in range(POOL_STAGES))
    assert w.shape[0] == len(POOL_WINDOWS) and w.shape[1] * w.shape[0] == d
    halo_blocks = ts // POOL_HALO
    return pl.pallas_call(
        _pool_kernel,
        grid=(b, s // ts),
        in_specs=[
            pl.BlockSpec((1, ts, d), lambda bi, i: (bi, i, 0)),
            pl.BlockSpec((1, POOL_HALO, d),
                         lambda bi, i: (bi, jnp.maximum(i * halo_blocks - 1, 0), 0)),
            pl.BlockSpec((1, d), lambda bi, i: (0, 0)),
            pl.BlockSpec((1, d), lambda bi, i: (0, 0)),
            pl.BlockSpec(w.shape, lambda bi, i: (0, 0, 0), pipeline_mode=pl.Buffered(1)),
            pl.BlockSpec((1, d), lambda bi, i: (0, 0)),
        ],
        out_specs=pl.BlockSpec((1, ts, d), lambda bi, i: (bi, i, 0)),
        out_shape=jax.ShapeDtypeStruct((b, s, d), F32),
        scratch_shapes=[pltpu.VMEM(w.shape, BF16)]
                       + [pltpu.VMEM((POOL_HALO + ts, d), F32)] * POOL_STAGES,
        compiler_params=_params("arbitrary", "arbitrary"),
        name="pool_mixer",
    )(x, x, gpre, gpost, w, scale)


def _norm_proj_kernel(out_scale, transposed, x_ref, g_ref, w_ref, o_ref, h_ref, wb_ref):
    @pl.when(pl.program_id(0) == 0)
    def _():
        wb_ref[...] = w_ref[...].astype(BF16)

    _norm_rows_to(h_ref, x_ref, g_ref[...], RMS_EPS)
    h = h_ref[...]
    for c0 in range(0, wb_ref.shape[1], PROJ_COLS):
        cols = slice(c0, c0 + PROJ_COLS)
        y = jnp.dot(h, wb_ref[:, cols], preferred_element_type=F32)
        if out_scale != 1.0:
            y = y * out_scale
        if transposed:
            o_ref[cols, :] = y.T.astype(o_ref.dtype)
        else:
            o_ref[:, cols] = y.astype(o_ref.dtype)


def _norm_proj(x, gain, w, out_scale, transposed=False):
    m, d = x.shape
    n = w.shape[1]
    tm = NORM_PROJ_ROWS
    assert m % tm == 0 and n % PROJ_COLS == 0
    if transposed:
        out_spec = pl.BlockSpec((n, tm), lambda i: (0, i))
        out_shape = jax.ShapeDtypeStruct((n, m), BF16)
    else:
        out_spec = pl.BlockSpec((tm, n), lambda i: (i, 0))
        out_shape = jax.ShapeDtypeStruct((m, n), BF16)
    return pl.pallas_call(
        functools.partial(_norm_proj_kernel, out_scale, transposed),
        grid=(m // tm,),
        in_specs=[pl.BlockSpec((tm, d), lambda i: (i, 0)),
                  pl.BlockSpec((1, d), lambda i: (0, 0)),
                  pl.BlockSpec((d, n), lambda i: (0, 0), pipeline_mode=pl.Buffered(1))],
        out_specs=out_spec,
        out_shape=out_shape,
        scratch_shapes=[pltpu.VMEM((tm, d), BF16), pltpu.VMEM((d, n), BF16)],
        compiler_params=_params("arbitrary"),
        name="norm_proj",
    )(x, gain, w)


def _rel_bucket(rel):
    half = NUM_BUCKETS // 2
    max_exact = half // 2
    ret = jnp.where(rel > 0, half, 0)
    n = jnp.abs(rel)
    nf = jnp.maximum(n, 1).astype(F32)
    large = max_exact + (jnp.log(nf / max_exact) / math.log(MAX_DISTANCE / max_exact)
                         * (half - max_exact)).astype(jnp.int32)
    large = jnp.minimum(large, half - 1)
    return ret + jnp.where(n < max_exact, n, large)


FAR_BUCKET = NUM_BUCKETS // 2 - 1
FAR_DISTANCE = MAX_DISTANCE


def _bias_tile_constants(t):
    assert t + 1 >= FAR_DISTANCE and t % CHUNK == 0
    q = np.arange(t, dtype=np.int32)[:, None]
    k = np.arange(t, dtype=np.int32)[None, :]
    buckets = jnp.stack([_rel_bucket(jnp.asarray(k - q)), _rel_bucket(jnp.asarray(k - t - q))])
    allowed = np.stack([(k // CHUNK) <= (q // CHUNK), np.ones((t, t), bool)])
    mask = np.where(allowed, 0.0, -np.inf).astype(np.float32)
    return buckets, jnp.asarray(mask)


def _bias_kernel(table_ref, bucket_ref, mask_ref, o_ref):
    t = bucket_ref.shape[-1]
    table = jnp.broadcast_to(table_ref[0], (t, LANES))
    for tile in range(bucket_ref.shape[0]):
        for c0 in range(0, t, LANES):
            cols = slice(c0, c0 + LANES)
            looked_up = jnp.take_along_axis(table, bucket_ref[tile, :, cols], axis=1)
            o_ref[0, tile, :, cols] = looked_up * LOG2_E + mask_ref[tile, :, cols]


def _bias_tiles(rel_bias, t):
    buckets, mask = _bias_tile_constants(t)
    n_buckets, n_maps = rel_bias.shape
    assert n_buckets <= LANES and t % LANES == 0
    table = jnp.pad(rel_bias.T, ((0, 0), (0, LANES - n_buckets))).reshape(n_maps, 1, LANES)
    return pl.pallas_call(
        _bias_kernel,
        grid=(n_maps,),
        in_specs=[
            pl.BlockSpec((1, 1, LANES), lambda m: (m, 0, 0)),
            pl.BlockSpec((2, t, t), lambda m: (0, 0, 0)),
            pl.BlockSpec((2, t, t), lambda m: (0, 0, 0)),
        ],
        out_specs=pl.BlockSpec((1, 2, t, t), lambda m: (m, 0, 0, 0)),
        out_shape=jax.ShapeDtypeStruct((n_maps, 2, t, t), F32),
        compiler_params=_params("parallel"),
        name="bias_tiles",
    )(table, buckets, mask)


def _softmax_pv(qi, t, q_ref, k_ref, v_ref, cols, far_bias, bias_ref):
    qc = q_ref[0, qi * t:(qi + 1) * t, cols]
    near_bias, diag_bias = bias_ref.at[1], bias_ref.at[0]
    pieces = []
    if qi >= 2:
        pieces.append((0, (qi - 1) * t, None))
    if qi >= 1:
        pieces.append(((qi - 1) * t, t, near_bias))
    pieces.append((qi * t, t, diag_bias))

    scores, row_max = [], []
    for start, size, bias in pieces:
        s = jnp.dot(qc, k_ref[cols, start:start + size], preferred_element_type=F32)
        if bias is None:
            row_max.append(jnp.max(s, axis=-1, keepdims=True) + far_bias)
        else:
            s = s + bias[...]
            row_max.append(jnp.max(s, axis=-1, keepdims=True))
        scores.append(s)
    m = functools.reduce(jnp.maximum, row_max)
    yield
    probs = []
    for s, (_, _, bias) in zip(scores, pieces):
        probs.append(jnp.exp2(s - ((m - far_bias) if bias is None else m)))
        if bias is None:
            yield
    denom = sum(jnp.sum(p, axis=-1, keepdims=True) for p in probs)
    yield
    acc = sum(jnp.dot(p.astype(BF16), v_ref[0, start:start + size, :], preferred_element_type=F32)
              for p, (start, size, _) in zip(probs, pieces))
    return acc * (1.0 / denom)


def _run_staggered(tasks, n_phases, on_done):
    finished = set()
    for step in range(len(tasks) + n_phases - 1):
        for phase in range(n_phases):
            n = step - phase
            if 0 <= n < len(tasks) and n not in finished:
                try:
                    next(tasks[n])
                except StopIteration as done:
                    finished.add(n)
                    on_done(n, done.value)


def _attn_kernel(lambda_init, q_ref, k_ref, v_ref, bias_ref, table_ref, lam_ref, subg_ref, o_ref):
    head = pl.program_id(1)
    t = bias_ref.shape[-1]

    lam = lam_ref[...]
    lam_full = (jnp.exp(jnp.sum(lam[0:1] * lam[1:2], keepdims=True))
                - jnp.exp(jnp.sum(lam[2:3] * lam[3:4], keepdims=True)) + lambda_init)

    tiles = list(reversed(range(q_ref.shape[1] // t)))
    tasks = []
    for i in tiles:
        for c in range(2):
            cols = slice(c * HEAD_DIM, (c + 1) * HEAD_DIM)
            tasks.append(_softmax_pv(i, t, q_ref, k_ref, v_ref, cols,
                                     table_ref[FAR_BUCKET, 2 * head + c] * LOG2_E, bias_ref.at[c]))

    first_map = {}

    def on_done(n, value):
        if n % 2 == 0:
            first_map[n // 2] = value
            return
        i = tiles[n // 2]
        o = first_map.pop(n // 2) - lam_full * value
        o = o * lax.rsqrt(jnp.mean(o * o, axis=-1, keepdims=True) + SUBLN_EPS)
        o_ref[0, i * t:(i + 1) * t, :] = (o * subg_ref[...] * (1.0 - lambda_init)).astype(o_ref.dtype)

    _run_staggered(tasks, 4, on_done)


def _attention(q, k, v, bias_tiles, rel_bias, lam, subln_g, lambda_init):
    b, s, width = q.shape
    t = ATT_TILE
    n_heads = width // V_HEAD_DIM
    assert s % t == 0
    return pl.pallas_call(
        functools.partial(_attn_kernel, lambda_init),
        grid=(b, n_heads),
        in_specs=[
            pl.BlockSpec((1, s, V_HEAD_DIM), lambda bi, h: (bi, 0, h)),
            pl.BlockSpec((V_HEAD_DIM, s), lambda bi, h: (h, bi)),
            pl.BlockSpec((1, s, V_HEAD_DIM), lambda bi, h: (bi, 0, h)),
            pl.BlockSpec((2, 2, t, t), lambda bi, h: (h, 0, 0, 0)),
            pl.BlockSpec(memory_space=pltpu.SMEM),
            pl.BlockSpec(lam.shape, lambda bi, h: (0, 0)),
            pl.BlockSpec((1, V_HEAD_DIM), lambda bi, h: (0, 0)),
        ],
        out_specs=pl.BlockSpec((1, s, V_HEAD_DIM), lambda bi, h: (bi, 0, h)),
        out_shape=jax.ShapeDtypeStruct((b, s, width), BF16),
        compiler_params=_params("parallel", "parallel"),
        name="diff_attention",
    )(q, k, v, bias_tiles, rel_bias, lam, subln_g)


def _out_proj_kernel(o_ref, w_ref, x_ref, g_ref, y_ref, wb_ref):
    @pl.when(pl.program_id(0) == 0)
    def _():
        wb_ref[...] = w_ref[...].astype(BF16)

    y_ref[...] = jnp.dot(o_ref[...], wb_ref[...], preferred_element_type=F32)
    gain = g_ref[...]

    def body(rows):
        y_ref[rows, :] = x_ref[rows, :] + _rms(y_ref[rows, :], gain, RMS_EPS)
    _for_row_chunks(y_ref.shape[0], body)


def _out_proj(o, w, x, gain):
    m, d = x.shape
    kdim = o.shape[1]
    tm = PROJ_ROWS
    assert m % tm == 0
    return pl.pallas_call(
        _out_proj_kernel,
        grid=(m // tm,),
        in_specs=[
            pl.BlockSpec((tm, kdim), lambda i: (i, 0)),
            pl.BlockSpec((kdim, d), lambda i: (0, 0), pipeline_mode=pl.Buffered(1)),
            pl.BlockSpec((tm, d), lambda i: (i, 0)),
            pl.BlockSpec((1, d), lambda i: (0, 0)),
        ],
        out_specs=pl.BlockSpec((tm, d), lambda i: (i, 0)),
        out_shape=jax.ShapeDtypeStruct((m, d), F32),
        scratch_shapes=[pltpu.VMEM((kdim, d), BF16)],
        compiler_params=_params("arbitrary"),
        name="out_proj",
    )(o, w, x, gain)


def kernel(x, norm_gains, ffn_w_gate, ffn_w_up, ffn_w_down, pool_w, pool_scale, kv_norm, w_k, w_v,
           rel_bias, w_q, w_o, lambdas, subln_gain):
    b, s, d = x.shape
    depth = norm_gains.shape[0]
    n_a = pool_w.shape[0]
    gains = norm_gains.reshape(depth, 3, 2, 1, d)
    wg, wu, wd = ffn_w_gate, ffn_w_up, ffn_w_down

    xf = x.reshape(b * s, d)
    k = v = bias_tiles = None
    for l in range(depth):
        g = gains[l]
        if l == n_a:
            k = _norm_proj(xf, kv_norm.reshape(1, d), w_k, 1.0, transposed=True)
            v = _norm_proj(xf, kv_norm.reshape(1, d), w_v, 1.0)
            bias_tiles = _bias_tiles(rel_bias, ATT_TILE)
        xf = _ffn(xf, g[0, 0], g[0, 1], wg, wu, wd, l, 0)
        if l < n_a:
            xf = _pool_mixer(xf.reshape(b, s, d), g[1, 0], g[1, 1], pool_w[l],
                             pool_scale[l].reshape(1, d)).reshape(b * s, d)
        else:
            j = l - n_a
            lambda_init = 0.8 - 0.6 * math.exp(-0.3 * l)
            q = _norm_proj(xf, g[1, 0], w_q[j], HEAD_DIM ** -0.5 * LOG2_E)
            o = _attention(q.reshape(b, s, -1), k, v.reshape(b, s, -1), bias_tiles,
                           rel_bias, lambdas[j], subln_gain[j].reshape(1, -1), lambda_init)
            xf = _out_proj(o.reshape(b * s, -1), w_o[j], xf, g[1, 1])
        xf = _ffn(xf, g[2, 0], g[2, 1], wg, wu, wd, l, 1)
    return xf.reshape(b, s, d)
```

```python
import functools
import math

import numpy as np
import jax
import jax.numpy as jnp
from jax import lax
from jax.experimental import pallas as pl
from jax.experimental.pallas import tpu as pltpu

F32 = jnp.float32
BF16 = jnp.bfloat16

CHUNK = 64
POOL_WINDOWS = (2, 4, 8, 16)
HEAD_DIM = 128
V_HEAD_DIM = 2 * HEAD_DIM
NUM_BUCKETS = 32
MAX_DISTANCE = 128
RMS_EPS = 1e-6
SUBLN_EPS = 1e-5
LOG2_E = math.log2(math.e)

LANES = 128
SUBLANES = 8

VMEM_LIMIT_BYTES = 60 * 1024 * 1024
FFN_ROWS = 1024
FFN_COLS = 512
FFN_X_PREFETCH_STEP = 6
NORM_PROJ_ROWS = 1024
PROJ_ROWS = 512
PROJ_COLS = 1024
NORM_ROW_CHUNK = 16
POOL_ROWS = 512
POOL_STAGES = 4
POOL_HALO = SUBLANES * POOL_STAGES
ATT_TILE = 256
BIAS_MAPS_PER_STEP = 8


def _rms(x, gain, eps):
    ms = jnp.mean(x * x, axis=-1, keepdims=True)
    return x * lax.rsqrt(ms + eps) * gain


def _for_row_chunks(n_rows, body):
    for r0 in range(0, n_rows, NORM_ROW_CHUNK):
        body(slice(r0, r0 + NORM_ROW_CHUNK))


def _norm_rows_to(dst_ref, src_ref, gain, eps):
    def body(rows):
        dst_ref[rows, :] = _rms(src_ref[rows, :], gain, eps).astype(dst_ref.dtype)
    _for_row_chunks(src_ref.shape[0], body)


def _params(*semantics):
    return pltpu.CompilerParams(dimension_semantics=semantics,
                                vmem_limit_bytes=VMEM_LIMIT_BYTES)


def _ffn_kernel(x_hbm, gpre_ref, gpost_ref, wg_ref, wu_ref, wd_ref, o_hbm, h_ref, a_ref, acc_ref,
                x_buf, x_sem, o_sem):
    i = pl.program_id(0)
    j = pl.program_id(1)
    n_tiles = pl.num_programs(0)
    tm = acc_ref.shape[0]
    slot = i % 2
    x_ref = x_buf.at[slot]

    def x_copy(tile, to_slot):
        rows = pl.ds(pl.multiple_of(tile * tm, tm), tm)
        return pltpu.make_async_copy(x_hbm.at[rows, :], x_buf.at[to_slot], x_sem.at[to_slot])

    def out_copy(tile):
        rows = pl.ds(pl.multiple_of(tile * tm, tm), tm)
        return pltpu.make_async_copy(acc_ref, o_hbm.at[rows, :], o_sem.at[0])

    def gate_up():
        h = h_ref[...]
        g = jnp.dot(h, wg_ref[...].astype(BF16), preferred_element_type=F32)
        u = jnp.dot(h, wu_ref[...].astype(BF16), preferred_element_type=F32)
        return (g * (1.0 / (1.0 + jnp.exp(-g))) * u).astype(BF16)

    def down(a):
        return jnp.dot(a, wd_ref[...].astype(BF16), preferred_element_type=F32)

    @pl.when(jnp.logical_and(i == 0, j == 0))
    def _():
        x_copy(0, 0).start()

    @pl.when(j == 0)
    def _():
        x_copy(i, slot).wait()
        _norm_rows_to(h_ref, x_ref, gpre_ref[...], RMS_EPS)
        a_ref[...] = gate_up()

    @pl.when(jnp.logical_and(j == 0, i > 0))
    def _():
        out_copy(i - 1).wait()

    @pl.when(j == 0)
    def _():
        acc_ref[...] = down(a_ref[...])

    @pl.when(jnp.logical_and(j == FFN_X_PREFETCH_STEP, i + 1 < n_tiles))
    def _():
        x_copy(i + 1, 1 - slot).start()

    @pl.when(j > 0)
    def _():
        acc_ref[...] += down(gate_up())

    @pl.when(j == pl.num_programs(1) - 1)
    def _():
        half_gain = 0.5 * gpost_ref[...]

        def body(rows):
            acc_ref[rows, :] = x_ref[rows, :] + _rms(acc_ref[rows, :], half_gain, RMS_EPS)
        _for_row_chunks(tm, body)
        out_copy(i).start()

    @pl.when(jnp.logical_and(j == pl.num_programs(1) - 1, i == n_tiles - 1))
    def _():
        out_copy(i).wait()


def _ffn(x, gpre, gpost, wg, wu, wd, layer, half):
    m, d = x.shape
    ff = wg.shape[-1]
    tm, tf = FFN_ROWS, FFN_COLS
    assert m % tm == 0 and ff % tf == 0 and 0 < FFN_X_PREFETCH_STEP < ff // tf
    return pl.pallas_call(
        _ffn_kernel,
        grid=(m // tm, ff // tf),
        in_specs=[
            pl.BlockSpec(memory_space=pl.ANY),
            pl.BlockSpec((1, d), lambda i, j: (0, 0)),
            pl.BlockSpec((1, d), lambda i, j: (0, 0)),
            pl.BlockSpec((None, None, d, tf), lambda i, j: (layer, half, 0, j)),
            pl.BlockSpec((None, None, d, tf), lambda i, j: (layer, half, 0, j)),
            pl.BlockSpec((None, None, tf, d), lambda i, j: (layer, half, j, 0)),
        ],
        out_specs=pl.BlockSpec(memory_space=pl.ANY),
        out_shape=jax.ShapeDtypeStruct((m, d), F32),
        scratch_shapes=[pltpu.VMEM((tm, d), BF16), pltpu.VMEM((tm, tf), BF16),
                        pltpu.VMEM((tm, d), F32), pltpu.VMEM((2, tm, d), F32),
                        pltpu.SemaphoreType.DMA((2,)), pltpu.SemaphoreType.DMA((1,))],
        compiler_params=_params("arbitrary", "arbitrary"),
        name="ffn",
    )(x, gpre, gpost, wg, wu, wd)


def _pool_kernel(x_ref, halo_ref, gpre_ref, gpost_ref, w_ref, scale_ref, o_ref, wb_ref,
                 *sum_refs):
    i = pl.program_id(1)
    ts = x_ref.shape[1]
    d = x_ref.shape[2]
    gdim = w_ref.shape[1]
    rows_all = POOL_HALO + ts
    hx_ref = sum_refs[0]
    gpre = gpre_ref[...]

    @pl.when(jnp.logical_and(pl.program_id(0) == 0, i == 0))
    def _():
        wb_ref[...] = w_ref[...].astype(BF16)

    hh = _rms(halo_ref[0], gpre, RMS_EPS)
    hx_ref[0:POOL_HALO, :] = jnp.where(i > 0, hh, 0.0)
    for r0 in range(0, ts, NORM_ROW_CHUNK):
        hx_ref[POOL_HALO + r0:POOL_HALO + r0 + NORM_ROW_CHUNK, :] = _rms(
            x_ref[0, r0:r0 + NORM_ROW_CHUNK, :], gpre, RMS_EPS)

    for s in range(1, len(sum_refs)):
        shift = 2 ** (s - 1)
        r0 = SUBLANES * s
        c0 = (s - 1) * gdim
        prev = sum_refs[s - 1]
        sum_refs[s][r0:rows_all, c0:d] = (prev[r0:rows_all, c0:d]
                                          + prev[r0 - shift:rows_all - shift, c0:d])

    t = i * ts + lax.broadcasted_iota(jnp.int32, (ts, 1), 0)
    main = slice(POOL_HALO, rows_all)
    for gi, win in enumerate(POOL_WINDOWS):
        cols = slice(gi * gdim, (gi + 1) * gdim)
        stage = win.bit_length() - 1
        if stage < len(sum_refs):
            total = sum_refs[stage][main, cols]
        else:
            prev = sum_refs[stage - 1]
            half = win // 2
            total = prev[main, cols] + prev[POOL_HALO - half:rows_all - half, cols]
        inv_cnt = 1.0 / jnp.minimum(t + 1, win).astype(F32)
        dg = total * inv_cnt - hx_ref[main, cols]
        o_ref[0, :, cols] = (jnp.dot(dg.astype(BF16), wb_ref[gi], preferred_element_type=F32)
                             * scale_ref[:, cols])

    gpost = gpost_ref[...]
    for r0 in range(0, ts, NORM_ROW_CHUNK):
        rows = slice(r0, r0 + NORM_ROW_CHUNK)
        o_ref[0, rows, :] = x_ref[0, rows, :] + _rms(o_ref[0, rows, :], gpost, RMS_EPS)


def _pool_mixer(x, gpre, gpost, w, scale):
    b, s, d = x.shape
    ts = POOL_ROWS
    assert s % ts == 0 and ts % POOL_HALO == 0
    assert POOL_WINDOWS == tuple(2 ** (g + 1) for g in range(POOL_STAGES))
    assert w.shape[0] == len(POOL_WINDOWS) and w.shape[1] * w.shape[0] == d
    halo_blocks = ts // POOL_HALO
    return pl.pallas_call(
        _pool_kernel,
        grid=(b, s // ts),
        in_specs=[
            pl.BlockSpec((1, ts, d), lambda bi, i: (bi, i, 0)),
            pl.BlockSpec((1, POOL_HALO, d),
                         lambda bi, i: (bi, jnp.maximum(i * halo_blocks - 1, 0), 0)),
            pl.BlockSpec((1, d), lambda bi, i: (0, 0)),
            pl.BlockSpec((1, d), lambda bi, i: (0, 0)),
            pl.BlockSpec(w.shape, lambda bi, i: (0, 0, 0), pipeline_mode=pl.Buffered(1)),
            pl.BlockSpec((1, d), lambda bi, i: (0, 0)),
        ],
        out_specs=pl.BlockSpec((1, ts, d), lambda bi, i: (bi, i, 0)),
        out_shape=jax.ShapeDtypeStruct((b, s, d), F32),
        scratch_shapes=[pltpu.VMEM(w.shape, BF16)]
                       + [pltpu.VMEM((POOL_HALO + ts, d), F32)] * POOL_STAGES,
        compiler_params=_params("arbitrary", "arbitrary"),
        name="pool_mixer",
    )(x, x, gpre, gpost, w, scale)


def _norm_proj_kernel(out_scale, x_ref, g_ref, w_ref, o_ref, h_ref, wb_ref):
    @pl.when(pl.program_id(0) == 0)
    def _():
        wb_ref[...] = w_ref[...].astype(BF16)

    _norm_rows_to(h_ref, x_ref, g_ref[...], RMS_EPS)
    h = h_ref[...]
    for c0 in range(0, o_ref.shape[1], PROJ_COLS):
        cols = slice(c0, c0 + PROJ_COLS)
        y = jnp.dot(h, wb_ref[:, cols], preferred_element_type=F32)
        if out_scale != 1.0:
            y = y * out_scale
        o_ref[:, cols] = y.astype(o_ref.dtype)


def _norm_proj(x, gain, w, out_scale):
    m, d = x.shape
    n = w.shape[1]
    tm = NORM_PROJ_ROWS
    assert m % tm == 0 and n % PROJ_COLS == 0
    return pl.pallas_call(
        functools.partial(_norm_proj_kernel, out_scale),
        grid=(m // tm,),
        in_specs=[pl.BlockSpec((tm, d), lambda i: (i, 0)),
                  pl.BlockSpec((1, d), lambda i: (0, 0)),
                  pl.BlockSpec((d, n), lambda i: (0, 0), pipeline_mode=pl.Buffered(1))],
        out_specs=pl.BlockSpec((tm, n), lambda i: (i, 0)),
        out_shape=jax.ShapeDtypeStruct((m, n), BF16),
        scratch_shapes=[pltpu.VMEM((tm, d), BF16), pltpu.VMEM((d, n), BF16)],
        compiler_params=_params("arbitrary"),
        name="norm_proj",
    )(x, gain, w)


def _rel_bucket(rel):
    half = NUM_BUCKETS // 2
    max_exact = half // 2
    ret = jnp.where(rel > 0, half, 0)
    n = jnp.abs(rel)
    nf = jnp.maximum(n, 1).astype(F32)
    large = max_exact + (jnp.log(nf / max_exact) / math.log(MAX_DISTANCE / max_exact)
                         * (half - max_exact)).astype(jnp.int32)
    large = jnp.minimum(large, half - 1)
    return ret + jnp.where(n < max_exact, n, large)


FAR_BUCKET = NUM_BUCKETS // 2 - 1
FAR_DISTANCE = MAX_DISTANCE


def _bias_tile_constants(t):
    assert t + 1 >= FAR_DISTANCE and t % CHUNK == 0
    q = np.arange(t, dtype=np.int32)[:, None]
    k = np.arange(t, dtype=np.int32)[None, :]
    buckets = jnp.stack([_rel_bucket(jnp.asarray(k - q)), _rel_bucket(jnp.asarray(k - t - q))])
    allowed = np.stack([(k // CHUNK) <= (q // CHUNK), np.ones((t, t), bool)])
    mask = np.where(allowed, 0.0, -np.inf).astype(np.float32)
    return buckets, jnp.asarray(mask)


def _bias_kernel(table_ref, bucket_ref, mask_ref, o_ref):
    t = bucket_ref.shape[-1]
    for m in range(o_ref.shape[0]):
        table = jnp.broadcast_to(table_ref[m], (t, LANES))
        for tile in range(bucket_ref.shape[0]):
            for c0 in range(0, t, LANES):
                cols = slice(c0, c0 + LANES)
                looked_up = jnp.take_along_axis(table, bucket_ref[tile, :, cols], axis=1)
                o_ref[m, tile, :, cols] = looked_up * LOG2_E + mask_ref[tile, :, cols]


def _bias_tiles(rel_bias, t):
    buckets, mask = _bias_tile_constants(t)
    n_buckets, n_maps = rel_bias.shape
    assert n_buckets <= LANES and t % LANES == 0 and n_maps % BIAS_MAPS_PER_STEP == 0
    table = jnp.pad(rel_bias.T, ((0, 0), (0, LANES - n_buckets))).reshape(n_maps, 1, LANES)
    return pl.pallas_call(
        _bias_kernel,
        grid=(n_maps // BIAS_MAPS_PER_STEP,),
        in_specs=[
            pl.BlockSpec((BIAS_MAPS_PER_STEP, 1, LANES), lambda m: (m, 0, 0)),
            pl.BlockSpec((2, t, t), lambda m: (0, 0, 0)),
            pl.BlockSpec((2, t, t), lambda m: (0, 0, 0)),
        ],
        out_specs=pl.BlockSpec((BIAS_MAPS_PER_STEP, 2, t, t), lambda m: (m, 0, 0, 0)),
        out_shape=jax.ShapeDtypeStruct((n_maps, 2, t, t), F32),
        compiler_params=_params("parallel"),
        name="bias_tiles",
    )(table, buckets, mask)


def _softmax_pv(qi, t, q_ref, k_ref, v_ref, cols, far_bias, bias_ref):
    qc = q_ref[0, qi * t:(qi + 1) * t, cols]
    near_bias, diag_bias = bias_ref.at[1], bias_ref.at[0]
    pieces = []
    if qi >= 2:
        pieces.append((0, (qi - 1) * t, None))
    if qi >= 1:
        pieces.append(((qi - 1) * t, t, near_bias))
    pieces.append((qi * t, t, diag_bias))

    scores, row_max = [], []
    for start, size, bias in pieces:
        kc = k_ref[0, start:start + size, cols]
        s = lax.dot_general(qc, kc, (((1,), (1,)), ((), ())), preferred_element_type=F32)
        if bias is None:
            row_max.append(jnp.max(s, axis=-1, keepdims=True) + far_bias)
        else:
            s = s + bias[...]
            row_max.append(jnp.max(s, axis=-1, keepdims=True))
        scores.append(s)
    m = functools.reduce(jnp.maximum, row_max)
    yield
    probs = []
    for s, (_, _, bias) in zip(scores, pieces):
        probs.append(jnp.exp2(s - ((m - far_bias) if bias is None else m)))
        if bias is None:
            yield
    denom = sum(jnp.sum(p, axis=-1, keepdims=True) for p in probs)
    yield
    acc = sum(jnp.dot(p.astype(BF16), v_ref[0, start:start + size, :], preferred_element_type=F32)
              for p, (start, size, _) in zip(probs, pieces))
    return acc * (1.0 / denom)


def _run_staggered(tasks, n_phases, on_done):
    finished = set()
    for step in range(len(tasks) + n_phases - 1):
        for phase in range(n_phases):
            n = step - phase
            if 0 <= n < len(tasks) and n not in finished:
                try:
                    next(tasks[n])
                except StopIteration as done:
                    finished.add(n)
                    on_done(n, done.value)


def _attn_kernel(lambda_init, q_ref, k_ref, v_ref, bias_ref, table_ref, lam_ref, subg_ref, o_ref):
    head = pl.program_id(1)
    t = bias_ref.shape[-1]

    lam = lam_ref[...]
    lam_full = (jnp.exp(jnp.sum(lam[0:1] * lam[1:2], keepdims=True))
                - jnp.exp(jnp.sum(lam[2:3] * lam[3:4], keepdims=True)) + lambda_init)

    tiles = list(reversed(range(q_ref.shape[1] // t)))
    tasks = []
    for i in tiles:
        for c in range(2):
            cols = slice(c * HEAD_DIM, (c + 1) * HEAD_DIM)
            tasks.append(_softmax_pv(i, t, q_ref, k_ref, v_ref, cols,
                                     table_ref[FAR_BUCKET, 2 * head + c] * LOG2_E, bias_ref.at[c]))

    first_map = {}

    def on_done(n, value):
        if n % 2 == 0:
            first_map[n // 2] = value
            return
        i = tiles[n // 2]
        o = first_map.pop(n // 2) - lam_full * value
        o = o * lax.rsqrt(jnp.mean(o * o, axis=-1, keepdims=True) + SUBLN_EPS)
        o_ref[0, i * t:(i + 1) * t, :] = (o * subg_ref[...] * (1.0 - lambda_init)).astype(o_ref.dtype)

    _run_staggered(tasks, 4, on_done)


def _attention(q, k, v, bias_tiles, rel_bias, lam, subln_g, lambda_init):
    b, s, width = q.shape
    t = ATT_TILE
    n_heads = width // V_HEAD_DIM
    assert s % t == 0
    return pl.pallas_call(
        functools.partial(_attn_kernel, lambda_init),
        grid=(b, n_heads),
        in_specs=[
            pl.BlockSpec((1, s, V_HEAD_DIM), lambda bi, h: (bi, 0, h)),
            pl.BlockSpec((1, s, V_HEAD_DIM), lambda bi, h: (bi, 0, h)),
            pl.BlockSpec((1, s, V_HEAD_DIM), lambda bi, h: (bi, 0, h)),
            pl.BlockSpec((2, 2, t, t), lambda bi, h: (h, 0, 0, 0)),
            pl.BlockSpec(memory_space=pltpu.SMEM),
            pl.BlockSpec(lam.shape, lambda bi, h: (0, 0)),
            pl.BlockSpec((1, V_HEAD_DIM), lambda bi, h: (0, 0)),
        ],
        out_specs=pl.BlockSpec((1, s, V_HEAD_DIM), lambda bi, h: (bi, 0, h)),
        out_shape=jax.ShapeDtypeStruct((b, s, width), BF16),
        compiler_params=_params("parallel", "parallel"),
        name="diff_attention",
    )(q, k, v, bias_tiles, rel_bias, lam, subln_g)


def _out_proj_kernel(o_ref, w_ref, x_ref, g_ref, y_ref, wb_ref):
    @pl.when(pl.program_id(0) == 0)
    def _():
        wb_ref[...] = w_ref[...].astype(BF16)

    y_ref[...] = jnp.dot(o_ref[...], wb_ref[...], preferred_element_type=F32)
    gain = g_ref[...]

    def body(rows):
        y_ref[rows, :] = x_ref[rows, :] + _rms(y_ref[rows, :], gain, RMS_EPS)
    _for_row_chunks(y_ref.shape[0], body)


def _out_proj(o, w, x, gain):
    m, d = x.shape
    kdim = o.shape[1]
    tm = PROJ_ROWS
    assert m % tm == 0
    return pl.pallas_call(
        _out_proj_kernel,
        grid=(m // tm,),
        in_specs=[
            pl.BlockSpec((tm, kdim), lambda i: (i, 0)),
            pl.BlockSpec((kdim, d), lambda i: (0, 0), pipeline_mode=pl.Buffered(1)),
            pl.BlockSpec((tm, d), lambda i: (i, 0)),
            pl.BlockSpec((1, d), lambda i: (0, 0)),
        ],
        out_specs=pl.BlockSpec((tm, d), lambda i: (i, 0)),
        out_shape=jax.ShapeDtypeStruct((m, d), F32),
        scratch_shapes=[pltpu.VMEM((kdim, d), BF16)],
        compiler_params=_params("arbitrary"),
        name="out_proj",
    )(o, w, x, gain)


def kernel(x, norm_gains, ffn_w_gate, ffn_w_up, ffn_w_down, pool_w, pool_scale, kv_norm, w_k, w_v,
           rel_bias, w_q, w_o, lambdas, subln_gain):
    b, s, d = x.shape
    depth = norm_gains.shape[0]
    n_a = pool_w.shape[0]
    gains = norm_gains.reshape(depth, 3, 2, 1, d)
    wg, wu, wd = ffn_w_gate, ffn_w_up, ffn_w_down

    xf = x.reshape(b * s, d)
    k = v = bias_tiles = None
    for l in range(depth):
        g = gains[l]
        if l == n_a:
            k = _norm_proj(xf, kv_norm.reshape(1, d), w_k, 1.0)
            v = _norm_proj(xf, kv_norm.reshape(1, d), w_v, 1.0)
            bias_tiles = _bias_tiles(rel_bias, ATT_TILE)
        xf = _ffn(xf, g[0, 0], g[0, 1], wg, wu, wd, l, 0)
        if l < n_a:
            xf = _pool_mixer(xf.reshape(b, s, d), g[1, 0], g[1, 1], pool_w[l],
                             pool_scale[l].reshape(1, d)).reshape(b * s, d)
        else:
            j = l - n_a
            lambda_init = 0.8 - 0.6 * math.exp(-0.3 * l)
            q = _norm_proj(xf, g[1, 0], w_q[j], HEAD_DIM ** -0.5 * LOG2_E)
            o = _attention(q.reshape(b, s, -1), k.reshape(b, s, -1), v.reshape(b, s, -1), bias_tiles,
                           rel_bias, lambdas[j], subln_gain[j].reshape(1, -1), lambda_init)
            xf = _out_proj(o.reshape(b * s, -1), w_o[j], xf, g[1, 1])
        xf = _ffn(xf, g[2, 0], g[2, 1], wg, wu, wd, l, 1)
    return xf.reshape(b, s, d)
```

```python
import functools
import math

import numpy as np
import jax
import jax.numpy as jnp
from jax import lax
from jax.experimental import pallas as pl
from jax.experimental.pallas import tpu as pltpu

F32 = jnp.float32
BF16 = jnp.bfloat16

CHUNK = 64
POOL_WINDOWS = (2, 4, 8, 16)
HEAD_DIM = 128
V_HEAD_DIM = 2 * HEAD_DIM
NUM_BUCKETS = 32
MAX_DISTANCE = 128
RMS_EPS = 1e-6
SUBLN_EPS = 1e-5
LOG2_E = math.log2(math.e)

LANES = 128
SUBLANES = 8

VMEM_LIMIT_BYTES = 60 * 1024 * 1024
FFN_ROWS = 1024
FFN_COLS = 512
FFN_X_PREFETCH_STEP = 6
NORM_PROJ_ROWS = 1024
PROJ_ROWS = 512
PROJ_COLS = 1024
NORM_ROW_CHUNK = 16
POOL_ROWS = 512
POOL_STAGES = 4
POOL_HALO = SUBLANES * POOL_STAGES
ATT_TILE = 256


def _rms(x, gain, eps):
    ms = jnp.mean(x * x, axis=-1, keepdims=True)
    return x * lax.rsqrt(ms + eps) * gain


def _for_row_chunks(n_rows, body):
    for r0 in range(0, n_rows, NORM_ROW_CHUNK):
        body(slice(r0, r0 + NORM_ROW_CHUNK))


def _norm_rows_to(dst_ref, src_ref, gain, eps):
    def body(rows):
        dst_ref[rows, :] = _rms(src_ref[rows, :], gain, eps).astype(dst_ref.dtype)
    _for_row_chunks(src_ref.shape[0], body)


def _params(*semantics):
    return pltpu.CompilerParams(dimension_semantics=semantics,
                                vmem_limit_bytes=VMEM_LIMIT_BYTES)


def _ffn_kernel(x_hbm, gpre_ref, gpost_ref, wg_ref, wu_ref, wd_ref, o_hbm, h_ref, a_ref, acc_ref,
                x_buf, x_sem, o_sem):
    i = pl.program_id(0)
    j = pl.program_id(1)
    n_tiles = pl.num_programs(0)
    tm = acc_ref.shape[0]
    slot = i % 2
    x_ref = x_buf.at[slot]

    def x_copy(tile, to_slot):
        rows = pl.ds(pl.multiple_of(tile * tm, tm), tm)
        return pltpu.make_async_copy(x_hbm.at[rows, :], x_buf.at[to_slot], x_sem.at[to_slot])

    def out_copy(tile):
        rows = pl.ds(pl.multiple_of(tile * tm, tm), tm)
        return pltpu.make_async_copy(acc_ref, o_hbm.at[rows, :], o_sem.at[0])

    def gate_up():
        h = h_ref[...]
        g = jnp.dot(h, wg_ref[...].astype(BF16), preferred_element_type=F32)
        u = jnp.dot(h, wu_ref[...].astype(BF16), preferred_element_type=F32)
        return (g * (1.0 / (1.0 + jnp.exp(-g))) * u).astype(BF16)

    def down(a):
        return jnp.dot(a, wd_ref[...].astype(BF16), preferred_element_type=F32)

    @pl.when(jnp.logical_and(i == 0, j == 0))
    def _():
        x_copy(0, 0).start()

    @pl.when(j == 0)
    def _():
        x_copy(i, slot).wait()
        _norm_rows_to(h_ref, x_ref, gpre_ref[...], RMS_EPS)
        a_ref[...] = gate_up()

    @pl.when(jnp.logical_and(j == 0, i > 0))
    def _():
        out_copy(i - 1).wait()

    @pl.when(j == 0)
    def _():
        acc_ref[...] = down(a_ref[...])

    @pl.when(jnp.logical_and(j == FFN_X_PREFETCH_STEP, i + 1 < n_tiles))
    def _():
        x_copy(i + 1, 1 - slot).start()

    @pl.when(j > 0)
    def _():
        acc_ref[...] += down(gate_up())

    @pl.when(j == pl.num_programs(1) - 1)
    def _():
        half_gain = 0.5 * gpost_ref[...]

        def body(rows):
            acc_ref[rows, :] = x_ref[rows, :] + _rms(acc_ref[rows, :], half_gain, RMS_EPS)
        _for_row_chunks(tm, body)
        out_copy(i).start()

    @pl.when(jnp.logical_and(j == pl.num_programs(1) - 1, i == n_tiles - 1))
    def _():
        out_copy(i).wait()


def _ffn(x, gpre, gpost, wg, wu, wd, layer, half):
    m, d = x.shape
    ff = wg.shape[-1]
    tm, tf = FFN_ROWS, FFN_COLS
    assert m % tm == 0 and ff % tf == 0 and 0 < FFN_X_PREFETCH_STEP < ff // tf
    return pl.pallas_call(
        _ffn_kernel,
        grid=(m // tm, ff // tf),
        in_specs=[
            pl.BlockSpec(memory_space=pl.ANY),
            pl.BlockSpec((1, d), lambda i, j: (0, 0)),
            pl.BlockSpec((1, d), lambda i, j: (0, 0)),
            pl.BlockSpec((None, None, d, tf), lambda i, j: (layer, half, 0, j)),
            pl.BlockSpec((None, None, d, tf), lambda i, j: (layer, half, 0, j)),
            pl.BlockSpec((None, None, tf, d), lambda i, j: (layer, half, j, 0)),
        ],
        out_specs=pl.BlockSpec(memory_space=pl.ANY),
        out_shape=jax.ShapeDtypeStruct((m, d), F32),
        scratch_shapes=[pltpu.VMEM((tm, d), BF16), pltpu.VMEM((tm, tf), BF16),
                        pltpu.VMEM((tm, d), F32), pltpu.VMEM((2, tm, d), F32),
                        pltpu.SemaphoreType.DMA((2,)), pltpu.SemaphoreType.DMA((1,))],
        compiler_params=_params("arbitrary", "arbitrary"),
        name="ffn",
    )(x, gpre, gpost, wg, wu, wd)


def _pool_kernel(x_ref, halo_ref, gpre_ref, gpost_ref, w_ref, scale_ref, o_ref, wb_ref,
                 *sum_refs):
    i = pl.program_id(1)
    ts = x_ref.shape[1]
    d = x_ref.shape[2]
    gdim = w_ref.shape[1]
    rows_all = POOL_HALO + ts
    hx_ref = sum_refs[0]
    gpre = gpre_ref[...]

    @pl.when(jnp.logical_and(pl.program_id(0) == 0, i == 0))
    def _():
        wb_ref[...] = w_ref[...].astype(BF16)

    hh = _rms(halo_ref[0], gpre, RMS_EPS)
    hx_ref[0:POOL_HALO, :] = jnp.where(i > 0, hh, 0.0)
    for r0 in range(0, ts, NORM_ROW_CHUNK):
        hx_ref[POOL_HALO + r0:POOL_HALO + r0 + NORM_ROW_CHUNK, :] = _rms(
            x_ref[0, r0:r0 + NORM_ROW_CHUNK, :], gpre, RMS_EPS)

    for s in range(1, len(sum_refs)):
        shift = 2 ** (s - 1)
        r0 = SUBLANES * s
        c0 = (s - 1) * gdim
        prev = sum_refs[s - 1]
        sum_refs[s][r0:rows_all, c0:d] = (prev[r0:rows_all, c0:d]
                                          + prev[r0 - shift:rows_all - shift, c0:d])

    t = i * ts + lax.broadcasted_iota(jnp.int32, (ts, 1), 0)
    main = slice(POOL_HALO, rows_all)
    for gi, win in enumerate(POOL_WINDOWS):
        cols = slice(gi * gdim, (gi + 1) * gdim)
        stage = win.bit_length() - 1
        if stage < len(sum_refs):
            total = sum_refs[stage][main, cols]
        else:
            prev = sum_refs[stage - 1]
            half = win // 2
            total = prev[main, cols] + prev[POOL_HALO - half:rows_all - half, cols]
        inv_cnt = 1.0 / jnp.minimum(t + 1, win).astype(F32)
        dg = total * inv_cnt - hx_ref[main, cols]
        o_ref[0, :, cols] = (jnp.dot(dg.astype(BF16), wb_ref[gi], preferred_element_type=F32)
                             * scale_ref[:, cols])

    gpost = gpost_ref[...]
    for r0 in range(0, ts, NORM_ROW_CHUNK):
        rows = slice(r0, r0 + NORM_ROW_CHUNK)
        o_ref[0, rows, :] = x_ref[0, rows, :] + _rms(o_ref[0, rows, :], gpost, RMS_EPS)


def _pool_mixer(x, gpre, gpost, w, scale):
    b, s, d = x.shape
    ts = POOL_ROWS
    assert s % ts == 0 and ts % POOL_HALO == 0
    assert POOL_WINDOWS == tuple(2 ** (g + 1) for g in range(POOL_STAGES))
    assert w.shape[0] == len(POOL_WINDOWS) and w.shape[1] * w.shape[0] == d
    halo_blocks = ts // POOL_HALO
    return pl.pallas_call(
        _pool_kernel,
        grid=(b, s // ts),
        in_specs=[
            pl.BlockSpec((1, ts, d), lambda bi, i: (bi, i, 0)),
            pl.BlockSpec((1, POOL_HALO, d),
                         lambda bi, i: (bi, jnp.maximum(i * halo_blocks - 1, 0), 0)),
            pl.BlockSpec((1, d), lambda bi, i: (0, 0)),
            pl.BlockSpec((1, d), lambda bi, i: (0, 0)),
            pl.BlockSpec(w.shape, lambda bi, i: (0, 0, 0), pipeline_mode=pl.Buffered(1)),
            pl.BlockSpec((1, d), lambda bi, i: (0, 0)),
        ],
        out_specs=pl.BlockSpec((1, ts, d), lambda bi, i: (bi, i, 0)),
        out_shape=jax.ShapeDtypeStruct((b, s, d), F32),
        scratch_shapes=[pltpu.VMEM(w.shape, BF16)]
                       + [pltpu.VMEM((POOL_HALO + ts, d), F32)] * POOL_STAGES,
        compiler_params=_params("arbitrary", "arbitrary"),
        name="pool_mixer",
    )(x, x, gpre, gpost, w, scale)


def _norm_proj_kernel(out_scale, x_ref, g_ref, w_ref, o_ref, h_ref, wb_ref):
    @pl.when(pl.program_id(0) == 0)
    def _():
        wb_ref[...] = w_ref[...].astype(BF16)

    _norm_rows_to(h_ref, x_ref, g_ref[...], RMS_EPS)
    h = h_ref[...]
    for c0 in range(0, o_ref.shape[1], PROJ_COLS):
        cols = slice(c0, c0 + PROJ_COLS)
        y = jnp.dot(h, wb_ref[:, cols], preferred_element_type=F32)
        if out_scale != 1.0:
            y = y * out_scale
        o_ref[:, cols] = y.astype(o_ref.dtype)


def _norm_proj(x, gain, w, out_scale):
    m, d = x.shape
    n = w.shape[1]
    tm = NORM_PROJ_ROWS
    assert m % tm == 0 and n % PROJ_COLS == 0
    return pl.pallas_call(
        functools.partial(_norm_proj_kernel, out_scale),
        grid=(m // tm,),
        in_specs=[pl.BlockSpec((tm, d), lambda i: (i, 0)),
                  pl.BlockSpec((1, d), lambda i: (0, 0)),
                  pl.BlockSpec((d, n), lambda i: (0, 0), pipeline_mode=pl.Buffered(1))],
        out_specs=pl.BlockSpec((tm, n), lambda i: (i, 0)),
        out_shape=jax.ShapeDtypeStruct((m, n), BF16),
        scratch_shapes=[pltpu.VMEM((tm, d), BF16), pltpu.VMEM((d, n), BF16)],
        compiler_params=_params("arbitrary"),
        name="norm_proj",
    )(x, gain, w)


def _rel_bucket(rel):
    half = NUM_BUCKETS // 2
    max_exact = half // 2
    ret = jnp.where(rel > 0, half, 0)
    n = jnp.abs(rel)
    nf = jnp.maximum(n, 1).astype(F32)
    large = max_exact + (jnp.log(nf / max_exact) / math.log(MAX_DISTANCE / max_exact)
                         * (half - max_exact)).astype(jnp.int32)
    large = jnp.minimum(large, half - 1)
    return ret + jnp.where(n < max_exact, n, large)


FAR_BUCKET = NUM_BUCKETS // 2 - 1
FAR_DISTANCE = MAX_DISTANCE


def _bias_tile_constants(t):
    assert t + 1 >= FAR_DISTANCE and t % CHUNK == 0
    q = np.arange(t, dtype=np.int32)[:, None]
    k = np.arange(t, dtype=np.int32)[None, :]
    buckets = jnp.stack([_rel_bucket(jnp.asarray(k - q)), _rel_bucket(jnp.asarray(k - t - q))])
    allowed = np.stack([(k // CHUNK) <= (q // CHUNK), np.ones((t, t), bool)])
    mask = np.where(allowed, 0.0, -np.inf).astype(np.float32)
    return buckets, jnp.asarray(mask)


def _bias_kernel(table_ref, bucket_ref, mask_ref, o_ref):
    t = bucket_ref.shape[-1]
    table = jnp.broadcast_to(table_ref[0], (t, LANES))
    for tile in range(bucket_ref.shape[0]):
        for c0 in range(0, t, LANES):
            cols = slice(c0, c0 + LANES)
            looked_up = jnp.take_along_axis(table, bucket_ref[tile, :, cols], axis=1)
            o_ref[0, tile, :, cols] = looked_up * LOG2_E + mask_ref[tile, :, cols]


def _bias_tiles(rel_bias, t):
    buckets, mask = _bias_tile_constants(t)
    n_buckets, n_maps = rel_bias.shape
    assert n_buckets <= LANES and t % LANES == 0
    table = jnp.pad(rel_bias.T, ((0, 0), (0, LANES - n_buckets))).reshape(n_maps, 1, LANES)
    return pl.pallas_call(
        _bias_kernel,
        grid=(n_maps,),
        in_specs=[
            pl.BlockSpec((1, 1, LANES), lambda m: (m, 0, 0)),
            pl.BlockSpec((2, t, t), lambda m: (0, 0, 0)),
            pl.BlockSpec((2, t, t), lambda m: (0, 0, 0)),
        ],
        out_specs=pl.BlockSpec((1, 2, t, t), lambda m: (m, 0, 0, 0)),
        out_shape=jax.ShapeDtypeStruct((n_maps, 2, t, t), F32),
        compiler_params=_params("parallel"),
        name="bias_tiles",
    )(table, buckets, mask)


def _softmax_pv(qi, t, q_ref, k_ref, v_ref, cols, far_bias, bias_ref):
    qc = q_ref[0, qi * t:(qi + 1) * t, cols]
    near_bias, diag_bias = bias_ref.at[1], bias_ref.at[0]
    pieces = []
    if qi >= 2:
        pieces.append((0, (qi - 1) * t, None))
    if qi >= 1:
        pieces.append(((qi - 1) * t, t, near_bias))
    pieces.append((qi * t, t, diag_bias))

    scores, row_max = [], []
    for start, size, bias in pieces:
        kc = k_ref[0, start:start + size, cols]
        s = lax.dot_general(qc, kc, (((1,), (1,)), ((), ())), preferred_element_type=F32)
        if bias is None:
            row_max.append(jnp.max(s, axis=-1, keepdims=True) + far_bias)
        else:
            s = s + bias[...]
            row_max.append(jnp.max(s, axis=-1, keepdims=True))
        scores.append(s)
    m = functools.reduce(jnp.maximum, row_max)
    yield
    probs = []
    for s, (_, _, bias) in zip(scores, pieces):
        probs.append(jnp.exp2(s - ((m - far_bias) if bias is None else m)))
        if bias is None:
            yield
    denom = sum(jnp.sum(p, axis=-1, keepdims=True) for p in probs)
    yield
    acc = sum(jnp.dot(p.astype(BF16), v_ref[0, start:start + size, :], preferred_element_type=F32)
              for p, (start, size, _) in zip(probs, pieces))
    return acc * (1.0 / denom)


def _run_staggered(tasks, n_phases, on_done):
    finished = set()
    for step in range(len(tasks) + n_phases - 1):
        for phase in range(n_phases):
            n = step - phase
            if 0 <= n < len(tasks) and n not in finished:
                try:
                    next(tasks[n])
                except StopIteration as done:
                    finished.add(n)
                    on_done(n, done.value)


def _attn_kernel(lambda_init, q_ref, k_ref, v_ref, bias_ref, table_ref, lam_ref, subg_ref, o_ref):
    head = pl.program_id(0)
    t = bias_ref.shape[-1]

    lam = lam_ref[...]
    lam_full = (jnp.exp(jnp.sum(lam[0:1] * lam[1:2], keepdims=True))
                - jnp.exp(jnp.sum(lam[2:3] * lam[3:4], keepdims=True)) + lambda_init)

    tiles = list(reversed(range(q_ref.shape[1] // t)))
    tasks = []
    for i in tiles:
        for c in range(2):
            cols = slice(c * HEAD_DIM, (c + 1) * HEAD_DIM)
            tasks.append(_softmax_pv(i, t, q_ref, k_ref, v_ref, cols,
                                     table_ref[FAR_BUCKET, 2 * head + c] * LOG2_E, bias_ref.at[c]))

    first_map = {}

    def on_done(n, value):
        if n % 2 == 0:
            first_map[n // 2] = value
            return
        i = tiles[n // 2]
        o = first_map.pop(n // 2) - lam_full * value
        o = o * lax.rsqrt(jnp.mean(o * o, axis=-1, keepdims=True) + SUBLN_EPS)
        o_ref[0, i * t:(i + 1) * t, :] = (o * subg_ref[...] * (1.0 - lambda_init)).astype(o_ref.dtype)

    _run_staggered(tasks, 4, on_done)


def _attention(q, k, v, bias_tiles, rel_bias, lam, subln_g, lambda_init):
    b, s, width = q.shape
    t = ATT_TILE
    n_heads = width // V_HEAD_DIM
    assert s % t == 0
    return pl.pallas_call(
        functools.partial(_attn_kernel, lambda_init),
        grid=(n_heads, b),
        in_specs=[
            pl.BlockSpec((1, s, V_HEAD_DIM), lambda h, bi: (bi, 0, h)),
            pl.BlockSpec((1, s, V_HEAD_DIM), lambda h, bi: (bi, 0, h)),
            pl.BlockSpec((1, s, V_HEAD_DIM), lambda h, bi: (bi, 0, h)),
            pl.BlockSpec((2, 2, t, t), lambda h, bi: (h, 0, 0, 0)),
            pl.BlockSpec(memory_space=pltpu.SMEM),
            pl.BlockSpec(lam.shape, lambda h, bi: (0, 0)),
            pl.BlockSpec((1, V_HEAD_DIM), lambda h, bi: (0, 0)),
        ],
        out_specs=pl.BlockSpec((1, s, V_HEAD_DIM), lambda h, bi: (bi, 0, h)),
        out_shape=jax.ShapeDtypeStruct((b, s, width), BF16),
        compiler_params=_params("parallel", "parallel"),
        name="diff_attention",
    )(q, k, v, bias_tiles, rel_bias, lam, subln_g)


def _out_proj_kernel(o_ref, w_ref, x_ref, g_ref, y_ref, wb_ref):
    @pl.when(pl.program_id(0) == 0)
    def _():
        wb_ref[...] = w_ref[...].astype(BF16)

    y_ref[...] = jnp.dot(o_ref[...], wb_ref[...], preferred_element_type=F32)
    gain = g_ref[...]

    def body(rows):
        y_ref[rows, :] = x_ref[rows, :] + _rms(y_ref[rows, :], gain, RMS_EPS)
    _for_row_chunks(y_ref.shape[0], body)


def _out_proj(o, w, x, gain):
    m, d = x.shape
    kdim = o.shape[1]
    tm = PROJ_ROWS
    assert m % tm == 0
    return pl.pallas_call(
        _out_proj_kernel,
        grid=(m // tm,),
        in_specs=[
            pl.BlockSpec((tm, kdim), lambda i: (i, 0)),
            pl.BlockSpec((kdim, d), lambda i: (0, 0), pipeline_mode=pl.Buffered(1)),
            pl.BlockSpec((tm, d), lambda i: (i, 0)),
            pl.BlockSpec((1, d), lambda i: (0, 0)),
        ],
        out_specs=pl.BlockSpec((tm, d), lambda i: (i, 0)),
        out_shape=jax.ShapeDtypeStruct((m, d), F32),
        scratch_shapes=[pltpu.VMEM((kdim, d), BF16)],
        compiler_params=_params("arbitrary"),
        name="out_proj",
    )(o, w, x, gain)


def kernel(x, norm_gains, ffn_w_gate, ffn_w_up, ffn_w_down, pool_w, pool_scale, kv_norm, w_k, w_v,
           rel_bias, w_q, w_o, lambdas, subln_gain):
    b, s, d = x.shape
    depth = norm_gains.shape[0]
    n_a = pool_w.shape[0]
    gains = norm_gains.reshape(depth, 3, 2, 1, d)
    wg, wu, wd = ffn_w_gate, ffn_w_up, ffn_w_down

    xf = x.reshape(b * s, d)
    k = v = bias_tiles = None
    for l in range(depth):
        g = gains[l]
        if l == n_a:
            k = _norm_proj(xf, kv_norm.reshape(1, d), w_k, 1.0)
            v = _norm_proj(xf, kv_norm.reshape(1, d), w_v, 1.0)
            bias_tiles = _bias_tiles(rel_bias, ATT_TILE)
        xf = _ffn(xf, g[0, 0], g[0, 1], wg, wu, wd, l, 0)
        if l < n_a:
            xf = _pool_mixer(xf.reshape(b, s, d), g[1, 0], g[1, 1], pool_w[l],
                             pool_scale[l].reshape(1, d)).reshape(b * s, d)
        else:
            j = l - n_a
            lambda_init = 0.8 - 0.6 * math.exp(-0.3 * l)
            q = _norm_proj(xf, g[1, 0], w_q[j], HEAD_DIM ** -0.5 * LOG2_E)
            o = _attention(q.reshape(b, s, -1), k.reshape(b, s, -1), v.reshape(b, s, -1), bias_tiles,
                           rel_bias, lambdas[j], subln_gain[j].reshape(1, -1), lambda_init)
            xf = _out_proj(o.reshape(b * s, -1), w_o[j], xf, g[1, 1])
        xf = _ffn(xf, g[2, 0], g[2, 1], wg, wu, wd, l, 1)
    return xf.reshape(b, s, d)
```
